```python
import jax
import jax.numpy as jnp
from jax import lax
import numpy as np

D_MODEL = 1024
BATCH = 8
SEQ = 2048
DEPTH = 4

CTX_LEN = 256
GRID_W = 64
ROPE_THETA = 10000.0
NEG_INF = -1e30
EPS = 1e-6

MLA_HEADS = 8
MLA_Q_RANK = 384
MLA_KV_RANK = 256
MLA_NOPE = 64
MLA_ROPE = 32
MLA_V = 64
Q_BLOCK = 128

SWA_HEADS = 8
SWA_KV_HEADS = 2
SWA_HEAD_DIM = 64
WINDOW = 128
WIN_BLOCK = 128

GDN_HEADS = 8
GDN_DK = 64
GDN_DV = 64
GDN_CONV = 3
GDN_CHUNK = 64

FFN_DIM = 2816
FFN_CONV = 3

PROJ_SPLITS = (MLA_Q_RANK, MLA_KV_RANK, MLA_ROPE,
               SWA_HEADS * SWA_HEAD_DIM, SWA_KV_HEADS * SWA_HEAD_DIM, SWA_KV_HEADS * SWA_HEAD_DIM,
               GDN_HEADS * (2 * GDN_DK + GDN_DV), GDN_HEADS * GDN_DV, 2 * GDN_HEADS, 2 * GDN_HEADS,
               3 * D_MODEL)
PROJ_WIDTH = sum(PROJ_SPLITS)

kernel_name = 'hybrid_mla_swa_gdn_diffusion_block'


def rmsnorm(x, g):
    xf = x.astype(jnp.float32)
    y = xf * lax.rsqrt(jnp.mean(xf * xf, axis=-1, keepdims=True) + EPS)
    return (y * g.astype(jnp.float32)).astype(x.dtype)


def l2norm(x):
    xf = x.astype(jnp.float32)
    return xf * lax.rsqrt(jnp.sum(xf * xf, axis=-1, keepdims=True) + EPS)


def modulation(cvec, w_mod, b_mod):
    m = jax.nn.silu(cvec) @ w_mod + b_mod
    return jnp.split(m[..., None, :], 6, axis=-1)


def adaln(x, g, shift, scale):
    return rmsnorm(x, g) * (1 + scale) + shift


def dwconv(x, w):
    k = w.shape[0]
    return lax.conv_general_dilated(x, w[:, None, :].astype(x.dtype), window_strides=(1,),
                                    padding=[(k // 2, k // 2)],
                                    dimension_numbers=('NWC', 'WIO', 'NWC'),
                                    feature_group_count=x.shape[-1])


def axial_rope(x, rows, cols):
    d = x.shape[-1]
    da = d // 2
    half = da // 2
    inv = ROPE_THETA ** (-jnp.arange(half, dtype=jnp.float32) / half)

    def rot(xa, pos):
        ang = pos.astype(jnp.float32)[:, None] * inv[None, :]
        cos = jnp.cos(ang)[None, :, None, :]
        sin = jnp.sin(ang)[None, :, None, :]
        x1 = xa[..., :half].astype(jnp.float32)
        x2 = xa[..., half:].astype(jnp.float32)
        return jnp.concatenate([x1 * cos - x2 * sin, x1 * sin + x2 * cos], axis=-1)

    return jnp.concatenate([rot(x[..., :da], rows), rot(x[..., da:], cols)], axis=-1).astype(x.dtype)


def split_proj(u):
    idx = np.cumsum(PROJ_SPLITS)[:-1].tolist()
    return jnp.split(u, idx, axis=-1)


def flat_heads(t):
    return t.reshape(t.shape[0], t.shape[1], -1)


def attend(q, k, v, scale, sink=None):
    B, Lq, H, d = q.shape
    KV = k.shape[2]
    G = H // KV
    qg = q.reshape(B, Lq, KV, G, d)
    s = jnp.einsum('bqkgd,bjkd->bkgqj', qg, k).astype(jnp.float32) * scale
    if sink is not None:
        sk = jnp.broadcast_to(sink.astype(jnp.float32).reshape(1, KV, G, 1, 1), s.shape[:-1] + (1,))
        p = jax.nn.softmax(jnp.concatenate([sk, s], axis=-1), axis=-1)[..., 1:]
    else:
        p = jax.nn.softmax(s, axis=-1)
    o = jnp.einsum('bkgqj,bjkd->bqkgd', p.astype(v.dtype), v)
    return o.reshape(B, Lq, H, v.shape[-1])


def blockwise_attend(q, k, v, scale):
    B, S, H, d = q.shape
    nb = S // Q_BLOCK
    qb = jnp.moveaxis(q.reshape(B, nb, Q_BLOCK, H, d), 1, 0)
    o = lax.map(lambda qq: attend(qq, k, v, scale), qb)
    return jnp.moveaxis(o, 0, 1).reshape(B, S, H, v.shape[-1])


def window_attend(q, k, v, k_ctx, v_ctx, sink, scale):
    B, S, H, d = q.shape
    KV = k.shape[2]
    G = H // KV
    W = WIN_BLOCK
    nb = S // W
    C = k_ctx.shape[1]
    qb = q.reshape(B, nb, W, KV, G, d)

    def band(t):
        tp = jnp.pad(t, ((0, 0), (W, W), (0, 0), (0, 0))).reshape(B, nb + 2, W, KV, t.shape[-1])
        return jnp.concatenate([tp[:, :-2], tp[:, 1:-1], tp[:, 2:]], axis=2)

    kb, vb = band(k), band(v)
    s_loc = jnp.einsum('bnqkgd,bnjkd->bnkgqj', qb, kb).astype(jnp.float32) * scale
    s_ctx = jnp.einsum('bnqkgd,bckd->bnkgqc', qb, k_ctx).astype(jnp.float32) * scale
    blk = jnp.arange(nb)[:, None, None] * W
    qpos = blk + jnp.arange(W)[None, :, None]
    kpos = blk - W + jnp.arange(3 * W)[None, None, :]
    valid = (jnp.abs(kpos - qpos) <= WINDOW) & (kpos >= 0) & (kpos < S)
    s_loc = jnp.where(valid[None, :, None, None], s_loc, NEG_INF)
    sk = jnp.broadcast_to(sink.astype(jnp.float32).reshape(1, 1, KV, G, 1, 1), s_loc.shape[:-1] + (1,))
    p = jax.nn.softmax(jnp.concatenate([sk, s_ctx, s_loc], axis=-1), axis=-1)
    p_ctx = p[..., 1:1 + C].astype(v.dtype)
    p_loc = p[..., 1 + C:].astype(v.dtype)
    o = (jnp.einsum('bnkgqc,bckd->bnqkgd', p_ctx, v_ctx)
         + jnp.einsum('bnkgqj,bnjkd->bnqkgd', p_loc, vb))
    return o.reshape(B, S, H, d)


def mla_heads(cq, ckv, kr, q_norm, kv_norm, w_uq, w_ukv, rows, cols):
    B, L = cq.shape[:2]
    q = (rmsnorm(cq, q_norm) @ w_uq).reshape(B, L, MLA_HEADS, MLA_NOPE + MLA_ROPE)
    kv = (rmsnorm(ckv, kv_norm) @ w_ukv).reshape(B, L, MLA_HEADS, MLA_NOPE + MLA_V)
    q_nope, q_rope = q[..., :MLA_NOPE], q[..., MLA_NOPE:]
    k_rope = kr[:, :, None, :]
    if rows is not None:
        q_rope = axial_rope(q_rope, rows, cols)
        k_rope = axial_rope(k_rope, rows, cols)
    k_rope = jnp.broadcast_to(k_rope, (B, L, MLA_HEADS, MLA_ROPE))
    q = jnp.concatenate([q_nope, q_rope], axis=-1)
    k = jnp.concatenate([kv[..., :MLA_NOPE], k_rope], axis=-1)
    return q, k, kv[..., MLA_NOPE:]


def swa_heads(q, k, v, rows, cols):
    B, L = q.shape[:2]
    q = q.reshape(B, L, SWA_HEADS, SWA_HEAD_DIM)
    k = k.reshape(B, L, SWA_KV_HEADS, SWA_HEAD_DIM)
    v = v.reshape(B, L, SWA_KV_HEADS, SWA_HEAD_DIM)
    if rows is not None:
        q = axial_rope(q, rows, cols)
        k = axial_rope(k, rows, cols)
    return q, k, v


def gdn_prep(qkv, conv_w):
    B, L, _ = qkv.shape
    y = jax.nn.silu(dwconv(qkv, conv_w))
    q, k, v = jnp.split(y, [GDN_HEADS * GDN_DK, 2 * GDN_HEADS * GDN_DK], axis=-1)
    q = l2norm(q.reshape(B, L, GDN_HEADS, GDN_DK)) * (GDN_DK ** -0.5)
    k = l2norm(k.reshape(B, L, GDN_HEADS, GDN_DK))
    return q, k, v.reshape(B, L, GDN_HEADS, GDN_DV)


def gdn_gates(a, b, a_log, dt_bias):
    B, L = a.shape[:2]
    a = a.astype(jnp.float32).reshape(B, L, 2, GDN_HEADS)
    b = b.astype(jnp.float32).reshape(B, L, 2, GDN_HEADS)
    g = -jnp.exp(a_log.astype(jnp.float32)) * jax.nn.softplus(a + dt_bias.astype(jnp.float32))
    return g, jax.nn.sigmoid(b)


def gated_delta_chunked(q, k, v, g, beta, s0):
    B, L, H, _ = q.shape
    C = GDN_CHUNK
    n = L // C

    def chunks(t):
        return t.astype(jnp.float32).reshape(B, n, C, H, t.shape[-1]).transpose(1, 0, 3, 2, 4)

    q, k, v = chunks(q), chunks(k), chunks(v)
    g = g.reshape(B, n, C, H).transpose(1, 0, 3, 2)
    beta = beta.reshape(B, n, C, H).transpose(1, 0, 3, 2)
    gam = jnp.cumsum(g, axis=-1)
    decay = jnp.exp(jnp.minimum(gam[..., :, None] - gam[..., None, :], 0.0))
    strict = jnp.tril(jnp.ones((C, C), dtype=bool), -1)
    incl = jnp.tril(jnp.ones((C, C), dtype=bool))
    kb = k * beta[..., None]
    a_mat = jnp.where(strict, jnp.einsum('nbhid,nbhjd->nbhij', kb, k) * decay, 0.0)
    t_mat = a_mat + jnp.eye(C, dtype=jnp.float32)
    rhs = jnp.concatenate([v * beta[..., None], kb * jnp.exp(gam)[..., None]], axis=-1)
    wy = lax.linalg.triangular_solve(t_mat, rhs, left_side=True, lower=True, unit_diagonal=True)
    u, w = wy[..., :GDN_DV], wy[..., GDN_DV:]
    qk = jnp.where(incl, jnp.einsum('nbhid,nbhjd->nbhij', q, k) * decay, 0.0)

    def step(S, xs):
        q_c, k_c, u_c, w_c, qk_c, gam_c = xs
        v_new = u_c - jnp.einsum('bhck,bhkv->bhcv', w_c, S)
        o = (jnp.einsum('bhck,bhkv->bhcv', q_c * jnp.exp(gam_c)[..., None], S)
             + jnp.einsum('bhcj,bhjv->bhcv', qk_c, v_new))
        g_last = gam_c[..., -1:]
        S = (S * jnp.exp(g_last)[..., None]
             + jnp.einsum('bhck,bhcv->bhkv', k_c * jnp.exp(g_last - gam_c)[..., None], v_new))
        return S, o

    s_fin, o = lax.scan(step, s0, (q, k, u, w, qk, gam))
    return o.transpose(1, 0, 3, 2, 4).reshape(B, L, H, GDN_DV), s_fin


def gdn_bidir(q, k, v, g, beta, s0_f, s0_b):
    o_f, s_f = gated_delta_chunked(q, k, v, g[..., 0, :], beta[..., 0, :], s0_f)
    rev = lambda t: jnp.flip(t, axis=1)
    o_b, s_b = gated_delta_chunked(rev(q), rev(k), rev(v), rev(g[..., 1, :]), rev(beta[..., 1, :]), s0_b)
    return o_f + rev(o_b), s_f, s_b


def gdn_out(o, z, o_norm):
    B, L = z.shape[:2]
    y = rmsnorm(o, o_norm) * jax.nn.silu(z.reshape(B, L, GDN_HEADS, GDN_DV).astype(jnp.float32))
    return y.reshape(B, L, GDN_HEADS * GDN_DV).astype(z.dtype)


def merge(gates, ya, yb, yc, w_pa, w_pb, w_pc, w_out):
    ga, gb, gc = jnp.split(jax.nn.sigmoid(gates), 3, axis=-1)
    return (ga * (ya @ w_pa) + gb * (yb @ w_pb) + gc * (yc @ w_pc)) @ w_out


def token_mixer(xn_lat, xn_ctx, rows, cols, w_in, q_norm, kv_norm, w_uq, w_ukv, sink,
                conv_w, a_log, dt_bias, o_norm, w_pa, w_pb, w_pc, w_out, with_ctx_out):
    B = xn_lat.shape[0]
    (cq_l, ckv_l, kr_l, sq_l, sk_l, sv_l, qkv_l, z_l, a_l, b_l, gate_l) = split_proj(xn_lat @ w_in)
    (cq_c, ckv_c, kr_c, sq_c, sk_c, sv_c, qkv_c, z_c, a_c, b_c, gate_c) = split_proj(xn_ctx @ w_in)
    scale_a = (MLA_NOPE + MLA_ROPE) ** -0.5
    scale_b = SWA_HEAD_DIM ** -0.5

    qa_l, ka_l, va_l = mla_heads(cq_l, ckv_l, kr_l, q_norm, kv_norm, w_uq, w_ukv, rows, cols)
    qa_c, ka_c, va_c = mla_heads(cq_c, ckv_c, kr_c, q_norm, kv_norm, w_uq, w_ukv, None, None)
    ya_l = blockwise_attend(qa_l, jnp.concatenate([ka_c, ka_l], axis=1),
                            jnp.concatenate([va_c, va_l], axis=1), scale_a)

    qb_l, kb_l, vb_l = swa_heads(sq_l, sk_l, sv_l, rows, cols)
    qb_c, kb_c, vb_c = swa_heads(sq_c, sk_c, sv_c, None, None)
    yb_l = window_attend(qb_l, kb_l, vb_l, kb_c, vb_c, sink, scale_b)

    zeros = jnp.zeros((B, GDN_HEADS, GDN_DK, GDN_DV), jnp.float32)
    q_c, k_c, v_c = gdn_prep(qkv_c, conv_w)
    g_c, beta_c = gdn_gates(a_c, b_c, a_log, dt_bias)
    oc_c, s_f, s_b = gdn_bidir(q_c, k_c, v_c, g_c, beta_c, zeros, zeros)
    q_l, k_l, v_l = gdn_prep(qkv_l, conv_w)
    g_l, beta_l = gdn_gates(a_l, b_l, a_log, dt_bias)
    oc_l, _, _ = gdn_bidir(q_l, k_l, v_l, g_l, beta_l, s_f, s_b)

    y_lat = merge(gate_l, flat_heads(ya_l), flat_heads(yb_l), gdn_out(oc_l, z_l, o_norm),
                  w_pa, w_pb, w_pc, w_out)
    if not with_ctx_out:
        return y_lat, None
    ya_c = attend(qa_c, ka_c, va_c, scale_a)
    yb_c = attend(qb_c, kb_c, vb_c, scale_b, sink)
    y_ctx = merge(gate_c, flat_heads(ya_c), flat_heads(yb_c), gdn_out(oc_c, z_c, o_norm),
                  w_pa, w_pb, w_pc, w_out)
    return y_lat, y_ctx


def conv_ffn(xn, w_up, w_conv, w_down):
    h = dwconv(xn @ w_up, w_conv)
    a, b = jnp.split(h, 2, axis=-1)
    return (jax.nn.silu(a) * b) @ w_down


def setup_inputs(seed: int = 0) -> dict:
    key = jax.random.key(seed)
    ks = jax.random.split(key, 32)
    D = D_MODEL

    def nrm(k, shape, s):
        return jax.random.normal(k, shape, jnp.float32) * s

    dt = jnp.exp(jax.random.uniform(ks[14], (DEPTH, 2, GDN_HEADS), jnp.float32,
                                    float(np.log(1e-3)), float(np.log(1e-1))))
    return {
        'x': nrm(ks[0], (BATCH, SEQ, D), 1.0),
        'c': nrm(ks[1], (BATCH, D), 1.0),
        'ctx': nrm(ks[2], (BATCH, CTX_LEN, D), 1.0),
        'c_ctx': nrm(ks[3], (D,), 1.0),
        'w_mod': nrm(ks[4], (DEPTH, D, 6 * D), 0.5 * D ** -0.5),
        'b_mod': nrm(ks[5], (DEPTH, 6 * D), 0.02),
        'norm1': 1.0 + nrm(ks[6], (DEPTH, D), 0.05),
        'norm2': 1.0 + nrm(ks[7], (DEPTH, D), 0.05),
        'w_in': nrm(ks[8], (DEPTH, D, PROJ_WIDTH), D ** -0.5),
        'mla_q_norm': 1.0 + nrm(ks[9], (DEPTH, MLA_Q_RANK), 0.05),
        'mla_kv_norm': 1.0 + nrm(ks[10], (DEPTH, MLA_KV_RANK), 0.05),
        'w_uq': nrm(ks[11], (DEPTH, MLA_Q_RANK, MLA_HEADS * (MLA_NOPE + MLA_ROPE)), MLA_Q_RANK ** -0.5),
        'w_ukv': nrm(ks[12], (DEPTH, MLA_KV_RANK, MLA_HEADS * (MLA_NOPE + MLA_V)), MLA_KV_RANK ** -0.5),
        'swa_sink': nrm(ks[13], (DEPTH, SWA_HEADS), 0.5),
        'gdn_conv': nrm(ks[15], (DEPTH, GDN_CONV, GDN_HEADS * (2 * GDN_DK + GDN_DV)), GDN_CONV ** -0.5),
        'gdn_a_log': jnp.log(jax.random.uniform(ks[16], (DEPTH, 2, GDN_HEADS), jnp.float32, 1.0, 16.0)),
        'gdn_dt_bias': jnp.log(jnp.expm1(dt)),
        'gdn_norm': 1.0 + nrm(ks[17], (DEPTH, GDN_DV), 0.05),
        'w_branch_a': nrm(ks[18], (DEPTH, MLA_HEADS * MLA_V, D), (MLA_HEADS * MLA_V) ** -0.5),
        'w_branch_b': nrm(ks[19], (DEPTH, SWA_HEADS * SWA_HEAD_DIM, D), (SWA_HEADS * SWA_HEAD_DIM) ** -0.5),
        'w_branch_c': nrm(ks[20], (DEPTH, GDN_HEADS * GDN_DV, D), (GDN_HEADS * GDN_DV) ** -0.5),
        'w_out': nrm(ks[21], (DEPTH, D, D), D ** -0.5),
        'ffn_up': nrm(ks[22], (DEPTH, D, 2 * FFN_DIM), D ** -0.5),
        'ffn_conv': nrm(ks[23], (DEPTH, FFN_CONV, 2 * FFN_DIM), FFN_CONV ** -0.5),
        'ffn_down': nrm(ks[24], (DEPTH, FFN_DIM, D), FFN_DIM ** -0.5),
        'norm_f': 1.0 + nrm(ks[25], (D,), 0.05),
    }


def reference(x, c, ctx, c_ctx, w_mod, b_mod, norm1, norm2, w_in, mla_q_norm, mla_kv_norm,
              w_uq, w_ukv, swa_sink, gdn_conv, gdn_a_log, gdn_dt_bias, gdn_norm,
              w_branch_a, w_branch_b, w_branch_c, w_out, ffn_up, ffn_conv, ffn_down, norm_f):
    S = x.shape[1]
    ROWS = S // GRID_W
    rows = jnp.repeat(jnp.arange(ROWS, dtype=jnp.int32), GRID_W)
    cols = jnp.tile(jnp.arange(GRID_W, dtype=jnp.int32), ROWS)
    h = ctx
    for l in range(DEPTH):
        last = l == DEPTH - 1
        m_lat = modulation(c, w_mod[l], b_mod[l])
        m_ctx = modulation(c_ctx, w_mod[l], b_mod[l])
        xn = adaln(x, norm1[l], m_lat[0], m_lat[1])
        hn = adaln(h, norm1[l], m_ctx[0], m_ctx[1])
        y_lat, y_ctx = token_mixer(xn, hn, rows, cols, w_in[l], mla_q_norm[l], mla_kv_norm[l],
                                   w_uq[l], w_ukv[l], swa_sink[l], gdn_conv[l], gdn_a_log[l],
                                   gdn_dt_bias[l], gdn_norm[l], w_branch_a[l], w_branch_b[l],
                                   w_branch_c[l], w_out[l], not last)
        x = x + m_lat[2] * y_lat
        x = x + m_lat[5] * conv_ffn(adaln(x, norm2[l], m_lat[3], m_lat[4]), ffn_up[l], ffn_conv[l], ffn_down[l])
        if not last:
            h = h + m_ctx[2] * y_ctx
            h = h + m_ctx[5] * conv_ffn(adaln(h, norm2[l], m_ctx[3], m_ctx[4]), ffn_up[l], ffn_conv[l], ffn_down[l])
    return rmsnorm(x, norm_f)
```

```python
import functools

import numpy as np
import jax
import jax.numpy as jnp
from jax import lax
from jax.experimental import pallas as pl
from jax.experimental.pallas import tpu as pltpu

F32 = jnp.float32
BF16 = jnp.bfloat16
HIGHEST = lax.Precision.HIGHEST

GRID_W = 64
ROPE_THETA = 10000.0
NEG_INF = -1e30
EPS = 1e-6

MLA_HEADS = 8
MLA_Q_RANK = 384
MLA_KV_RANK = 256
MLA_NOPE = 64
MLA_ROPE = 32
MLA_V = 64

SWA_HEADS = 8
SWA_KV_HEADS = 2
SWA_HEAD_DIM = 64
WINDOW = 128

GDN_HEADS = 8
GDN_DK = 64
GDN_DV = 64
GDN_CHUNK = 64

FFN_DIM = 2816

LANES = 128
SUBLANES = 8
VMEM_LIMIT_BYTES = 56 * 1024 * 1024

TOKEN_TILE = 256
SWA_TILE = 128
PACK = 4
PACK_W = PACK * GDN_DK
FFN_SPLIT = 2


def _sigmoid(x):
    return 1.0 / (1.0 + jnp.exp(-x))


def _silu(x):
    return x * _sigmoid(x)


def _softplus(x):
    return jnp.maximum(x, 0.0) + jnp.log(1.0 + jnp.exp(-jnp.abs(x)))


def _rms(x, g):
    return x * lax.rsqrt(jnp.mean(x * x, axis=-1, keepdims=True) + EPS) * g


def _mm(a, b):
    return jnp.dot(a.astype(BF16), b.astype(BF16), preferred_element_type=F32)


def _mm_nt(a, b):
    return lax.dot_general(a.astype(BF16), b.astype(BF16), (((1,), (1,)), ((), ())),
                           preferred_element_type=F32)


def _mm_tn(a, b):
    return lax.dot_general(a.astype(BF16), b.astype(BF16), (((0,), (0,)), ((), ())),
                           preferred_element_type=F32)


def _mm_f32(a, b):
    return jnp.dot(a, b, precision=HIGHEST, preferred_element_type=F32)


def _iota(shape, dim):
    return lax.broadcasted_iota(jnp.int32, shape, dim)


def _group_ones(n, group):
    sh = group.bit_length() - 1
    return (jnp.right_shift(_iota((n, n), 0), sh) == jnp.right_shift(_iota((n, n), 1), sh)).astype(F32)


def _rope(x, c, sa, sb, half):
    n = x.shape[-1]
    return x * c + pltpu.roll(x, n - half, 1) * sa + pltpu.roll(x, half, 1) * sb


def _cparams(sem):
    return pltpu.CompilerParams(dimension_semantics=sem, vmem_limit_bytes=VMEM_LIMIT_BYTES)


def _mod_kernel(c_ref, w_ref, b_ref, o_ref):
    s = _silu(c_ref[...])
    o_ref[0] = _mm(s, w_ref[0]) + b_ref[0]


def _modulation(cc, w_mod, b_mod):
    depth, d, n = w_mod.shape
    r = cc.shape[0]
    tn = 1024
    return pl.pallas_call(
        _mod_kernel,
        grid=(depth, n // tn),
        in_specs=[pl.BlockSpec((r, d), lambda l, j: (0, 0)),
                  pl.BlockSpec((1, d, tn), lambda l, j: (l, 0, j)),
                  pl.BlockSpec((1, 1, tn), lambda l, j: (l, 0, j))],
        out_specs=pl.BlockSpec((1, r, tn), lambda l, j: (l, 0, j)),
        out_shape=jax.ShapeDtypeStruct((depth, r, n), F32),
        compiler_params=_cparams(("parallel", "parallel")),
        name="modulation",
    )(cc, w_mod, b_mod.reshape(depth, 1, n))


_P_CQ = (0, 384)
_P_CKV = (384, 640)
_P_KR = (640, 768)
_P_SQ = (768, 1280)
_P_SK = (1280, 1408)
_P_SV = (1408, 1536)
_P_QKV = (1536, 3072)
_P_Z = (3072, 3584)
_P_AB = (3584, 3712)
_P_GATE = (3712, 6784)
_P_WIDTH = 6784


def _proj_kernel(x_ref, mod_ref, g1_ref, win_ref, qn_ref, kvn_ref, wuq_ref, wukvk_ref, wukvv_ref,
                 mc_ref, ma_ref, mb_ref, sc_ref, sa_ref, sb_ref,
                 qm_ref, km_ref, vm_ref, sq_ref, sk_ref, sv_ref, qkv_ref, z_ref, ab_ref, gate_ref):
    x = x_ref[0]
    d = x.shape[-1]
    m = mod_ref[0]
    xb = (_rms(x, g1_ref[...]) * (1.0 + m[:, d:2 * d]) + m[:, 0:d]).astype(BF16)

    def proj(piece):
        return jnp.dot(xb, win_ref[:, piece[0]:piece[1]], preferred_element_type=F32)

    mc, ma, mb = mc_ref[...], ma_ref[...], mb_ref[...]
    sc, sa, sb = sc_ref[...], sa_ref[...], sb_ref[...]

    cq = _rms(proj(_P_CQ), qn_ref[...])
    q = _mm(cq, wuq_ref[...]) * ((MLA_NOPE + MLA_ROPE) ** -0.5)
    ckv = _rms(proj(_P_CKV), kvn_ref[...])
    kn = _mm(ckv, wukvk_ref[...])
    vm_ref[0] = _mm(ckv, wukvv_ref[...]).astype(vm_ref.dtype)
    kr = _rope(proj(_P_KR), mc, ma, mb, MLA_ROPE // 4)
    for h in range(MLA_HEADS):
        sl = slice(h * LANES, (h + 1) * LANES)
        qm_ref[0, :, sl] = _rope(q[:, sl], mc, ma, mb, MLA_ROPE // 4).astype(qm_ref.dtype)
        km_ref[0, :, sl] = (kn[:, sl] + kr).astype(km_ref.dtype)

    sq = proj(_P_SQ) * (SWA_HEAD_DIM ** -0.5)
    for g in range(SWA_HEADS * SWA_HEAD_DIM // LANES):
        sl = slice(g * LANES, (g + 1) * LANES)
        sq_ref[0, :, sl] = _rope(sq[:, sl], sc, sa, sb, SWA_HEAD_DIM // 4).astype(sq_ref.dtype)
    sk_ref[0] = _rope(proj(_P_SK), sc, sa, sb, SWA_HEAD_DIM // 4).astype(sk_ref.dtype)
    sv_ref[0] = proj(_P_SV).astype(sv_ref.dtype)

    qkv_ref[0] = proj(_P_QKV)
    z_ref[0] = proj(_P_Z)
    ab_ref[0] = proj(_P_AB)
    gate_ref[0] = proj(_P_GATE)


def _proj(xs, mod_l, g1, win, qn, kvn, wuq, wukvk, wukvv, rope_tabs, nctt, ctx_row):
    b, t, d = xs.shape
    tm = TOKEN_TILE
    nt = t // tm

    def tok(w):
        return pl.BlockSpec((1, tm, w), lambda i, j: (i, j, 0))

    def full(a):
        return pl.BlockSpec(a.shape, lambda i, j: (0,) * a.ndim)

    tab = pl.BlockSpec((tm, LANES), lambda i, j: (j, 0))
    mod_spec = pl.BlockSpec((1, 1, mod_l.shape[-1]),
                            lambda i, j: (jnp.where(j < nctt, ctx_row, i), 0, 0))
    widths = (1024, 1024, 512, 512, 128, 128, 1536, 512, 128, 3072)
    dtypes = (BF16, BF16, BF16, BF16, BF16, BF16, F32, F32, F32, F32)
    return pl.pallas_call(
        _proj_kernel,
        grid=(b, nt),
        in_specs=[tok(d), mod_spec, full(g1), full(win), full(qn), full(kvn), full(wuq),
                  full(wukvk), full(wukvv)] + [tab] * 6,
        out_specs=[tok(w) for w in widths],
        out_shape=[jax.ShapeDtypeStruct((b, t, w), dt) for w, dt in zip(widths, dtypes)],
        compiler_params=_cparams(("parallel", "parallel")),
        name="proj",
    )(xs, mod_l, g1, win, qn, kvn, wuq, wukvk, wukvv, *rope_tabs)


def _mla_kernel(q_ref, k_ref, v_ref, o_ref, *, n_ctx, q_off):
    i = pl.program_id(2) + q_off
    tq = q_ref.shape[1]
    nctt = n_ctx // tq
    low = _iota((tq, LANES), 1) < MLA_V

    def run(nk):
        v = v_ref[0, 0:nk, :]
        outs = []
        for hh in range(2):
            q = q_ref[0, :, hh * LANES:(hh + 1) * LANES]
            k = k_ref[0, 0:nk, hh * LANES:(hh + 1) * LANES]
            s = _mm_nt(q, k)
            p = jnp.exp(s - jnp.max(s, axis=-1, keepdims=True))
            l = jnp.sum(p, axis=-1, keepdims=True)
            outs.append(_mm(p, v) / l)
        o_ref[0] = jnp.where(low, outs[0], outs[1]).astype(o_ref.dtype)

    if q_off < nctt:
        @pl.when(i < nctt)
        def _():
            run(n_ctx)

    @pl.when(i >= nctt)
    def _():
        run(k_ref.shape[1])


def _mla(qm, km, vm, n_ctx, q_off):
    b, t, _ = qm.shape
    tq = TOKEN_TILE
    nt = t // tq - q_off
    return pl.pallas_call(
        functools.partial(_mla_kernel, n_ctx=n_ctx, q_off=q_off),
        grid=(b, MLA_HEADS // 2, nt),
        in_specs=[pl.BlockSpec((1, tq, 2 * LANES), lambda i, j, n: (i, n + q_off, j)),
                  pl.BlockSpec((1, t, 2 * LANES), lambda i, j, n: (i, 0, j)),
                  pl.BlockSpec((1, t, LANES), lambda i, j, n: (i, 0, j))],
        out_specs=pl.BlockSpec((1, tq, LANES), lambda i, j, n: (i, n + q_off, j)),
        out_shape=jax.ShapeDtypeStruct((b, t, MLA_HEADS * MLA_V), BF16),
        compiler_params=_cparams(("parallel", "parallel", "parallel")),
        name="mla",
    )(qm, km, vm)


def _swa_kernel(sink_ref, q_ref, k_ref, v_ref, o_ref, *, n_ctx, q_off):
    i = pl.program_id(1) + q_off
    blk = q_ref.shape[1]
    t = k_ref.shape[1]
    n_lat = t - n_ctx
    nctt = n_ctx // blk
    low = _iota((blk, LANES), 1) < SWA_HEAD_DIM
    n_slab = SWA_HEADS * SWA_HEAD_DIM // LANES
    kc = k_ref[0, 0:n_ctx, :]
    vc = v_ref[0, 0:n_ctx, :]

    def attend(loc):
        for g in range(n_slab):
            qs = q_ref[0, :, g * LANES:(g + 1) * LANES]
            outs = []
            for hh in range(2):
                qh = jnp.where(low if hh == 0 else jnp.logical_not(low), qs, jnp.zeros_like(qs))
                head = hh * n_slab + g
                sink = sink_ref[head:head + 1, 0:1]
                s_c = _mm_nt(qh, kc)
                mx = jnp.maximum(jnp.max(s_c, axis=-1, keepdims=True), sink)
                if loc is not None:
                    kl, vl, valid = loc
                    s_l = jnp.where(valid, _mm_nt(qh, kl), NEG_INF)
                    mx = jnp.maximum(mx, jnp.max(s_l, axis=-1, keepdims=True))
                p_c = jnp.exp(s_c - mx)
                den = jnp.sum(p_c, axis=-1, keepdims=True) + jnp.exp(sink - mx)
                acc = _mm(p_c, vc)
                if loc is not None:
                    p_l = jnp.exp(s_l - mx)
                    den = den + jnp.sum(p_l, axis=-1, keepdims=True)
                    acc = acc + _mm(p_l, vl)
                outs.append(acc / den)
            o_ref[0, :, g * LANES:(g + 1) * LANES] = jnp.where(low, outs[0], outs[1]).astype(o_ref.dtype)

    if q_off < nctt:
        @pl.when(i < nctt)
        def _():
            attend(None)

    @pl.when(i >= nctt)
    def _():
        n = i - nctt
        span = 3 * blk
        start = jnp.clip((n - 1) * blk, 0, n_lat - span)
        row0 = pl.multiple_of(n_ctx + start, blk)
        kl = k_ref[0, pl.ds(row0, span), :]
        vl = v_ref[0, pl.ds(row0, span), :]
        kpos = start + _iota((blk, span), 1)
        qpos = n * blk + _iota((blk, span), 0)
        valid = jnp.abs(kpos - qpos) <= WINDOW
        attend((kl, vl, valid))


def _swa(sink, sq, sk, sv, n_ctx, q_off):
    b, t, w = sq.shape
    blk = SWA_TILE
    nt = t // blk - q_off
    return pl.pallas_call(
        functools.partial(_swa_kernel, n_ctx=n_ctx, q_off=q_off),
        grid=(b, nt),
        in_specs=[pl.BlockSpec(sink.shape, lambda i, n: (0, 0)),
                  pl.BlockSpec((1, blk, w), lambda i, n: (i, n + q_off, 0)),
                  pl.BlockSpec((1, t, LANES), lambda i, n: (i, 0, 0)),
                  pl.BlockSpec((1, t, LANES), lambda i, n: (i, 0, 0))],
        out_specs=pl.BlockSpec((1, blk, w), lambda i, n: (i, n + q_off, 0)),
        out_shape=jax.ShapeDtypeStruct((b, t, w), BF16),
        compiler_params=_cparams(("parallel", "parallel")),
        name="swa",
    )(sink, sq, sk, sv)


def _block_diag(x, bd_mask):
    return jnp.where(bd_mask, jnp.concatenate([x] * PACK, axis=0), 0.0)


def _unit_tri_inverse(a, eye, bd_mask):
    n = -a
    t = eye + n
    p = _mm_f32(n, _block_diag(n, bd_mask))
    steps = GDN_CHUNK.bit_length() - 2
    for _ in range(steps - 1):
        r = _block_diag(p, bd_mask)
        t = t + _mm_f32(t, r)
        p = _mm_f32(p, r)
    return t + _mm_f32(t, _block_diag(p, bd_mask))


def _gdn_prep_kernel(qkv_ref, prev_ref, next_ref, ab_ref, cw_ref, gp_ref,
                     u_ref, w_ref, qk_ref, qe_ref, ke_ref, dl_ref, *, seg_starts, seg_ends):
    ti = pl.program_id(1)
    tm = qkv_ref.shape[1]
    hw = GDN_HEADS * GDN_DK
    x = qkv_ref[0]
    row = _iota((tm, 1), 0)
    is_start = functools.reduce(jnp.logical_or, [ti == s for s in seg_starts])
    is_end = functools.reduce(jnp.logical_or, [ti == s for s in seg_ends])
    prev_row = jnp.where(is_start, 0.0, prev_ref[0, SUBLANES - 1:SUBLANES, :])
    next_row = jnp.where(is_end, 0.0, next_ref[0, 0:1, :])
    x_prev = jnp.where(row == 0, prev_row, pltpu.roll(x, 1, 0))
    x_next = jnp.where(row == tm - 1, next_row, pltpu.roll(x, tm - 1, 0))
    y = _silu(x_prev * cw_ref[0:1, :] + x * cw_ref[1:2, :] + x_next * cw_ref[2:3, :])

    ones_blk = _group_ones(LANES, GDN_DK)

    def l2(slab):
        return slab * lax.rsqrt(_mm_f32(slab * slab, ones_blk) + EPS)

    q = jnp.concatenate([l2(y[:, s * LANES:(s + 1) * LANES]) for s in range(hw // LANES)], axis=1)
    q = q * (GDN_DK ** -0.5)
    k = jnp.concatenate([l2(y[:, hw + s * LANES:hw + (s + 1) * LANES]) for s in range(hw // LANES)], axis=1)
    v = y[:, 2 * hw:3 * hw]

    ab = ab_ref[0]
    g_all = -jnp.exp(gp_ref[0:1, :]) * _softplus(ab + gp_ref[1:2, :])
    beta_all = _sigmoid(ab)
    col_head = jnp.right_shift(_iota((LANES, 2 * hw), 1), GDN_DK.bit_length() - 1)
    src = _iota((LANES, 2 * hw), 0)
    g_exp = _mm_f32(g_all, (src == col_head).astype(F32))
    b_exp = _mm_f32(beta_all, (src == col_head + 2 * GDN_HEADS).astype(F32))

    ri = _iota((tm, tm), 0)
    ci = _iota((tm, tm), 1)
    same = jnp.right_shift(ri, 6) == jnp.right_shift(ci, 6)
    gam_f = _mm_f32(jnp.logical_and(same, ci <= ri).astype(F32), g_exp[:, 0:hw])
    gam_b = _mm_f32(jnp.logical_and(same, ci >= ri).astype(F32), g_exp[:, hw:2 * hw])

    c = GDN_CHUNK
    r64 = _iota((c, PACK_W), 0)
    c64 = jnp.bitwise_and(_iota((c, PACK_W), 1), c - 1)
    eye = (r64 == c64).astype(F32)
    bd_mask = jnp.right_shift(_iota((PACK_W, PACK_W), 0), 6) == jnp.right_shift(_iota((PACK_W, PACK_W), 1), 6)

    for ch in range(tm // c):
        rows = slice(ch * c, (ch + 1) * c)
        for gi in range(hw // PACK_W):
            lanes = slice(gi * PACK_W, (gi + 1) * PACK_W)
            kc, qc, vc = k[rows, lanes], q[rows, lanes], v[rows, lanes]
            kbd = _block_diag(kc, bd_mask)
            kq = _mm_nt(jnp.concatenate([kc, qc], axis=0), kbd)
            kk, qkm = kq[0:c], kq[c:2 * c]
            for d in range(2):
                gam = (gam_f if d == 0 else gam_b)[rows, lanes]
                beta = b_exp[rows, d * hw + gi * PACK_W:d * hw + (gi + 1) * PACK_W]
                gam_row = jnp.sum(gam * eye, axis=0, keepdims=True)
                decay = jnp.exp(jnp.minimum(gam - gam_row, 0.0))
                strict = (r64 > c64) if d == 0 else (r64 < c64)
                incl = (r64 >= c64) if d == 0 else (r64 <= c64)
                a_mat = jnp.where(strict, beta * kk * decay, 0.0)
                tinv = _unit_tri_inverse(a_mat, eye, bd_mask)
                e_gam = jnp.exp(gam)
                g_last = gam[c - 1:c, :] if d == 0 else gam[0:1, :]
                out = slice(d * hw + gi * PACK_W, d * hw + (gi + 1) * PACK_W)
                u_ref[0, rows, out] = _mm_f32(tinv, _block_diag(vc * beta, bd_mask))
                w_ref[0, rows, out] = _mm_f32(tinv, _block_diag(kc * beta * e_gam, bd_mask)).astype(w_ref.dtype)
                qk_ref[0, rows, out] = jnp.where(incl, qkm * decay, 0.0).astype(qk_ref.dtype)
                qe_ref[0, rows, out] = (qc * e_gam).astype(qe_ref.dtype)
                ke_ref[0, rows, out] = (kc * jnp.exp(g_last - gam)).astype(ke_ref.dtype)
                dl_ref[0, ch, :, out] = jnp.exp(g_last)


def _gdn_prep(qkv, ab, cw, gp, seg_starts, seg_ends):
    b, t, w = qkv.shape
    tm = TOKEN_TILE
    nt = t // tm
    hb = tm // SUBLANES
    nh = t // SUBLANES
    wide = 2 * GDN_HEADS * GDN_DK
    cpt = tm // GDN_CHUNK
    tok = lambda width: pl.BlockSpec((1, tm, width), lambda i, j: (i, j, 0))
    out_shape = [jax.ShapeDtypeStruct((b, t, wide), F32)] + \
                [jax.ShapeDtypeStruct((b, t, wide), BF16)] * 4 + \
                [jax.ShapeDtypeStruct((b, t // GDN_CHUNK, 1, wide), F32)]
    return pl.pallas_call(
        functools.partial(_gdn_prep_kernel, seg_starts=seg_starts, seg_ends=seg_ends),
        grid=(b, nt),
        in_specs=[tok(w),
                  pl.BlockSpec((1, SUBLANES, w), lambda i, j: (i, jnp.maximum(j * hb - 1, 0), 0)),
                  pl.BlockSpec((1, SUBLANES, w), lambda i, j: (i, jnp.minimum((j + 1) * hb, nh - 1), 0)),
                  tok(LANES),
                  pl.BlockSpec(cw.shape, lambda i, j: (0, 0)),
                  pl.BlockSpec(gp.shape, lambda i, j: (0, 0))],
        out_specs=[tok(wide)] * 5 + [pl.BlockSpec((1, cpt, 1, wide), lambda i, j: (i, j, 0, 0))],
        out_shape=out_shape,
        compiler_params=_cparams(("parallel", "parallel")),
        name="gdn_prep",
    )(qkv, qkv, qkv, ab, cw, gp)


def _gdn_scan_kernel(uf, wf, qkf, qef, kef, dlf, ub, wb, qkb, qeb, keb, dlb, o_ref, s_ref, *, n_ctx_chunks):
    n_chunks = uf.shape[1] // GDN_CHUNK
    c = GDN_CHUNK
    bd_mask = jnp.right_shift(_iota((PACK_W, PACK_W), 0), 6) == jnp.right_shift(_iota((PACK_W, PACK_W), 1), 6)
    o_ref[...] = jnp.zeros_like(o_ref)
    s_ref[...] = jnp.zeros_like(s_ref)
    chains = ((uf, wf, qkf, qef, kef, dlf), (ub, wb, qkb, qeb, keb, dlb))

    def step(s, carry):
        fwd_chunk = s
        bwd_chunk = jnp.where(s < n_ctx_chunks, n_ctx_chunks - 1 - s, n_chunks - 1 - (s - n_ctx_chunks))
        for d, (u, w, qk, qe, ke, dl) in enumerate(chains):
            ci = fwd_chunk if d == 0 else bwd_chunk
            rows = pl.ds(pl.multiple_of(ci * c, c), c)
            state = s_ref[d]
            wq = _mm(jnp.concatenate([w[0, rows, :], qe[0, rows, :]], axis=0), state)
            v_new = u[0, rows, :] - wq[0:c]
            o = wq[c:2 * c] + _mm(qk[0, rows, :], _block_diag(v_new, bd_mask))
            o_ref[0, rows, :] = o_ref[0, rows, :] + o
            kv = _mm_tn(ke[0, rows, :], v_new)
            s_ref[d] = state * dl[0, ci] + jnp.where(bd_mask, kv, 0.0)
        return carry

    lax.fori_loop(0, n_chunks, step, 0)


def _gdn_scan(u, w, qk, qe, ke, dl, n_ctx):
    b, t, wide = u.shape
    ng = GDN_HEADS // PACK
    nch = t // GDN_CHUNK

    def slab(d):
        return pl.BlockSpec((1, t, PACK_W), lambda i, g: (i, 0, d * ng + g))

    def dslab(d):
        return pl.BlockSpec((1, nch, 1, PACK_W), lambda i, g: (i, 0, 0, d * ng + g))

    in_specs = [slab(0)] * 5 + [dslab(0)] + [slab(1)] * 5 + [dslab(1)]
    return pl.pallas_call(
        functools.partial(_gdn_scan_kernel, n_ctx_chunks=n_ctx // GDN_CHUNK),
        grid=(b, ng),
        in_specs=in_specs,
        out_specs=pl.BlockSpec((1, t, PACK_W), lambda i, g: (i, 0, g)),
        out_shape=jax.ShapeDtypeStruct((b, t, GDN_HEADS * GDN_DV), F32),
        scratch_shapes=[pltpu.VMEM((2, PACK_W, PACK_W), F32)],
        compiler_params=_cparams(("parallel", "parallel")),
        name="gdn_scan",
    )(u, w, qk, qe, ke, dl, u, w, qk, qe, ke, dl)


def _merge_kernel(x_ref, mod_ref, ya_ref, yb_ref, oc_ref, z_ref, gate_ref, on_ref,
                  wpa_ref, wpb_ref, wpc_ref, wout_ref, o_ref):
    x = x_ref[0]
    d = x.shape[-1]
    m = mod_ref[0]
    oc = oc_ref[0]
    ones_blk = _group_ones(LANES, GDN_DV)
    ms = jnp.concatenate(
        [_mm_f32(oc[:, s * LANES:(s + 1) * LANES] ** 2, ones_blk) for s in range(oc.shape[-1] // LANES)],
        axis=1) * (1.0 / GDN_DV)
    yc = oc * lax.rsqrt(ms + EPS) * on_ref[...] * _silu(z_ref[0])
    gate = gate_ref[0]
    mix = (_sigmoid(gate[:, 0:d]) * _mm(ya_ref[0], wpa_ref[...])
           + _sigmoid(gate[:, d:2 * d]) * _mm(yb_ref[0], wpb_ref[...])
           + _sigmoid(gate[:, 2 * d:3 * d]) * _mm(yc, wpc_ref[...]))
    o_ref[0] = x + m[:, 2 * d:3 * d] * _mm(mix, wout_ref[...])


def _merge(xs, mod_l, ya, yb, oc, z, gate, onorm, wpa, wpb, wpc, wout, nctt, ctx_row, t_off):
    b, t, d = xs.shape
    tm = TOKEN_TILE
    nt = t // tm - t_off
    tok = lambda width: pl.BlockSpec((1, tm, width), lambda i, j: (i, j + t_off, 0))
    full = lambda a: pl.BlockSpec(a.shape, lambda i, j: (0,) * a.ndim)
    mod_spec = pl.BlockSpec((1, 1, mod_l.shape[-1]),
                            lambda i, j: (jnp.where(j + t_off < nctt, ctx_row, i), 0, 0))
    return pl.pallas_call(
        _merge_kernel,
        grid=(b, nt),
        in_specs=[tok(d), mod_spec, tok(512), tok(512), tok(512), tok(512), tok(3 * d),
                  full(onorm), full(wpa), full(wpb), full(wpc), full(wout)],
        out_specs=pl.BlockSpec((1, tm, d), lambda i, j: (i, j, 0)),
        out_shape=jax.ShapeDtypeStruct((b, nt * tm, d), F32),
        compiler_params=_cparams(("parallel", "parallel")),
        name="merge",
    )(xs, mod_l, ya, yb, oc, z, gate, onorm, wpa, wpb, wpc, wout)


def _ffn_kernel(x_ref, prev_ref, next_ref, mod_ref, g2_ref, wup_ref, cw_ref, wdn_ref, nf_ref, o_ref,
                *, seg_starts, seg_ends, final):
    ti = pl.program_id(1)
    x = x_ref[0]
    tm, d = x.shape
    m = mod_ref[0]
    xall = jnp.concatenate([prev_ref[0], x, next_ref[0]], axis=0)
    xa = (_rms(xall, g2_ref[...]) * (1.0 + m[:, 4 * d:5 * d]) + m[:, 3 * d:4 * d]).astype(BF16)
    n_all = tm + 2 * SUBLANES
    row = _iota((tm, 1), 0)
    is_start = functools.reduce(jnp.logical_or, [ti == s for s in seg_starts])
    is_end = functools.reduce(jnp.logical_or, [ti == s for s in seg_ends])
    drop_prev = jnp.logical_and(row == 0, is_start)
    drop_next = jnp.logical_and(row == tm - 1, is_end)

    def conv(h, cols):
        hp = pltpu.roll(h, 1, 0)[SUBLANES:SUBLANES + tm]
        hn = pltpu.roll(h, n_all - 1, 0)[SUBLANES:SUBLANES + tm]
        hp = jnp.where(drop_prev, 0.0, hp)
        hn = jnp.where(drop_next, 0.0, hn)
        return (hp * cw_ref[0:1, cols] + h[SUBLANES:SUBLANES + tm] * cw_ref[1:2, cols]
                + hn * cw_ref[2:3, cols])

    cwid = FFN_DIM // FFN_SPLIT
    acc = jnp.zeros((tm, d), F32)
    for j in range(FFN_SPLIT):
        ca = slice(j * cwid, (j + 1) * cwid)
        cb = slice(FFN_DIM + j * cwid, FFN_DIM + (j + 1) * cwid)
        ha = conv(jnp.dot(xa, wup_ref[:, ca], preferred_element_type=F32), ca)
        hb = conv(jnp.dot(xa, wup_ref[:, cb], preferred_element_type=F32), cb)
        acc = acc + _mm(_silu(ha) * hb, wdn_ref[ca, :])
    y = x + m[:, 5 * d:6 * d] * acc
    if final:
        y = _rms(y, nf_ref[...])
    o_ref[0] = y


def _ffn(x1, mod_l, g2, wup, cw, wdn, nf, nctt_mod, ctx_row, seg_starts, seg_ends, final):
    b, t, d = x1.shape
    tm = TOKEN_TILE
    nt = t // tm
    hb = tm // SUBLANES
    nh = t // SUBLANES
    full = lambda a: pl.BlockSpec(a.shape, lambda i, j: (0,) * a.ndim)
    mod_spec = pl.BlockSpec((1, 1, mod_l.shape[-1]),
                            lambda i, j: (jnp.where(j < nctt_mod, ctx_row, i), 0, 0))
    return pl.pallas_call(
        functools.partial(_ffn_kernel, seg_starts=seg_starts, seg_ends=seg_ends, final=final),
        grid=(b, nt),
        in_specs=[pl.BlockSpec((1, tm, d), lambda i, j: (i, j, 0)),
                  pl.BlockSpec((1, SUBLANES, d), lambda i, j: (i, jnp.maximum(j * hb - 1, 0), 0)),
                  pl.BlockSpec((1, SUBLANES, d), lambda i, j: (i, jnp.minimum((j + 1) * hb, nh - 1), 0)),
                  mod_spec, full(g2), full(wup), full(cw), full(wdn), full(nf)],
        out_specs=pl.BlockSpec((1, tm, d), lambda i, j: (i, j, 0)),
        out_shape=jax.ShapeDtypeStruct((b, t, d), F32),
        compiler_params=_cparams(("parallel", "parallel")),
        name="ffn",
    )(x1, x1, x1, mod_l, g2, wup, cw, wdn, nf)


def _win_columns():
    off = np.cumsum([0, MLA_Q_RANK, MLA_KV_RANK, MLA_ROPE, SWA_HEADS * SWA_HEAD_DIM,
                     SWA_KV_HEADS * SWA_HEAD_DIM, SWA_KV_HEADS * SWA_HEAD_DIM,
                     GDN_HEADS * (2 * GDN_DK + GDN_DV), GDN_HEADS * GDN_DV, 2 * GDN_HEADS, 2 * GDN_HEADS,
                     3 * 1024])
    cq, ckv, kr, sq, sk, sv, qkv, z, a, b, gate = [np.arange(off[i], off[i + 1]) for i in range(11)]
    pad = lambda n: -np.ones(n, np.int64)
    group = SWA_HEADS // SWA_KV_HEADS
    sq_perm = np.concatenate([np.concatenate([sq[(kv * group + g) * 64:(kv * group + g + 1) * 64]
                                              for kv in range(SWA_KV_HEADS)]) for g in range(group)])
    cols = np.concatenate([cq, ckv, pad(MLA_NOPE), kr, pad(LANES - MLA_NOPE - MLA_ROPE), sq_perm, sk, sv,
                           qkv, z, a, b, pad(LANES - 4 * GDN_HEADS), gate])
    assert cols.shape[0] == _P_WIDTH
    return cols


def _take_cols(w, cols):
    valid = jnp.asarray(cols >= 0)
    return jnp.where(valid, jnp.take(w, jnp.asarray(np.maximum(cols, 0)), axis=-1), 0.0)


def _swa_head_perm():
    group = SWA_HEADS // SWA_KV_HEADS
    return np.concatenate([np.arange((kv * group + g) * 64, (kv * group + g + 1) * 64)
                           for g in range(group) for kv in range(SWA_KV_HEADS)])


def _rope_tables(n_ctx, n_lat):
    t = np.arange(n_lat)
    pos = np.stack([t // GRID_W, t % GRID_W]).astype(np.float32)

    def build(head_w, rope_lo, rope_w):
        da = rope_w // 2
        half = da // 2
        inv = ROPE_THETA ** (-np.arange(half, dtype=np.float32) / half)
        c = np.ones((n_lat, LANES), np.float32)
        sa = np.zeros((n_lat, LANES), np.float32)
        sb = np.zeros((n_lat, LANES), np.float32)
        for lane in range(LANES):
            dd = lane % head_w - rope_lo
            if dd < 0 or dd >= rope_w:
                continue
            axis, r = dd // da, dd % da
            ang = pos[axis] * inv[r % half]
            c[:, lane] = np.cos(ang)
            if r < half:
                sa[:, lane] = -np.sin(ang)
            else:
                sb[:, lane] = np.sin(ang)
        ctx = [np.ones((n_ctx, LANES), np.float32), np.zeros((n_ctx, LANES), np.float32),
               np.zeros((n_ctx, LANES), np.float32)]
        return [jnp.asarray(np.concatenate([cx, lt])) for cx, lt in zip(ctx, (c, sa, sb))]

    return build(LANES, MLA_NOPE, MLA_ROPE) + build(SWA_HEAD_DIM, 0, SWA_HEAD_DIM)


def kernel(x, c, ctx, c_ctx, w_mod, b_mod, norm1, norm2, w_in, mla_q_norm, mla_kv_norm, w_uq, w_ukv,
           swa_sink, gdn_conv, gdn_a_log, gdn_dt_bias, gdn_norm, w_branch_a, w_branch_b, w_branch_c,
           w_out, ffn_up, ffn_conv, ffn_down, norm_f):
    b, s, d = x.shape
    n_ctx = ctx.shape[1]
    depth = w_mod.shape[0]
    tm = TOKEN_TILE
    assert n_ctx % tm == 0 and s % tm == 0 and s >= 3 * SWA_TILE
    nctt = n_ctx // tm
    ntt = (n_ctx + s) // tm

    rows = -(-(b + 1) // SUBLANES) * SUBLANES
    cc = jnp.zeros((rows, d), F32).at[0:b].set(c).at[b].set(c_ctx)
    mod = _modulation(cc, w_mod, b_mod).reshape(depth, rows, 1, 6 * d)

    win = _take_cols(w_in, _win_columns()).astype(BF16)
    hq = np.arange(MLA_HEADS)[:, None]
    uq_cols = np.concatenate([hq * (MLA_NOPE + MLA_ROPE) + np.arange(MLA_NOPE + MLA_ROPE)[None],
                              -np.ones((MLA_HEADS, LANES - MLA_NOPE - MLA_ROPE), np.int64)], axis=1).reshape(-1)
    wuq = _take_cols(w_uq, uq_cols).astype(BF16)
    uk_cols = np.concatenate([hq * (MLA_NOPE + MLA_V) + np.arange(MLA_NOPE)[None],
                              -np.ones((MLA_HEADS, LANES - MLA_NOPE), np.int64)], axis=1).reshape(-1)
    wukvk = _take_cols(w_ukv, uk_cols).astype(BF16)
    uv_cols = (hq * (MLA_NOPE + MLA_V) + MLA_NOPE + np.arange(MLA_V)[None]).reshape(-1)
    wukvv = _take_cols(w_ukv, uv_cols).astype(BF16)
    wpa = w_branch_a.astype(BF16)
    wpb = jnp.take(w_branch_b, jnp.asarray(_swa_head_perm()), axis=1).astype(BF16)
    wpc = w_branch_c.astype(BF16)
    wout = w_out.astype(BF16)
    wup = ffn_up.astype(BF16)
    wdn = ffn_down.astype(BF16)
    sink = jnp.broadcast_to(swa_sink[:, :, None], (depth, SWA_HEADS, LANES))
    onorm = jnp.tile(gdn_norm, (1, GDN_HEADS)).reshape(depth, 1, GDN_HEADS * GDN_DV)
    pad_rows = lambda a: jnp.pad(a, ((0, 0), (0, SUBLANES - a.shape[1]), (0, 0)))
    gconv = pad_rows(gdn_conv)
    fconv = pad_rows(ffn_conv)
    gpar = jnp.zeros((depth, SUBLANES, LANES), F32)
    gpar = gpar.at[:, 0, 0:2 * GDN_HEADS].set(gdn_a_log.reshape(depth, -1))
    gpar = gpar.at[:, 1, 0:2 * GDN_HEADS].set(gdn_dt_bias.reshape(depth, -1))
    rope_tabs = _rope_tables(n_ctx, s)

    xs = jnp.concatenate([ctx, x], axis=1)
    row2 = lambda a: a.reshape(1, -1)
    for l in range(depth):
        last = l == depth - 1
        t_off = nctt if last else 0
        qm, km, vm, sq, sk, sv, qkv, z, ab, gate = _proj(
            xs, mod[l], row2(norm1[l]), win[l], row2(mla_q_norm[l]), row2(mla_kv_norm[l]),
            wuq[l], wukvk[l], wukvv[l], rope_tabs, nctt, b)
        ya = _mla(qm, km, vm, n_ctx, t_off)
        yb = _swa(sink[l], sq, sk, sv, n_ctx, t_off * (tm // SWA_TILE))
        u, w, qk, qe, ke, dl = _gdn_prep(qkv, ab, gconv[l], gpar[l], (0, nctt), (nctt - 1, ntt - 1))
        oc = _gdn_scan(u, w, qk, qe, ke, dl, n_ctx)
        x1 = _merge(xs, mod[l], ya, yb, oc, z, gate, onorm[l], wpa[l], wpb[l], wpc[l], wout[l],
                    nctt, b, t_off)
        if last:
            xs = _ffn(x1, mod[l], row2(norm2[l]), wup[l], fconv[l], wdn[l], row2(norm_f),
                      0, b, (0,), (s // tm - 1,), True)
        else:
            xs = _ffn(x1, mod[l], row2(norm2[l]), wup[l], fconv[l], wdn[l], row2(norm_f),
                      nctt, b, (0, nctt), (nctt - 1, ntt - 1), False)
    return xs
```

```python
import functools

import numpy as np
import jax
import jax.numpy as jnp
from jax import lax
from jax.experimental import pallas as pl
from jax.experimental.pallas import tpu as pltpu

F32 = jnp.float32
BF16 = jnp.bfloat16
HIGHEST = lax.Precision.HIGHEST

GRID_W = 64
ROPE_THETA = 10000.0
NEG_INF = -1e30
EPS = 1e-6

MLA_HEADS = 8
MLA_Q_RANK = 384
MLA_KV_RANK = 256
MLA_NOPE = 64
MLA_ROPE = 32
MLA_V = 64

SWA_HEADS = 8
SWA_KV_HEADS = 2
SWA_HEAD_DIM = 64
WINDOW = 128

GDN_HEADS = 8
GDN_DK = 64
GDN_DV = 64
GDN_CHUNK = 64

FFN_DIM = 2816

LANES = 128
SUBLANES = 8
VMEM_LIMIT_BYTES = 56 * 1024 * 1024

TOKEN_TILE = 256
SWA_TILE = 128
PACK = 4
PACK_W = PACK * GDN_DK
FFN_SPLIT = 2


def _sigmoid(x):
    return 1.0 / (1.0 + jnp.exp(-x))


def _silu(x):
    return x * _sigmoid(x)


def _softplus(x):
    return jnp.maximum(x, 0.0) + jnp.log(1.0 + jnp.exp(-jnp.abs(x)))


def _rms(x, g):
    return x * lax.rsqrt(jnp.mean(x * x, axis=-1, keepdims=True) + EPS) * g


def _mm(a, b):
    return jnp.dot(a.astype(BF16), b.astype(BF16), preferred_element_type=F32)


def _mm_nt(a, b):
    return lax.dot_general(a.astype(BF16), b.astype(BF16), (((1,), (1,)), ((), ())),
                           preferred_element_type=F32)


def _mm_tn(a, b):
    return lax.dot_general(a.astype(BF16), b.astype(BF16), (((0,), (0,)), ((), ())),
                           preferred_element_type=F32)


def _mm_f32(a, b):
    return jnp.dot(a, b, precision=HIGHEST, preferred_element_type=F32)


def _iota(shape, dim):
    return lax.broadcasted_iota(jnp.int32, shape, dim)


def _group_ones(n, group):
    sh = group.bit_length() - 1
    return (jnp.right_shift(_iota((n, n), 0), sh) == jnp.right_shift(_iota((n, n), 1), sh)).astype(F32)


def _rope(x, c, sa, sb, half):
    n = x.shape[-1]
    return x * c + pltpu.roll(x, n - half, 1) * sa + pltpu.roll(x, half, 1) * sb


def _cparams(sem):
    return pltpu.CompilerParams(dimension_semantics=sem, vmem_limit_bytes=VMEM_LIMIT_BYTES)


def _mod_kernel(c_ref, w_ref, b_ref, o_ref):
    s = _silu(c_ref[...])
    o_ref[0] = _mm(s, w_ref[0]) + b_ref[0]


def _modulation(cc, w_mod, b_mod):
    depth, d, n = w_mod.shape
    r = cc.shape[0]
    tn = 1024
    return pl.pallas_call(
        _mod_kernel,
        grid=(depth, n // tn),
        in_specs=[pl.BlockSpec((r, d), lambda l, j: (0, 0)),
                  pl.BlockSpec((1, d, tn), lambda l, j: (l, 0, j)),
                  pl.BlockSpec((1, 1, tn), lambda l, j: (l, 0, j))],
        out_specs=pl.BlockSpec((1, r, tn), lambda l, j: (l, 0, j)),
        out_shape=jax.ShapeDtypeStruct((depth, r, n), F32),
        compiler_params=_cparams(("parallel", "parallel")),
        name="modulation",
    )(cc, w_mod, b_mod.reshape(depth, 1, n))


_P_CQ = (0, 384)
_P_CKV = (384, 640)
_P_KR = (640, 768)
_P_SQ = (768, 1280)
_P_SK = (1280, 1408)
_P_SV = (1408, 1536)
_P_QKV = (1536, 3072)
_P_Z = (3072, 3584)
_P_AB = (3584, 3712)
_P_GATE = (3712, 6784)
_P_WIDTH = 6784


def _proj_kernel(x_ref, mod_ref, g1_ref, win_ref, qn_ref, kvn_ref, wuq_ref, wukvk_ref, wukvv_ref,
                 mc_ref, ma_ref, mb_ref, sc_ref, sa_ref, sb_ref,
                 qm_ref, km_ref, vm_ref, sq_ref, sk_ref, sv_ref, qkv_ref, z_ref, ab_ref, gate_ref):
    x = x_ref[0]
    d = x.shape[-1]
    m = mod_ref[0]
    xb = (_rms(x, g1_ref[...]) * (1.0 + m[:, d:2 * d]) + m[:, 0:d]).astype(BF16)

    def proj(piece):
        return jnp.dot(xb, win_ref[:, piece[0]:piece[1]], preferred_element_type=F32)

    mc, ma, mb = mc_ref[...], ma_ref[...], mb_ref[...]
    sc, sa, sb = sc_ref[...], sa_ref[...], sb_ref[...]

    cq = _rms(proj(_P_CQ), qn_ref[...])
    q = _mm(cq, wuq_ref[...]) * ((MLA_NOPE + MLA_ROPE) ** -0.5)
    ckv = _rms(proj(_P_CKV), kvn_ref[...])
    kn = _mm(ckv, wukvk_ref[...])
    vm_ref[0] = _mm(ckv, wukvv_ref[...]).astype(vm_ref.dtype)
    kr = _rope(proj(_P_KR), mc, ma, mb, MLA_ROPE // 4)
    for h in range(MLA_HEADS):
        sl = slice(h * LANES, (h + 1) * LANES)
        qm_ref[0, :, sl] = _rope(q[:, sl], mc, ma, mb, MLA_ROPE // 4).astype(qm_ref.dtype)
        km_ref[0, :, sl] = (kn[:, sl] + kr).astype(km_ref.dtype)

    sq = proj(_P_SQ) * (SWA_HEAD_DIM ** -0.5)
    for g in range(SWA_HEADS * SWA_HEAD_DIM // LANES):
        sl = slice(g * LANES, (g + 1) * LANES)
        sq_ref[0, :, sl] = _rope(sq[:, sl], sc, sa, sb, SWA_HEAD_DIM // 4).astype(sq_ref.dtype)
    sk_ref[0] = _rope(proj(_P_SK), sc, sa, sb, SWA_HEAD_DIM // 4).astype(sk_ref.dtype)
    sv_ref[0] = proj(_P_SV).astype(sv_ref.dtype)

    qkv_ref[0] = proj(_P_QKV)
    z_ref[0] = proj(_P_Z)
    ab_ref[0] = proj(_P_AB)
    gate_ref[0] = proj(_P_GATE)


def _proj(xs, mod_l, g1, win, qn, kvn, wuq, wukvk, wukvv, rope_tabs, nctt, ctx_row):
    b, t, d = xs.shape
    tm = TOKEN_TILE
    nt = t // tm

    def tok(w):
        return pl.BlockSpec((1, tm, w), lambda i, j: (i, j, 0))

    def full(a):
        return pl.BlockSpec(a.shape, lambda i, j: (0,) * a.ndim)

    tab = pl.BlockSpec((tm, LANES), lambda i, j: (j, 0))
    mod_spec = pl.BlockSpec((1, 1, mod_l.shape[-1]),
                            lambda i, j: (jnp.where(j < nctt, ctx_row, i), 0, 0))
    widths = (1024, 1024, 512, 512, 128, 128, 1536, 512, 128, 3072)
    dtypes = (BF16, BF16, BF16, BF16, BF16, BF16, F32, F32, F32, F32)
    return pl.pallas_call(
        _proj_kernel,
        grid=(b, nt),
        in_specs=[tok(d), mod_spec, full(g1), full(win), full(qn), full(kvn), full(wuq),
                  full(wukvk), full(wukvv)] + [tab] * 6,
        out_specs=[tok(w) for w in widths],
        out_shape=[jax.ShapeDtypeStruct((b, t, w), dt) for w, dt in zip(widths, dtypes)],
        compiler_params=_cparams(("parallel", "parallel")),
        name="proj",
    )(xs, mod_l, g1, win, qn, kvn, wuq, wukvk, wukvv, *rope_tabs)


def _mla_kernel(q_ref, k_ref, v_ref, o_ref, *, n_ctx, q_off):
    i = pl.program_id(2) + q_off
    tq = q_ref.shape[1]
    nctt = n_ctx // tq
    low = _iota((tq, LANES), 1) < MLA_V

    def run(nk):
        v = v_ref[0, 0:nk, :]
        outs = []
        for hh in range(2):
            q = q_ref[0, :, hh * LANES:(hh + 1) * LANES]
            k = k_ref[0, 0:nk, hh * LANES:(hh + 1) * LANES]
            s = _mm_nt(q, k)
            p = jnp.exp(s - jnp.max(s, axis=-1, keepdims=True))
            l = jnp.sum(p, axis=-1, keepdims=True)
            outs.append(_mm(p, v) / l)
        o_ref[0] = jnp.where(low, outs[0], outs[1]).astype(o_ref.dtype)

    if q_off < nctt:
        @pl.when(i < nctt)
        def _():
            run(n_ctx)

    @pl.when(i >= nctt)
    def _():
        run(k_ref.shape[1])


def _mla(qm, km, vm, n_ctx, q_off):
    b, t, _ = qm.shape
    tq = TOKEN_TILE
    nt = t // tq - q_off
    return pl.pallas_call(
        functools.partial(_mla_kernel, n_ctx=n_ctx, q_off=q_off),
        grid=(b, MLA_HEADS // 2, nt),
        in_specs=[pl.BlockSpec((1, tq, 2 * LANES), lambda i, j, n: (i, n + q_off, j)),
                  pl.BlockSpec((1, t, 2 * LANES), lambda i, j, n: (i, 0, j)),
                  pl.BlockSpec((1, t, LANES), lambda i, j, n: (i, 0, j))],
        out_specs=pl.BlockSpec((1, tq, LANES), lambda i, j, n: (i, n + q_off, j)),
        out_shape=jax.ShapeDtypeStruct((b, t, MLA_HEADS * MLA_V), BF16),
        compiler_params=_cparams(("parallel", "parallel", "parallel")),
        name="mla",
    )(qm, km, vm)


def _swa_kernel(sink_ref, q_ref, k_ref, v_ref, o_ref, *, n_ctx, q_off):
    i = pl.program_id(1) + q_off
    blk = q_ref.shape[1]
    t = k_ref.shape[1]
    n_lat = t - n_ctx
    nctt = n_ctx // blk
    low = _iota((blk, LANES), 1) < SWA_HEAD_DIM
    n_slab = SWA_HEADS * SWA_HEAD_DIM // LANES
    kc = k_ref[0, 0:n_ctx, :]
    vc = v_ref[0, 0:n_ctx, :]

    def attend(loc):
        for g in range(n_slab):
            qs = q_ref[0, :, g * LANES:(g + 1) * LANES]
            outs = []
            for hh in range(2):
                qh = jnp.where(low if hh == 0 else jnp.logical_not(low), qs, jnp.zeros_like(qs))
                head = hh * n_slab + g
                sink = sink_ref[head:head + 1, 0:1]
                s_c = _mm_nt(qh, kc)
                mx = jnp.maximum(jnp.max(s_c, axis=-1, keepdims=True), sink)
                if loc is not None:
                    kl, vl, valid = loc
                    s_l = jnp.where(valid, _mm_nt(qh, kl), NEG_INF)
                    mx = jnp.maximum(mx, jnp.max(s_l, axis=-1, keepdims=True))
                p_c = jnp.exp(s_c - mx)
                den = jnp.sum(p_c, axis=-1, keepdims=True) + jnp.exp(sink - mx)
                acc = _mm(p_c, vc)
                if loc is not None:
                    p_l = jnp.exp(s_l - mx)
                    den = den + jnp.sum(p_l, axis=-1, keepdims=True)
                    acc = acc + _mm(p_l, vl)
                outs.append(acc / den)
            o_ref[0, :, g * LANES:(g + 1) * LANES] = jnp.where(low, outs[0], outs[1]).astype(o_ref.dtype)

    if q_off < nctt:
        @pl.when(i < nctt)
        def _():
            attend(None)

    @pl.when(i >= nctt)
    def _():
        n = i - nctt
        span = 3 * blk
        start = jnp.clip((n - 1) * blk, 0, n_lat - span)
        row0 = pl.multiple_of(n_ctx + start, blk)
        kl = k_ref[0, pl.ds(row0, span), :]
        vl = v_ref[0, pl.ds(row0, span), :]
        kpos = start + _iota((blk, span), 1)
        qpos = n * blk + _iota((blk, span), 0)
        valid = jnp.abs(kpos - qpos) <= WINDOW
        attend((kl, vl, valid))


def _swa(sink, sq, sk, sv, n_ctx, q_off):
    b, t, w = sq.shape
    blk = SWA_TILE
    nt = t // blk - q_off
    return pl.pallas_call(
        functools.partial(_swa_kernel, n_ctx=n_ctx, q_off=q_off),
        grid=(b, nt),
        in_specs=[pl.BlockSpec(sink.shape, lambda i, n: (0, 0)),
                  pl.BlockSpec((1, blk, w), lambda i, n: (i, n + q_off, 0)),
                  pl.BlockSpec((1, t, LANES), lambda i, n: (i, 0, 0)),
                  pl.BlockSpec((1, t, LANES), lambda i, n: (i, 0, 0))],
        out_specs=pl.BlockSpec((1, blk, w), lambda i, n: (i, n + q_off, 0)),
        out_shape=jax.ShapeDtypeStruct((b, t, w), BF16),
        compiler_params=_cparams(("parallel", "parallel")),
        name="swa",
    )(sink, sq, sk, sv)


def _block_diag(x, bd_mask):
    return jnp.where(bd_mask, jnp.concatenate([x] * PACK, axis=0), 0.0)


def _gdn_prep_kernel(qkv_ref, prev_ref, next_ref, ab_ref, cw_ref, gp_ref,
                     a_ref, rv_ref, rk_ref, qk_ref, qe_ref, ke_ref, dl_ref, *, seg_starts, seg_ends):
    ti = pl.program_id(1)
    tm = qkv_ref.shape[1]
    hw = GDN_HEADS * GDN_DK
    x = qkv_ref[0]
    row = _iota((tm, 1), 0)
    is_start = functools.reduce(jnp.logical_or, [ti == s for s in seg_starts])
    is_end = functools.reduce(jnp.logical_or, [ti == s for s in seg_ends])
    prev_row = jnp.where(is_start, 0.0, prev_ref[0, SUBLANES - 1:SUBLANES, :])
    next_row = jnp.where(is_end, 0.0, next_ref[0, 0:1, :])
    x_prev = jnp.where(row == 0, prev_row, pltpu.roll(x, 1, 0))
    x_next = jnp.where(row == tm - 1, next_row, pltpu.roll(x, tm - 1, 0))
    y = _silu(x_prev * cw_ref[0:1, :] + x * cw_ref[1:2, :] + x_next * cw_ref[2:3, :])

    ones_blk = _group_ones(LANES, GDN_DK)

    def l2(slab):
        return slab * lax.rsqrt(_mm_f32(slab * slab, ones_blk) + EPS)

    q = jnp.concatenate([l2(y[:, s * LANES:(s + 1) * LANES]) for s in range(hw // LANES)], axis=1)
    q = q * (GDN_DK ** -0.5)
    k = jnp.concatenate([l2(y[:, hw + s * LANES:hw + (s + 1) * LANES]) for s in range(hw // LANES)], axis=1)
    v = y[:, 2 * hw:3 * hw]

    ab = ab_ref[0]
    g_all = -jnp.exp(gp_ref[0:1, :]) * _softplus(ab + gp_ref[1:2, :])
    beta_all = _sigmoid(ab)
    col_head = jnp.right_shift(_iota((LANES, 2 * hw), 1), GDN_DK.bit_length() - 1)
    src = _iota((LANES, 2 * hw), 0)
    g_exp = _mm_f32(g_all, (src == col_head).astype(F32))
    b_exp = _mm_f32(beta_all, (src == col_head + 2 * GDN_HEADS).astype(F32))

    ri = _iota((tm, tm), 0)
    ci = _iota((tm, tm), 1)
    same = jnp.right_shift(ri, 6) == jnp.right_shift(ci, 6)
    gam_f = _mm_f32(jnp.logical_and(same, ci <= ri).astype(F32), g_exp[:, 0:hw])
    gam_b = _mm_f32(jnp.logical_and(same, ci >= ri).astype(F32), g_exp[:, hw:2 * hw])

    c = GDN_CHUNK
    r64 = _iota((c, PACK_W), 0)
    c64 = jnp.bitwise_and(_iota((c, PACK_W), 1), c - 1)
    eye = (r64 == c64).astype(F32)
    bd_mask = jnp.right_shift(_iota((PACK_W, PACK_W), 0), 6) == jnp.right_shift(_iota((PACK_W, PACK_W), 1), 6)

    for ch in range(tm // c):
        rows = slice(ch * c, (ch + 1) * c)
        for gi in range(hw // PACK_W):
            lanes = slice(gi * PACK_W, (gi + 1) * PACK_W)
            kc, qc, vc = k[rows, lanes], q[rows, lanes], v[rows, lanes]
            kbd = _block_diag(kc, bd_mask)
            kq = _mm_nt(jnp.concatenate([kc, qc], axis=0), kbd)
            kk, qkm = kq[0:c], kq[c:2 * c]
            for d in range(2):
                gam = (gam_f if d == 0 else gam_b)[rows, lanes]
                beta = b_exp[rows, d * hw + gi * PACK_W:d * hw + (gi + 1) * PACK_W]
                gam_row = jnp.sum(gam * eye, axis=0, keepdims=True)
                decay = jnp.exp(jnp.minimum(gam - gam_row, 0.0))
                strict = (r64 > c64) if d == 0 else (r64 < c64)
                incl = (r64 >= c64) if d == 0 else (r64 <= c64)
                a_mat = jnp.where(strict, beta * kk * decay, 0.0)
                for h in range(PACK):
                    a_ref[d, 0, ch, gi * PACK + h] = a_mat[:, h * c:(h + 1) * c]
                e_gam = jnp.exp(gam)
                g_last = gam[c - 1:c, :] if d == 0 else gam[0:1, :]
                out = slice(d * hw + gi * PACK_W, d * hw + (gi + 1) * PACK_W)
                rv_ref[0, rows, out] = vc * beta
                rk_ref[0, rows, out] = kc * beta * e_gam
                qk_ref[0, rows, out] = jnp.where(incl, qkm * decay, 0.0).astype(qk_ref.dtype)
                qe_ref[0, rows, out] = (qc * e_gam).astype(qe_ref.dtype)
                ke_ref[0, rows, out] = (kc * jnp.exp(g_last - gam)).astype(ke_ref.dtype)
                dl_ref[0, ch, :, out] = jnp.exp(g_last)


def _gdn_prep(qkv, ab, cw, gp, seg_starts, seg_ends):
    b, t, w = qkv.shape
    tm = TOKEN_TILE
    nt = t // tm
    hb = tm // SUBLANES
    nh = t // SUBLANES
    wide = 2 * GDN_HEADS * GDN_DK
    cpt = tm // GDN_CHUNK
    c = GDN_CHUNK
    tok = lambda width: pl.BlockSpec((1, tm, width), lambda i, j: (i, j, 0))
    out_shape = [jax.ShapeDtypeStruct((2, b, t // c, GDN_HEADS, c, c), F32)] + \
                [jax.ShapeDtypeStruct((b, t, wide), F32)] * 2 + \
                [jax.ShapeDtypeStruct((b, t, wide), BF16)] * 3 + \
                [jax.ShapeDtypeStruct((b, t // c, 1, wide), F32)]
    return pl.pallas_call(
        functools.partial(_gdn_prep_kernel, seg_starts=seg_starts, seg_ends=seg_ends),
        grid=(b, nt),
        in_specs=[tok(w),
                  pl.BlockSpec((1, SUBLANES, w), lambda i, j: (i, jnp.maximum(j * hb - 1, 0), 0)),
                  pl.BlockSpec((1, SUBLANES, w), lambda i, j: (i, jnp.minimum((j + 1) * hb, nh - 1), 0)),
                  tok(LANES),
                  pl.BlockSpec(cw.shape, lambda i, j: (0, 0)),
                  pl.BlockSpec(gp.shape, lambda i, j: (0, 0))],
        out_specs=[pl.BlockSpec((2, 1, cpt, GDN_HEADS, c, c), lambda i, j: (0, i, j, 0, 0, 0))] +
                  [tok(wide)] * 5 + [pl.BlockSpec((1, cpt, 1, wide), lambda i, j: (i, j, 0, 0))],
        out_shape=out_shape,
        compiler_params=_cparams(("parallel", "parallel")),
        name="gdn_prep",
    )(qkv, qkv, qkv, ab, cw, gp)


def _gdn_solve_kernel(a_ref, o_ref, at_ref, x_ref, *, groups_per_dir):
    c = GDN_CHUNK
    kb_w = SUBLANES
    for tix in range(c * c // LANES):
        at_ref[tix * LANES:(tix + 1) * LANES, :] = a_ref[:, tix * LANES:(tix + 1) * LANES].T
    x_ref[...] = jnp.zeros_like(x_ref)
    upper = pl.program_id(0) >= groups_per_dir
    col = _iota((c, LANES), 0)

    def row(step, carry):
        i = jnp.where(upper, c - 1 - step, step)
        lo = jnp.where(upper, (i + 1) // kb_w, 0)
        hi = jnp.where(upper, c // kb_w, (i + kb_w - 1) // kb_w)

        def kblock(kb, acc):
            for kk in range(kb_w):
                k = kb * kb_w + kk
                coef = at_ref[pl.ds(i * c + k, 1), :]
                acc = acc - coef * x_ref[pl.ds(pl.multiple_of(k * c, c), c), :]
            return acc

        acc = lax.fori_loop(lo, hi, kblock, (col == i).astype(F32))
        x_ref[pl.ds(pl.multiple_of(i * c, c), c), :] = acc
        return carry

    lax.fori_loop(0, c, row, 0)
    for tix in range(c * c // LANES):
        o_ref[:, tix * LANES:(tix + 1) * LANES] = x_ref[tix * LANES:(tix + 1) * LANES, :].T


def _gdn_solve(a):
    shape = a.shape
    c = GDN_CHUNK
    n_sys = int(np.prod(shape[:-2]))
    per_dir = n_sys // 2
    assert per_dir % LANES == 0
    flat = a.reshape(n_sys, c * c)
    out = pl.pallas_call(
        functools.partial(_gdn_solve_kernel, groups_per_dir=per_dir // LANES),
        grid=(n_sys // LANES,),
        in_specs=[pl.BlockSpec((LANES, c * c), lambda g: (g, 0))],
        out_specs=pl.BlockSpec((LANES, c * c), lambda g: (g, 0)),
        out_shape=jax.ShapeDtypeStruct((n_sys, c * c), F32),
        scratch_shapes=[pltpu.VMEM((c * c, LANES), F32), pltpu.VMEM((c * c, LANES), F32)],
        compiler_params=_cparams(("parallel",)),
        name="gdn_solve",
    )(flat)
    return out.reshape(shape)


def _split_bf16(x):
    hi = x.astype(BF16)
    return hi, (x - hi.astype(F32)).astype(BF16)


def _gdn_apply_kernel(t_ref, rv_ref, rk_ref, u_ref, w_ref):
    tm = rv_ref.shape[1]
    c = GDN_CHUNK
    hw = GDN_HEADS * GDN_DK
    bd_mask = jnp.right_shift(_iota((PACK_W, PACK_W), 0), 6) == jnp.right_shift(_iota((PACK_W, PACK_W), 1), 6)
    for ch in range(tm // c):
        rows = slice(ch * c, (ch + 1) * c)
        for gi in range(hw // PACK_W):
            for d in range(2):
                out = slice(d * hw + gi * PACK_W, d * hw + (gi + 1) * PACK_W)
                tinv = jnp.concatenate([t_ref[d, 0, ch, gi * PACK + h] for h in range(PACK)], axis=1)
                t_hi, t_lo = _split_bf16(tinv)
                v_hi, v_lo = _split_bf16(rv_ref[0, rows, out])
                top = jnp.dot(jnp.concatenate([t_hi, t_lo], axis=0), _block_diag(v_hi, bd_mask),
                              preferred_element_type=F32)
                u_ref[0, rows, out] = (top[0:c] + top[c:2 * c]
                                       + jnp.dot(t_hi, _block_diag(v_lo, bd_mask), preferred_element_type=F32))
                w_ref[0, rows, out] = jnp.dot(t_hi, _block_diag(rk_ref[0, rows, out].astype(BF16), bd_mask),
                                              preferred_element_type=F32).astype(w_ref.dtype)


def _gdn_apply(tinv, rv, rk):
    b, t, wide = rv.shape
    tm = TOKEN_TILE
    c = GDN_CHUNK
    cpt = tm // c
    tok = pl.BlockSpec((1, tm, wide), lambda i, j: (i, j, 0))
    return pl.pallas_call(
        _gdn_apply_kernel,
        grid=(b, t // tm),
        in_specs=[pl.BlockSpec((2, 1, cpt, GDN_HEADS, c, c), lambda i, j: (0, i, j, 0, 0, 0)), tok, tok],
        out_specs=[tok, tok],
        out_shape=[jax.ShapeDtypeStruct((b, t, wide), F32), jax.ShapeDtypeStruct((b, t, wide), BF16)],
        compiler_params=_cparams(("parallel", "parallel")),
        name="gdn_apply",
    )(tinv, rv, rk)


def _gdn_scan_kernel(uf, wf, qkf, qef, kef, dlf, ub, wb, qkb, qeb, keb, dlb, o_ref, s_ref, *, n_ctx_chunks):
    n_chunks = uf.shape[1] // GDN_CHUNK
    c = GDN_CHUNK
    bd_mask = jnp.right_shift(_iota((PACK_W, PACK_W), 0), 6) == jnp.right_shift(_iota((PACK_W, PACK_W), 1), 6)
    o_ref[...] = jnp.zeros_like(o_ref)
    s_ref[...] = jnp.zeros_like(s_ref)
    chains = ((uf, wf, qkf, qef, kef, dlf), (ub, wb, qkb, qeb, keb, dlb))

    def step(s, carry):
        fwd_chunk = s
        bwd_chunk = jnp.where(s < n_ctx_chunks, n_ctx_chunks - 1 - s, n_chunks - 1 - (s - n_ctx_chunks))
        for d, (u, w, qk, qe, ke, dl) in enumerate(chains):
            ci = fwd_chunk if d == 0 else bwd_chunk
            rows = pl.ds(pl.multiple_of(ci * c, c), c)
            state = s_ref[d]
            wq = _mm(jnp.concatenate([w[0, rows, :], qe[0, rows, :]], axis=0), state)
            v_new = u[0, rows, :] - wq[0:c]
            o = wq[c:2 * c] + _mm(qk[0, rows, :], _block_diag(v_new, bd_mask))
            o_ref[0, rows, :] = o_ref[0, rows, :] + o
            kv = _mm_tn(ke[0, rows, :], v_new)
            s_ref[d] = state * dl[0, ci] + jnp.where(bd_mask, kv, 0.0)
        return carry

    lax.fori_loop(0, n_chunks, step, 0)


def _gdn_scan(u, w, qk, qe, ke, dl, n_ctx):
    b, t, wide = u.shape
    ng = GDN_HEADS // PACK
    nch = t // GDN_CHUNK

    def slab(d):
        return pl.BlockSpec((1, t, PACK_W), lambda i, g: (i, 0, d * ng + g))

    def dslab(d):
        return pl.BlockSpec((1, nch, 1, PACK_W), lambda i, g: (i, 0, 0, d * ng + g))

    in_specs = [slab(0)] * 5 + [dslab(0)] + [slab(1)] * 5 + [dslab(1)]
    return pl.pallas_call(
        functools.partial(_gdn_scan_kernel, n_ctx_chunks=n_ctx // GDN_CHUNK),
        grid=(b, ng),
        in_specs=in_specs,
        out_specs=pl.BlockSpec((1, t, PACK_W), lambda i, g: (i, 0, g)),
        out_shape=jax.ShapeDtypeStruct((b, t, GDN_HEADS * GDN_DV), F32),
        scratch_shapes=[pltpu.VMEM((2, PACK_W, PACK_W), F32)],
        compiler_params=_cparams(("parallel", "parallel")),
        name="gdn_scan",
    )(u, w, qk, qe, ke, dl, u, w, qk, qe, ke, dl)


def _merge_kernel(x_ref, mod_ref, ya_ref, yb_ref, oc_ref, z_ref, gate_ref, on_ref,
                  wpa_ref, wpb_ref, wpc_ref, wout_ref, o_ref):
    x = x_ref[0]
    d = x.shape[-1]
    m = mod_ref[0]
    oc = oc_ref[0]
    ones_blk = _group_ones(LANES, GDN_DV)
    ms = jnp.concatenate(
        [_mm_f32(oc[:, s * LANES:(s + 1) * LANES] ** 2, ones_blk) for s in range(oc.shape[-1] // LANES)],
        axis=1) * (1.0 / GDN_DV)
    yc = oc * lax.rsqrt(ms + EPS) * on_ref[...] * _silu(z_ref[0])
    gate = gate_ref[0]
    mix = (_sigmoid(gate[:, 0:d]) * _mm(ya_ref[0], wpa_ref[...])
           + _sigmoid(gate[:, d:2 * d]) * _mm(yb_ref[0], wpb_ref[...])
           + _sigmoid(gate[:, 2 * d:3 * d]) * _mm(yc, wpc_ref[...]))
    o_ref[0] = x + m[:, 2 * d:3 * d] * _mm(mix, wout_ref[...])


def _merge(xs, mod_l, ya, yb, oc, z, gate, onorm, wpa, wpb, wpc, wout, nctt, ctx_row, t_off):
    b, t, d = xs.shape
    tm = TOKEN_TILE
    nt = t // tm - t_off
    tok = lambda width: pl.BlockSpec((1, tm, width), lambda i, j: (i, j + t_off, 0))
    full = lambda a: pl.BlockSpec(a.shape, lambda i, j: (0,) * a.ndim)
    mod_spec = pl.BlockSpec((1, 1, mod_l.shape[-1]),
                            lambda i, j: (jnp.where(j + t_off < nctt, ctx_row, i), 0, 0))
    return pl.pallas_call(
        _merge_kernel,
        grid=(b, nt),
        in_specs=[tok(d), mod_spec, tok(512), tok(512), tok(512), tok(512), tok(3 * d),
                  full(onorm), full(wpa), full(wpb), full(wpc), full(wout)],
        out_specs=pl.BlockSpec((1, tm, d), lambda i, j: (i, j, 0)),
        out_shape=jax.ShapeDtypeStruct((b, nt * tm, d), F32),
        compiler_params=_cparams(("parallel", "parallel")),
        name="merge",
    )(xs, mod_l, ya, yb, oc, z, gate, onorm, wpa, wpb, wpc, wout)


def _ffn_kernel(x_ref, prev_ref, next_ref, mod_ref, g2_ref, wup_ref, cw_ref, wdn_ref, nf_ref, o_ref,
                *, seg_starts, seg_ends, final):
    ti = pl.program_id(1)
    x = x_ref[0]
    tm, d = x.shape
    m = mod_ref[0]
    xall = jnp.concatenate([prev_ref[0], x, next_ref[0]], axis=0)
    xa = (_rms(xall, g2_ref[...]) * (1.0 + m[:, 4 * d:5 * d]) + m[:, 3 * d:4 * d]).astype(BF16)
    n_all = tm + 2 * SUBLANES
    row = _iota((tm, 1), 0)
    is_start = functools.reduce(jnp.logical_or, [ti == s for s in seg_starts])
    is_end = functools.reduce(jnp.logical_or, [ti == s for s in seg_ends])
    drop_prev = jnp.logical_and(row == 0, is_start)
    drop_next = jnp.logical_and(row == tm - 1, is_end)

    def conv(h, cols):
        hp = pltpu.roll(h, 1, 0)[SUBLANES:SUBLANES + tm]
        hn = pltpu.roll(h, n_all - 1, 0)[SUBLANES:SUBLANES + tm]
        hp = jnp.where(drop_prev, 0.0, hp)
        hn = jnp.where(drop_next, 0.0, hn)
        return (hp * cw_ref[0:1, cols] + h[SUBLANES:SUBLANES + tm] * cw_ref[1:2, cols]
                + hn * cw_ref[2:3, cols])

    cwid = FFN_DIM // FFN_SPLIT
    acc = jnp.zeros((tm, d), F32)
    for j in range(FFN_SPLIT):
        ca = slice(j * cwid, (j + 1) * cwid)
        cb = slice(FFN_DIM + j * cwid, FFN_DIM + (j + 1) * cwid)
        ha = conv(jnp.dot(xa, wup_ref[:, ca], preferred_element_type=F32), ca)
        hb = conv(jnp.dot(xa, wup_ref[:, cb], preferred_element_type=F32), cb)
        acc = acc + _mm(_silu(ha) * hb, wdn_ref[ca, :])
    y = x + m[:, 5 * d:6 * d] * acc
    if final:
        y = _rms(y, nf_ref[...])
    o_ref[0] = y


def _ffn(x1, mod_l, g2, wup, cw, wdn, nf, nctt_mod, ctx_row, seg_starts, seg_ends, final):
    b, t, d = x1.shape
    tm = TOKEN_TILE
    nt = t // tm
    hb = tm // SUBLANES
    nh = t // SUBLANES
    full = lambda a: pl.BlockSpec(a.shape, lambda i, j: (0,) * a.ndim)
    mod_spec = pl.BlockSpec((1, 1, mod_l.shape[-1]),
                            lambda i, j: (jnp.where(j < nctt_mod, ctx_row, i), 0, 0))
    return pl.pallas_call(
        functools.partial(_ffn_kernel, seg_starts=seg_starts, seg_ends=seg_ends, final=final),
        grid=(b, nt),
        in_specs=[pl.BlockSpec((1, tm, d), lambda i, j: (i, j, 0)),
                  pl.BlockSpec((1, SUBLANES, d), lambda i, j: (i, jnp.maximum(j * hb - 1, 0), 0)),
                  pl.BlockSpec((1, SUBLANES, d), lambda i, j: (i, jnp.minimum((j + 1) * hb, nh - 1), 0)),
                  mod_spec, full(g2), full(wup), full(cw), full(wdn), full(nf)],
        out_specs=pl.BlockSpec((1, tm, d), lambda i, j: (i, j, 0)),
        out_shape=jax.ShapeDtypeStruct((b, t, d), F32),
        compiler_params=_cparams(("parallel", "parallel")),
        name="ffn",
    )(x1, x1, x1, mod_l, g2, wup, cw, wdn, nf)


def _win_columns():
    off = np.cumsum([0, MLA_Q_RANK, MLA_KV_RANK, MLA_ROPE, SWA_HEADS * SWA_HEAD_DIM,
                     SWA_KV_HEADS * SWA_HEAD_DIM, SWA_KV_HEADS * SWA_HEAD_DIM,
                     GDN_HEADS * (2 * GDN_DK + GDN_DV), GDN_HEADS * GDN_DV, 2 * GDN_HEADS, 2 * GDN_HEADS,
                     3 * 1024])
    cq, ckv, kr, sq, sk, sv, qkv, z, a, b, gate = [np.arange(off[i], off[i + 1]) for i in range(11)]
    pad = lambda n: -np.ones(n, np.int64)
    group = SWA_HEADS // SWA_KV_HEADS
    sq_perm = np.concatenate([np.concatenate([sq[(kv * group + g) * 64:(kv * group + g + 1) * 64]
                                              for kv in range(SWA_KV_HEADS)]) for g in range(group)])
    cols = np.concatenate([cq, ckv, pad(MLA_NOPE), kr, pad(LANES - MLA_NOPE - MLA_ROPE), sq_perm, sk, sv,
                           qkv, z, a, b, pad(LANES - 4 * GDN_HEADS), gate])
    assert cols.shape[0] == _P_WIDTH
    return cols


def _take_cols(w, cols):
    valid = jnp.asarray(cols >= 0)
    return jnp.where(valid, jnp.take(w, jnp.asarray(np.maximum(cols, 0)), axis=-1), 0.0)


def _swa_head_perm():
    group = SWA_HEADS // SWA_KV_HEADS
    return np.concatenate([np.arange((kv * group + g) * 64, (kv * group + g + 1) * 64)
                           for g in range(group) for kv in range(SWA_KV_HEADS)])


def _rope_tables(n_ctx, n_lat):
    t = np.arange(n_lat)
    pos = np.stack([t // GRID_W, t % GRID_W]).astype(np.float32)

    def build(head_w, rope_lo, rope_w):
        da = rope_w // 2
        half = da // 2
        inv = ROPE_THETA ** (-np.arange(half, dtype=np.float32) / half)
        c = np.ones((n_lat, LANES), np.float32)
        sa = np.zeros((n_lat, LANES), np.float32)
        sb = np.zeros((n_lat, LANES), np.float32)
        for lane in range(LANES):
            dd = lane % head_w - rope_lo
            if dd < 0 or dd >= rope_w:
                continue
            axis, r = dd // da, dd % da
            ang = pos[axis] * inv[r % half]
            c[:, lane] = np.cos(ang)
            if r < half:
                sa[:, lane] = -np.sin(ang)
            else:
                sb[:, lane] = np.sin(ang)
        ctx = [np.ones((n_ctx, LANES), np.float32), np.zeros((n_ctx, LANES), np.float32),
               np.zeros((n_ctx, LANES), np.float32)]
        return [jnp.asarray(np.concatenate([cx, lt])) for cx, lt in zip(ctx, (c, sa, sb))]

    return build(LANES, MLA_NOPE, MLA_ROPE) + build(SWA_HEAD_DIM, 0, SWA_HEAD_DIM)


def kernel(x, c, ctx, c_ctx, w_mod, b_mod, norm1, norm2, w_in, mla_q_norm, mla_kv_norm, w_uq, w_ukv,
           swa_sink, gdn_conv, gdn_a_log, gdn_dt_bias, gdn_norm, w_branch_a, w_branch_b, w_branch_c,
           w_out, ffn_up, ffn_conv, ffn_down, norm_f):
    b, s, d = x.shape
    n_ctx = ctx.shape[1]
    depth = w_mod.shape[0]
    tm = TOKEN_TILE
    assert n_ctx % tm == 0 and s % tm == 0 and s >= 3 * SWA_TILE
    nctt = n_ctx // tm
    ntt = (n_ctx + s) // tm

    rows = -(-(b + 1) // SUBLANES) * SUBLANES
    cc = jnp.zeros((rows, d), F32).at[0:b].set(c).at[b].set(c_ctx)
    mod = _modulation(cc, w_mod, b_mod).reshape(depth, rows, 1, 6 * d)

    win = _take_cols(w_in, _win_columns()).astype(BF16)
    hq = np.arange(MLA_HEADS)[:, None]
    uq_cols = np.concatenate([hq * (MLA_NOPE + MLA_ROPE) + np.arange(MLA_NOPE + MLA_ROPE)[None],
                              -np.ones((MLA_HEADS, LANES - MLA_NOPE - MLA_ROPE), np.int64)], axis=1).reshape(-1)
    wuq = _take_cols(w_uq, uq_cols).astype(BF16)
    uk_cols = np.concatenate([hq * (MLA_NOPE + MLA_V) + np.arange(MLA_NOPE)[None],
                              -np.ones((MLA_HEADS, LANES - MLA_NOPE), np.int64)], axis=1).reshape(-1)
    wukvk = _take_cols(w_ukv, uk_cols).astype(BF16)
    uv_cols = (hq * (MLA_NOPE + MLA_V) + MLA_NOPE + np.arange(MLA_V)[None]).reshape(-1)
    wukvv = _take_cols(w_ukv, uv_cols).astype(BF16)
    wpa = w_branch_a.astype(BF16)
    wpb = jnp.take(w_branch_b, jnp.asarray(_swa_head_perm()), axis=1).astype(BF16)
    wpc = w_branch_c.astype(BF16)
    wout = w_out.astype(BF16)
    wup = ffn_up.astype(BF16)
    wdn = ffn_down.astype(BF16)
    sink = jnp.broadcast_to(swa_sink[:, :, None], (depth, SWA_HEADS, LANES))
    onorm = jnp.tile(gdn_norm, (1, GDN_HEADS)).reshape(depth, 1, GDN_HEADS * GDN_DV)
    pad_rows = lambda a: jnp.pad(a, ((0, 0), (0, SUBLANES - a.shape[1]), (0, 0)))
    gconv = pad_rows(gdn_conv)
    fconv = pad_rows(ffn_conv)
    gpar = jnp.zeros((depth, SUBLANES, LANES), F32)
    gpar = gpar.at[:, 0, 0:2 * GDN_HEADS].set(gdn_a_log.reshape(depth, -1))
    gpar = gpar.at[:, 1, 0:2 * GDN_HEADS].set(gdn_dt_bias.reshape(depth, -1))
    rope_tabs = _rope_tables(n_ctx, s)

    xs = jnp.concatenate([ctx, x], axis=1)
    row2 = lambda a: a.reshape(1, -1)
    for l in range(depth):
        last = l == depth - 1
        t_off = nctt if last else 0
        qm, km, vm, sq, sk, sv, qkv, z, ab, gate = _proj(
            xs, mod[l], row2(norm1[l]), win[l], row2(mla_q_norm[l]), row2(mla_kv_norm[l]),
            wuq[l], wukvk[l], wukvv[l], rope_tabs, nctt, b)
        ya = _mla(qm, km, vm, n_ctx, t_off)
        yb = _swa(sink[l], sq, sk, sv, n_ctx, t_off * (tm // SWA_TILE))
        a_mat, rv, rk, qk, qe, ke, dl = _gdn_prep(qkv, ab, gconv[l], gpar[l], (0, nctt), (nctt - 1, ntt - 1))
        u, w = _gdn_apply(_gdn_solve(a_mat), rv, rk)
        oc = _gdn_scan(u, w, qk, qe, ke, dl, n_ctx)
        x1 = _merge(xs, mod[l], ya, yb, oc, z, gate, onorm[l], wpa[l], wpb[l], wpc[l], wout[l],
                    nctt, b, t_off)
        if last:
            xs = _ffn(x1, mod[l], row2(norm2[l]), wup[l], fconv[l], wdn[l], row2(norm_f),
                      0, b, (0,), (s // tm - 1,), True)
        else:
            xs = _ffn(x1, mod[l], row2(norm2[l]), wup[l], fconv[l], wdn[l], row2(norm_f),
                      nctt, b, (0, nctt), (nctt - 1, ntt - 1), False)
    return xs
```

```python
import functools

import numpy as np
import jax
import jax.numpy as jnp
from jax import lax
from jax.experimental import pallas as pl
from jax.experimental.pallas import tpu as pltpu

F32 = jnp.float32
BF16 = jnp.bfloat16

GRID_W = 64
ROPE_THETA = 10000.0
NEG_INF = -1e30
EPS = 1e-6
LOG2E = 1.4426950408889634

MLA_HEADS = 8
MLA_Q_RANK = 384
MLA_KV_RANK = 256
MLA_NOPE = 64
MLA_ROPE = 32
MLA_V = 64

SWA_HEADS = 8
SWA_KV_HEADS = 2
SWA_HEAD_DIM = 64
WINDOW = 128

GDN_HEADS = 8
GDN_DK = 64
GDN_DV = 64
GDN_CHUNK = 64

FFN_DIM = 2816

LANES = 128
SUBLANES = 8
VMEM_LIMIT_BYTES = 56 * 1024 * 1024

TOKEN_TILE = 256
SWA_TILE = 128
PACK = 4
PACK_W = PACK * GDN_DK
FFN_SPLIT = 2


def _sigmoid(x):
    return 1.0 / (1.0 + jnp.exp(-x))


def _silu(x):
    return x * _sigmoid(x)


def _softplus(x):
    return jnp.maximum(x, 0.0) + jnp.log(1.0 + jnp.exp(-jnp.abs(x)))


def _rms(x, g):
    return x * lax.rsqrt(jnp.mean(x * x, axis=-1, keepdims=True) + EPS) * g


def _mm(a, b):
    return jnp.dot(a.astype(BF16), b.astype(BF16), preferred_element_type=F32)


def _mm_nt(a, b):
    return lax.dot_general(a.astype(BF16), b.astype(BF16), (((1,), (1,)), ((), ())),
                           preferred_element_type=F32)


def _mm_tn(a, b):
    return lax.dot_general(a.astype(BF16), b.astype(BF16), (((0,), (0,)), ((), ())),
                           preferred_element_type=F32)


def _split_bf16(x, parts):
    out = []
    for _ in range(parts - 1):
        hi = x.astype(BF16)
        out.append(hi)
        x = x - hi.astype(F32)
    out.append(x.astype(BF16))
    return out


def _group_sum(x, ones_blk):
    n = x.shape[0]
    r = jnp.dot(jnp.concatenate(_split_bf16(x, 2), axis=0), ones_blk, preferred_element_type=F32)
    return r[0:n] + r[n:2 * n]


def _iota(shape, dim):
    return lax.broadcasted_iota(jnp.int32, shape, dim)


def _group_ones(n, group):
    sh = group.bit_length() - 1
    return (jnp.right_shift(_iota((n, n), 0), sh) == jnp.right_shift(_iota((n, n), 1), sh)).astype(F32)


def _rope(x, c, sa, sb, half):
    n = x.shape[-1]
    return x * c + pltpu.roll(x, n - half, 1) * sa + pltpu.roll(x, half, 1) * sb


def _cparams(sem):
    return pltpu.CompilerParams(dimension_semantics=sem, vmem_limit_bytes=VMEM_LIMIT_BYTES)


def _mod_kernel(c_ref, w_ref, b_ref, o_ref):
    s = _silu(c_ref[...])
    o_ref[0] = _mm(s, w_ref[0]) + b_ref[0]


def _modulation(cc, w_mod, b_mod):
    depth, d, n = w_mod.shape
    r = cc.shape[0]
    tn = 1024
    return pl.pallas_call(
        _mod_kernel,
        grid=(depth, n // tn),
        in_specs=[pl.BlockSpec((r, d), lambda l, j: (0, 0)),
                  pl.BlockSpec((1, d, tn), lambda l, j: (l, 0, j)),
                  pl.BlockSpec((1, 1, tn), lambda l, j: (l, 0, j))],
        out_specs=pl.BlockSpec((1, r, tn), lambda l, j: (l, 0, j)),
        out_shape=jax.ShapeDtypeStruct((depth, r, n), F32),
        compiler_params=_cparams(("parallel", "parallel")),
        name="modulation",
    )(cc, w_mod, b_mod.reshape(depth, 1, n))


_P_CQ = (0, 384)
_P_CKV = (384, 640)
_P_KR = (640, 768)
_P_SQ = (768, 1280)
_P_SK = (1280, 1408)
_P_SV = (1408, 1536)
_P_QKV = (1536, 3072)
_P_Z = (3072, 3584)
_P_AB = (3584, 3712)
_P_GATE = (3712, 6784)
_P_WIDTH = 6784


def _proj_kernel(x_ref, mod_ref, g1_ref, win_ref, qn_ref, kvn_ref, wuq_ref, wukvk_ref, wukvv_ref,
                 mc_ref, ma_ref, mb_ref, sc_ref, sa_ref, sb_ref,
                 qm_ref, km_ref, vm_ref, sq_ref, sk_ref, sv_ref, qkv_ref, z_ref, ab_ref, gate_ref):
    x = x_ref[0]
    d = x.shape[-1]
    m = mod_ref[0]
    xb = (_rms(x, g1_ref[...]) * (1.0 + m[:, d:2 * d]) + m[:, 0:d]).astype(BF16)

    def proj(piece):
        return jnp.dot(xb, win_ref[:, piece[0]:piece[1]], preferred_element_type=F32)

    mc, ma, mb = mc_ref[...], ma_ref[...], mb_ref[...]
    sc, sa, sb = sc_ref[...], sa_ref[...], sb_ref[...]

    cq = _rms(proj(_P_CQ), qn_ref[...])
    q = _mm(cq, wuq_ref[...]) * ((MLA_NOPE + MLA_ROPE) ** -0.5 * LOG2E)
    ckv = _rms(proj(_P_CKV), kvn_ref[...])
    kn = _mm(ckv, wukvk_ref[...])
    vm_ref[0] = _mm(ckv, wukvv_ref[...]).astype(vm_ref.dtype)
    kr = _rope(proj(_P_KR), mc, ma, mb, MLA_ROPE // 4)
    for h in range(MLA_HEADS):
        sl = slice(h * LANES, (h + 1) * LANES)
        qm_ref[0, :, sl] = _rope(q[:, sl], mc, ma, mb, MLA_ROPE // 4).astype(qm_ref.dtype)
        km_ref[0, :, sl] = (kn[:, sl] + kr).astype(km_ref.dtype)

    sq = proj(_P_SQ) * (SWA_HEAD_DIM ** -0.5 * LOG2E)
    for g in range(SWA_HEADS * SWA_HEAD_DIM // LANES):
        sl = slice(g * LANES, (g + 1) * LANES)
        sq_ref[0, :, sl] = _rope(sq[:, sl], sc, sa, sb, SWA_HEAD_DIM // 4).astype(sq_ref.dtype)
    sk_ref[0] = _rope(proj(_P_SK), sc, sa, sb, SWA_HEAD_DIM // 4).astype(sk_ref.dtype)
    sv_ref[0] = proj(_P_SV).astype(sv_ref.dtype)

    qkv_ref[0] = proj(_P_QKV)
    z_ref[0] = proj(_P_Z).astype(z_ref.dtype)
    ab_ref[0] = proj(_P_AB)
    gate_ref[0] = proj(_P_GATE).astype(gate_ref.dtype)


def _proj(xs, mod_l, g1, win, qn, kvn, wuq, wukvk, wukvv, rope_tabs, nctt, ctx_row):
    b, t, d = xs.shape
    tm = TOKEN_TILE
    nt = t // tm

    def tok(w):
        return pl.BlockSpec((1, tm, w), lambda i, j: (i, j, 0))

    def full(a):
        return pl.BlockSpec(a.shape, lambda i, j: (0,) * a.ndim)

    tab = pl.BlockSpec((tm, LANES), lambda i, j: (j, 0))
    mod_spec = pl.BlockSpec((1, 1, mod_l.shape[-1]),
                            lambda i, j: (jnp.where(j < nctt, ctx_row, i), 0, 0))
    widths = (1024, 1024, 512, 512, 128, 128, 1536, 512, 128, 3072)
    dtypes = (BF16, BF16, BF16, BF16, BF16, BF16, F32, BF16, F32, BF16)
    return pl.pallas_call(
        _proj_kernel,
        grid=(b, nt),
        in_specs=[tok(d), mod_spec, full(g1), full(win), full(qn), full(kvn), full(wuq),
                  full(wukvk), full(wukvv)] + [tab] * 6,
        out_specs=[tok(w) for w in widths],
        out_shape=[jax.ShapeDtypeStruct((b, t, w), dt) for w, dt in zip(widths, dtypes)],
        compiler_params=_cparams(("parallel", "parallel")),
        name="proj",
    )(xs, mod_l, g1, win, qn, kvn, wuq, wukvk, wukvv, *rope_tabs)


def _mla_kernel(q_ref, k_ref, v_ref, o_ref, *, n_ctx, q_off):
    i = pl.program_id(2) + q_off
    tq = q_ref.shape[1]
    nctt = n_ctx // tq
    low = _iota((tq, LANES), 1) < MLA_V

    def run(nk):
        v = v_ref[0, 0:nk, :]
        v_low = _iota((nk, LANES), 1) < MLA_V
        one = jnp.ones_like(v)
        outs = []
        for hh in range(2):
            q = q_ref[0, :, hh * LANES:(hh + 1) * LANES]
            k = k_ref[0, 0:nk, hh * LANES:(hh + 1) * LANES]
            s = _mm_nt(q, k)
            p = jnp.exp2(s - jnp.max(s, axis=-1, keepdims=True))
            acc = _mm(p, jnp.where(v_low if hh == 0 else jnp.logical_not(v_low), v, one))
            den = acc[:, MLA_V:MLA_V + 1] if hh == 0 else acc[:, 0:1]
            outs.append(acc / den)
        o_ref[0] = jnp.where(low, outs[0], outs[1]).astype(o_ref.dtype)

    if q_off < nctt:
        @pl.when(i < nctt)
        def _():
            run(n_ctx)

    @pl.when(i >= nctt)
    def _():
        run(k_ref.shape[1])


def _mla(qm, km, vm, n_ctx, q_off):
    b, t, _ = qm.shape
    tq = TOKEN_TILE
    nt = t // tq - q_off
    return pl.pallas_call(
        functools.partial(_mla_kernel, n_ctx=n_ctx, q_off=q_off),
        grid=(b, MLA_HEADS // 2, nt),
        in_specs=[pl.BlockSpec((1, tq, 2 * LANES), lambda i, j, n: (i, n + q_off, j)),
                  pl.BlockSpec((1, t, 2 * LANES), lambda i, j, n: (i, 0, j)),
                  pl.BlockSpec((1, t, LANES), lambda i, j, n: (i, 0, j))],
        out_specs=pl.BlockSpec((1, tq, LANES), lambda i, j, n: (i, n + q_off, j)),
        out_shape=jax.ShapeDtypeStruct((b, t, MLA_HEADS * MLA_V), BF16),
        compiler_params=_cparams(("parallel", "parallel", "parallel")),
        name="mla",
    )(qm, km, vm)


def _swa_kernel(sink_ref, q_ref, k_ref, v_ref, o_ref, *, n_ctx, q_off):
    i = pl.program_id(1) + q_off
    blk = q_ref.shape[1]
    t = k_ref.shape[1]
    n_lat = t - n_ctx
    nctt = n_ctx // blk
    low = _iota((blk, LANES), 1) < SWA_HEAD_DIM
    n_slab = SWA_HEADS * SWA_HEAD_DIM // LANES
    half = n_slab * blk

    parts, sinks = [], []
    for hh in range(2):
        for g in range(n_slab):
            qs = q_ref[0, :, g * LANES:(g + 1) * LANES]
            parts.append(jnp.where(low if hh == 0 else jnp.logical_not(low), qs, jnp.zeros_like(qs)))
            head = hh * n_slab + g
            sinks.append(jnp.broadcast_to(sink_ref[head:head + 1, 0:1] * LOG2E, (blk, 1)))
    q_all = jnp.concatenate(parts, axis=0)
    sink = jnp.concatenate(sinks, axis=0)

    def with_ones(v, hh):
        v_low = _iota(v.shape, 1) < SWA_HEAD_DIM
        return jnp.where(v_low if hh == 0 else jnp.logical_not(v_low), v, jnp.ones_like(v))

    def attend(loc):
        kc = k_ref[0, 0:n_ctx, :]
        vc = v_ref[0, 0:n_ctx, :]
        s_c = _mm_nt(q_all, kc)
        mx = jnp.maximum(jnp.max(s_c, axis=-1, keepdims=True), sink)
        if loc is not None:
            kl, vl, valid = loc
            s_l = jnp.where(valid, _mm_nt(q_all, kl), NEG_INF)
            mx = jnp.maximum(mx, jnp.max(s_l, axis=-1, keepdims=True))
        p_c = jnp.exp2(s_c - mx).astype(BF16)
        p_sink = jnp.exp2(sink - mx)
        if loc is not None:
            p_l = jnp.exp2(s_l - mx).astype(BF16)
        outs = []
        for hh in range(2):
            rows = slice(hh * half, (hh + 1) * half)
            acc = _mm(p_c[rows], with_ones(vc, hh))
            if loc is not None:
                acc = acc + _mm(p_l[rows], with_ones(vl, hh))
            den = (acc[:, SWA_HEAD_DIM:SWA_HEAD_DIM + 1] if hh == 0 else acc[:, 0:1]) + p_sink[rows]
            outs.append(acc / den)
        for g in range(n_slab):
            rows = slice(g * blk, (g + 1) * blk)
            o_ref[0, :, g * LANES:(g + 1) * LANES] = jnp.where(low, outs[0][rows], outs[1][rows]).astype(o_ref.dtype)

    if q_off < nctt:
        @pl.when(i < nctt)
        def _():
            attend(None)

    @pl.when(i >= nctt)
    def _():
        n = i - nctt
        span = 3 * blk
        start = jnp.clip((n - 1) * blk, 0, n_lat - span)
        row0 = pl.multiple_of(n_ctx + start, blk)
        kl = k_ref[0, pl.ds(row0, span), :]
        vl = v_ref[0, pl.ds(row0, span), :]
        kpos = start + _iota((blk, span), 1)
        qpos = n * blk + _iota((blk, span), 0)
        valid = jnp.abs(kpos - qpos) <= WINDOW
        attend((kl, vl, jnp.concatenate([valid] * (2 * n_slab), axis=0)))


def _swa(sink, sq, sk, sv, n_ctx, q_off):
    b, t, w = sq.shape
    blk = SWA_TILE
    nt = t // blk - q_off
    return pl.pallas_call(
        functools.partial(_swa_kernel, n_ctx=n_ctx, q_off=q_off),
        grid=(b, nt),
        in_specs=[pl.BlockSpec(sink.shape, lambda i, n: (0, 0)),
                  pl.BlockSpec((1, blk, w), lambda i, n: (i, n + q_off, 0)),
                  pl.BlockSpec((1, t, LANES), lambda i, n: (i, 0, 0)),
                  pl.BlockSpec((1, t, LANES), lambda i, n: (i, 0, 0))],
        out_specs=pl.BlockSpec((1, blk, w), lambda i, n: (i, n + q_off, 0)),
        out_shape=jax.ShapeDtypeStruct((b, t, w), BF16),
        compiler_params=_cparams(("parallel", "parallel")),
        name="swa",
    )(sink, sq, sk, sv)


def _block_diag(x, bd_mask):
    return jnp.where(bd_mask, jnp.concatenate([x] * PACK, axis=0), 0.0)


def _gdn_prep_kernel(qkv_ref, prev_ref, next_ref, ab_ref, cw_ref, gp_ref,
                     a_ref, rv_ref, rk_ref, qk_ref, qe_ref, ke_ref, dl_ref, *, seg_starts, seg_ends):
    ti = pl.program_id(1)
    tm = qkv_ref.shape[1]
    hw = GDN_HEADS * GDN_DK
    x = qkv_ref[0]
    row = _iota((tm, 1), 0)
    is_start = functools.reduce(jnp.logical_or, [ti == s for s in seg_starts])
    is_end = functools.reduce(jnp.logical_or, [ti == s for s in seg_ends])
    prev_row = jnp.where(is_start, 0.0, prev_ref[0, SUBLANES - 1:SUBLANES, :])
    next_row = jnp.where(is_end, 0.0, next_ref[0, 0:1, :])
    x_prev = jnp.where(row == 0, prev_row, pltpu.roll(x, 1, 0))
    x_next = jnp.where(row == tm - 1, next_row, pltpu.roll(x, tm - 1, 0))
    y = _silu(x_prev * cw_ref[0:1, :] + x * cw_ref[1:2, :] + x_next * cw_ref[2:3, :])

    ones_blk = _group_ones(LANES, GDN_DK).astype(BF16)

    def l2(slab):
        return slab * lax.rsqrt(_group_sum(slab * slab, ones_blk) + EPS)

    q = jnp.concatenate([l2(y[:, s * LANES:(s + 1) * LANES]) for s in range(hw // LANES)], axis=1)
    q = q * (GDN_DK ** -0.5)
    k = jnp.concatenate([l2(y[:, hw + s * LANES:hw + (s + 1) * LANES]) for s in range(hw // LANES)], axis=1)
    v = y[:, 2 * hw:3 * hw]

    ab = ab_ref[0]
    g_all = -jnp.exp(gp_ref[0:1, :]) * _softplus(ab + gp_ref[1:2, :])
    beta_all = _sigmoid(ab)
    ri = _iota((tm, tm), 0)
    ci = _iota((tm, tm), 1)
    same = jnp.right_shift(ri, 6) == jnp.right_shift(ci, 6)
    g_parts = jnp.concatenate(_split_bf16(g_all, 3), axis=1)

    def cumsum(tri):
        r = jnp.dot(jnp.logical_and(same, tri).astype(BF16), g_parts, preferred_element_type=F32)
        return r[:, 0:LANES] + r[:, LANES:2 * LANES] + r[:, 2 * LANES:3 * LANES]

    gam_all = jnp.where(_iota((tm, LANES), 1) < GDN_HEADS, cumsum(ci <= ri), cumsum(ci >= ri))
    col_head = jnp.right_shift(_iota((LANES, 2 * hw), 1), GDN_DK.bit_length() - 1)
    src = _iota((LANES, 2 * hw), 0)

    def expand(x, parts, sel):
        r = jnp.dot(jnp.concatenate(_split_bf16(x, parts), axis=0), sel.astype(BF16), preferred_element_type=F32)
        return functools.reduce(lambda a, b: a + b, [r[p * tm:(p + 1) * tm] for p in range(parts)])

    gam_exp = expand(gam_all, 3, src == col_head)
    b_exp = expand(beta_all, 2, src == col_head + 2 * GDN_HEADS)
    gam_f, gam_b = gam_exp[:, 0:hw], gam_exp[:, hw:2 * hw]

    c = GDN_CHUNK
    r64 = _iota((c, PACK_W), 0)
    c64 = jnp.bitwise_and(_iota((c, PACK_W), 1), c - 1)
    eye = (r64 == c64).astype(F32)
    bd_mask = jnp.right_shift(_iota((PACK_W, PACK_W), 0), 6) == jnp.right_shift(_iota((PACK_W, PACK_W), 1), 6)

    for ch in range(tm // c):
        rows = slice(ch * c, (ch + 1) * c)
        for gi in range(hw // PACK_W):
            lanes = slice(gi * PACK_W, (gi + 1) * PACK_W)
            kc, qc, vc = k[rows, lanes], q[rows, lanes], v[rows, lanes]
            kbd = _block_diag(kc, bd_mask)
            kq = _mm_nt(jnp.concatenate([kc, qc], axis=0), kbd)
            kk, qkm = kq[0:c], kq[c:2 * c]
            for d in range(2):
                gam = (gam_f if d == 0 else gam_b)[rows, lanes]
                beta = b_exp[rows, d * hw + gi * PACK_W:d * hw + (gi + 1) * PACK_W]
                gam_row = jnp.sum(gam * eye, axis=0, keepdims=True)
                decay = jnp.exp(jnp.minimum(gam - gam_row, 0.0))
                strict = (r64 > c64) if d == 0 else (r64 < c64)
                incl = (r64 >= c64) if d == 0 else (r64 <= c64)
                a_mat = jnp.where(strict, beta * kk * decay, 0.0)
                for pp in range(PACK_W // LANES):
                    a_ref[d, :, ch * (hw // LANES) + gi * (PACK_W // LANES) + pp, :] = \
                        a_mat[:, pp * LANES:(pp + 1) * LANES]
                e_gam = jnp.exp(gam)
                g_last = gam[c - 1:c, :] if d == 0 else gam[0:1, :]
                out = slice(d * hw + gi * PACK_W, d * hw + (gi + 1) * PACK_W)
                rv_ref[0, rows, out] = vc * beta
                rk_ref[0, rows, out] = kc * beta * e_gam
                qk_ref[0, rows, out] = jnp.where(incl, qkm * decay, 0.0).astype(qk_ref.dtype)
                qe_ref[0, rows, out] = (qc * e_gam).astype(qe_ref.dtype)
                ke_ref[0, rows, out] = (kc * jnp.exp(g_last - gam)).astype(ke_ref.dtype)
                dl_ref[0, ch, :, out] = jnp.exp(g_last)


def _gdn_prep(qkv, ab, cw, gp, seg_starts, seg_ends):
    b, t, w = qkv.shape
    tm = TOKEN_TILE
    nt = t // tm
    hb = tm // SUBLANES
    nh = t // SUBLANES
    wide = 2 * GDN_HEADS * GDN_DK
    c = GDN_CHUNK
    cpt = tm // c
    spt = cpt * (GDN_HEADS * GDN_DK // LANES)
    tok = lambda width: pl.BlockSpec((1, tm, width), lambda i, j: (i, j, 0))
    out_shape = [jax.ShapeDtypeStruct((2, c, b * nt * spt, LANES), F32)] + \
                [jax.ShapeDtypeStruct((b, t, wide), F32)] * 2 + \
                [jax.ShapeDtypeStruct((b, t, wide), BF16)] * 3 + \
                [jax.ShapeDtypeStruct((b, t // c, 1, wide), F32)]
    return pl.pallas_call(
        functools.partial(_gdn_prep_kernel, seg_starts=seg_starts, seg_ends=seg_ends),
        grid=(b, nt),
        in_specs=[tok(w),
                  pl.BlockSpec((1, SUBLANES, w), lambda i, j: (i, jnp.maximum(j * hb - 1, 0), 0)),
                  pl.BlockSpec((1, SUBLANES, w), lambda i, j: (i, jnp.minimum((j + 1) * hb, nh - 1), 0)),
                  tok(LANES),
                  pl.BlockSpec(cw.shape, lambda i, j: (0, 0)),
                  pl.BlockSpec(gp.shape, lambda i, j: (0, 0))],
        out_specs=[pl.BlockSpec((2, c, spt, LANES), lambda i, j: (0, 0, i * nt + j, 0))] +
                  [tok(wide)] * 5 + [pl.BlockSpec((1, cpt, 1, wide), lambda i, j: (i, j, 0, 0))],
        out_shape=out_shape,
        compiler_params=_cparams(("parallel", "parallel")),
        name="gdn_prep",
    )(qkv, qkv, qkv, ab, cw, gp)


def _gdn_solve_kernel(a_ref, o_ref, at_ref, x_ref, acc_ref):
    c = GDN_CHUNK
    sub = SUBLANES
    nb = c // sub
    for i in range(c):
        at_ref[i * LANES:(i + 1) * LANES, :] = a_ref[0, i].T
    x_ref[...] = jnp.zeros_like(x_ref)
    row_iota = _iota((LANES, LANES), 0)

    def solve(upper):
        def row(step, carry):
            i = (c - 1 - step) if upper else step
            base = pl.multiple_of(i * LANES, LANES)
            acc_ref[...] = (jnp.bitwise_and(row_iota, c - 1) == i).astype(F32)
            for kb in range(nb):
                active = (kb * sub + sub - 1 > i) if upper else (kb * sub < i)

                @pl.when(active)
                def _():
                    for h2 in range(LANES // c):
                        coefs = [at_ref[pl.ds(base + h2 * c + kb * sub + kk, 1), :] for kk in range(sub)]
                        for jb in (range(kb, nb) if upper else range(kb + 1)):
                            dst = slice(h2 * c + jb * sub, h2 * c + (jb + 1) * sub)
                            acc = acc_ref[dst, :]
                            for kk in range(sub):
                                lo = (kb * sub + kk) * LANES + h2 * c + jb * sub
                                acc = acc - coefs[kk] * x_ref[lo:lo + sub, :]
                            acc_ref[dst, :] = acc
            x_ref[pl.ds(base, LANES), :] = acc_ref[...]
            return carry

        lax.fori_loop(0, c, row, 0)

    @pl.when(pl.program_id(0) == 0)
    def _():
        solve(False)

    @pl.when(pl.program_id(0) == 1)
    def _():
        solve(True)

    for i in range(c):
        o_ref[0, i] = x_ref[i * LANES:(i + 1) * LANES, :].T


def _gdn_solve(a):
    _, c, n_slab, _ = a.shape
    assert n_slab % LANES == 0
    blk = pl.BlockSpec((1, c, LANES, LANES), lambda d, g: (d, 0, g, 0))
    return pl.pallas_call(
        _gdn_solve_kernel,
        grid=(2, n_slab // LANES),
        in_specs=[blk],
        out_specs=blk,
        out_shape=jax.ShapeDtypeStruct(a.shape, F32),
        scratch_shapes=[pltpu.VMEM((c * LANES, LANES), F32), pltpu.VMEM((c * LANES, LANES), F32),
                        pltpu.VMEM((LANES, LANES), F32)],
        compiler_params=_cparams(("parallel", "parallel")),
        name="gdn_solve",
    )(a)


def _gdn_apply_kernel(t_ref, rv_ref, rk_ref, u_ref, w_ref):
    tm = rv_ref.shape[1]
    c = GDN_CHUNK
    hw = GDN_HEADS * GDN_DK
    ppg = PACK_W // LANES
    bd_mask = jnp.right_shift(_iota((PACK_W, PACK_W), 0), 6) == jnp.right_shift(_iota((PACK_W, PACK_W), 1), 6)
    for ch in range(tm // c):
        rows = slice(ch * c, (ch + 1) * c)
        for gi in range(hw // PACK_W):
            for d in range(2):
                out = slice(d * hw + gi * PACK_W, d * hw + (gi + 1) * PACK_W)
                tinv = jnp.concatenate([t_ref[d, :, ch * (hw // LANES) + gi * ppg + pp, :] for pp in range(ppg)],
                                       axis=1)
                t_hi, t_lo = _split_bf16(tinv, 2)
                v_hi, v_lo = _split_bf16(rv_ref[0, rows, out], 2)
                top = jnp.dot(jnp.concatenate([t_hi, t_lo], axis=0), _block_diag(v_hi, bd_mask),
                              preferred_element_type=F32)
                u_ref[0, rows, out] = (top[0:c] + top[c:2 * c]
                                       + jnp.dot(t_hi, _block_diag(v_lo, bd_mask), preferred_element_type=F32))
                w_ref[0, rows, out] = jnp.dot(t_hi, _block_diag(rk_ref[0, rows, out].astype(BF16), bd_mask),
                                              preferred_element_type=F32).astype(w_ref.dtype)


def _gdn_apply(tinv, rv, rk):
    b, t, wide = rv.shape
    tm = TOKEN_TILE
    c = GDN_CHUNK
    nt = t // tm
    spt = (tm // c) * (GDN_HEADS * GDN_DK // LANES)
    tok = pl.BlockSpec((1, tm, wide), lambda i, j: (i, j, 0))
    return pl.pallas_call(
        _gdn_apply_kernel,
        grid=(b, nt),
        in_specs=[pl.BlockSpec((2, c, spt, LANES), lambda i, j: (0, 0, i * nt + j, 0)), tok, tok],
        out_specs=[tok, tok],
        out_shape=[jax.ShapeDtypeStruct((b, t, wide), F32), jax.ShapeDtypeStruct((b, t, wide), BF16)],
        compiler_params=_cparams(("parallel", "parallel")),
        name="gdn_apply",
    )(tinv, rv, rk)


def _gdn_scan_kernel(uf, wf, qkf, qef, kef, dlf, ub, wb, qkb, qeb, keb, dlb, of_ref, ob_ref, s_ref):
    c = GDN_CHUNK
    cpt = uf.shape[1] // c
    ng = uf.shape[2] // PACK_W
    bd_mask = jnp.right_shift(_iota((PACK_W, PACK_W), 0), 6) == jnp.right_shift(_iota((PACK_W, PACK_W), 1), 6)

    @pl.when(pl.program_id(1) == 0)
    def _():
        s_ref[...] = jnp.zeros_like(s_ref)

    chains = ((uf, wf, qkf, qef, kef, dlf, of_ref), (ub, wb, qkb, qeb, keb, dlb, ob_ref))
    for j in range(cpt):
        for d, (u, w, qk, qe, ke, dl, o_ref) in enumerate(chains):
            ch = j if d == 0 else cpt - 1 - j
            rows = slice(ch * c, (ch + 1) * c)
            for gi in range(ng):
                lanes = slice(gi * PACK_W, (gi + 1) * PACK_W)
                state = s_ref[d * ng + gi]
                wq = _mm(jnp.concatenate([w[0, rows, lanes], qe[0, rows, lanes]], axis=0), state)
                v_new = u[0, rows, lanes] - wq[0:c]
                o_ref[0, rows, lanes] = wq[c:2 * c] + _mm(qk[0, rows, lanes], _block_diag(v_new, bd_mask))
                kv = _mm_tn(ke[0, rows, lanes], v_new)
                s_ref[d * ng + gi] = state * dl[0, ch, :, lanes] + jnp.where(bd_mask, kv, 0.0)


def _gdn_scan(u, w, qk, qe, ke, dl, n_ctx):
    b, t, wide = u.shape
    tm = TOKEN_TILE
    nt = t // tm
    nctt = n_ctx // tm
    hw = wide // 2
    cpt = tm // GDN_CHUNK

    def mirror(g):
        return jnp.where(g < nctt, nctt - 1 - g, nt - 1 - (g - nctt))

    fwd = pl.BlockSpec((1, tm, hw), lambda i, g: (i, g, 0))
    bwd = pl.BlockSpec((1, tm, hw), lambda i, g: (i, mirror(g), 1))
    dfwd = pl.BlockSpec((1, cpt, 1, hw), lambda i, g: (i, g, 0, 0))
    dbwd = pl.BlockSpec((1, cpt, 1, hw), lambda i, g: (i, mirror(g), 0, 1))
    out_sds = jax.ShapeDtypeStruct((b, t, hw), F32)
    return pl.pallas_call(
        _gdn_scan_kernel,
        grid=(b, nt),
        in_specs=[fwd] * 5 + [dfwd] + [bwd] * 5 + [dbwd],
        out_specs=[pl.BlockSpec((1, tm, hw), lambda i, g: (i, g, 0)),
                   pl.BlockSpec((1, tm, hw), lambda i, g: (i, mirror(g), 0))],
        out_shape=[out_sds, out_sds],
        scratch_shapes=[pltpu.VMEM((2 * hw // PACK_W, PACK_W, PACK_W), F32)],
        compiler_params=_cparams(("parallel", "arbitrary")),
        name="gdn_scan",
    )(u, w, qk, qe, ke, dl, u, w, qk, qe, ke, dl)


def _merge_kernel(x_ref, mod_ref, ya_ref, yb_ref, of_ref, ob_ref, z_ref, gate_ref, on_ref,
                  wpa_ref, wpb_ref, wpc_ref, wout_ref, o_ref):
    x = x_ref[0]
    d = x.shape[-1]
    m = mod_ref[0]
    oc = of_ref[0] + ob_ref[0]
    ones_blk = _group_ones(LANES, GDN_DV).astype(BF16)
    ms = jnp.concatenate(
        [_group_sum(oc[:, s * LANES:(s + 1) * LANES] ** 2, ones_blk) for s in range(oc.shape[-1] // LANES)],
        axis=1) * (1.0 / GDN_DV)
    yc = oc * lax.rsqrt(ms + EPS) * on_ref[...] * _silu(z_ref[0].astype(F32))
    gate = gate_ref[0].astype(F32)
    mix = (_sigmoid(gate[:, 0:d]) * _mm(ya_ref[0], wpa_ref[...])
           + _sigmoid(gate[:, d:2 * d]) * _mm(yb_ref[0], wpb_ref[...])
           + _sigmoid(gate[:, 2 * d:3 * d]) * _mm(yc, wpc_ref[...]))
    o_ref[0] = x + m[:, 2 * d:3 * d] * _mm(mix, wout_ref[...])


def _merge(xs, mod_l, ya, yb, oc_f, oc_b, z, gate, onorm, wpa, wpb, wpc, wout, nctt, ctx_row, t_off):
    b, t, d = xs.shape
    tm = TOKEN_TILE
    nt = t // tm - t_off
    tok = lambda width: pl.BlockSpec((1, tm, width), lambda i, j: (i, j + t_off, 0))
    full = lambda a: pl.BlockSpec(a.shape, lambda i, j: (0,) * a.ndim)
    mod_spec = pl.BlockSpec((1, 1, mod_l.shape[-1]),
                            lambda i, j: (jnp.where(j + t_off < nctt, ctx_row, i), 0, 0))
    return pl.pallas_call(
        _merge_kernel,
        grid=(b, nt),
        in_specs=[tok(d), mod_spec, tok(512), tok(512), tok(512), tok(512), tok(512), tok(3 * d),
                  full(onorm), full(wpa), full(wpb), full(wpc), full(wout)],
        out_specs=pl.BlockSpec((1, tm, d), lambda i, j: (i, j, 0)),
        out_shape=jax.ShapeDtypeStruct((b, nt * tm, d), F32),
        compiler_params=_cparams(("parallel", "parallel")),
        name="merge",
    )(xs, mod_l, ya, yb, oc_f, oc_b, z, gate, onorm, wpa, wpb, wpc, wout)


def _ffn_kernel(x_ref, prev_ref, next_ref, mod_ref, g2_ref, wup_ref, cw_ref, wdn_ref, nf_ref, o_ref,
                *, seg_starts, seg_ends, final):
    ti = pl.program_id(1)
    x = x_ref[0]
    tm, d = x.shape
    m = mod_ref[0]
    xall = jnp.concatenate([prev_ref[0], x, next_ref[0]], axis=0)
    xa = (_rms(xall, g2_ref[...]) * (1.0 + m[:, 4 * d:5 * d]) + m[:, 3 * d:4 * d]).astype(BF16)
    n_all = tm + 2 * SUBLANES
    row = _iota((tm, 1), 0)
    is_start = functools.reduce(jnp.logical_or, [ti == s for s in seg_starts])
    is_end = functools.reduce(jnp.logical_or, [ti == s for s in seg_ends])
    drop_prev = jnp.logical_and(row == 0, is_start)
    drop_next = jnp.logical_and(row == tm - 1, is_end)

    def conv(h, cols):
        hp = pltpu.roll(h, 1, 0)[SUBLANES:SUBLANES + tm]
        hn = pltpu.roll(h, n_all - 1, 0)[SUBLANES:SUBLANES + tm]
        hp = jnp.where(drop_prev, 0.0, hp)
        hn = jnp.where(drop_next, 0.0, hn)
        return (hp * cw_ref[0:1, cols] + h[SUBLANES:SUBLANES + tm] * cw_ref[1:2, cols]
                + hn * cw_ref[2:3, cols])

    cwid = FFN_DIM // FFN_SPLIT
    acc = jnp.zeros((tm, d), F32)
    for j in range(FFN_SPLIT):
        ca = slice(j * cwid, (j + 1) * cwid)
        cb = slice(FFN_DIM + j * cwid, FFN_DIM + (j + 1) * cwid)
        ha = conv(jnp.dot(xa, wup_ref[:, ca], preferred_element_type=F32), ca)
        hb = conv(jnp.dot(xa, wup_ref[:, cb], preferred_element_type=F32), cb)
        acc = acc + _mm(_silu(ha) * hb, wdn_ref[ca, :])
    y = x + m[:, 5 * d:6 * d] * acc
    if final:
        y = _rms(y, nf_ref[...])
    o_ref[0] = y


def _ffn(x1, mod_l, g2, wup, cw, wdn, nf, nctt_mod, ctx_row, seg_starts, seg_ends, final):
    b, t, d = x1.shape
    tm = TOKEN_TILE
    nt = t // tm
    hb = tm // SUBLANES
    nh = t // SUBLANES
    full = lambda a: pl.BlockSpec(a.shape, lambda i, j: (0,) * a.ndim)
    mod_spec = pl.BlockSpec((1, 1, mod_l.shape[-1]),
                            lambda i, j: (jnp.where(j < nctt_mod, ctx_row, i), 0, 0))
    return pl.pallas_call(
        functools.partial(_ffn_kernel, seg_starts=seg_starts, seg_ends=seg_ends, final=final),
        grid=(b, nt),
        in_specs=[pl.BlockSpec((1, tm, d), lambda i, j: (i, j, 0)),
                  pl.BlockSpec((1, SUBLANES, d), lambda i, j: (i, jnp.maximum(j * hb - 1, 0), 0)),
                  pl.BlockSpec((1, SUBLANES, d), lambda i, j: (i, jnp.minimum((j + 1) * hb, nh - 1), 0)),
                  mod_spec, full(g2), full(wup), full(cw), full(wdn), full(nf)],
        out_specs=pl.BlockSpec((1, tm, d), lambda i, j: (i, j, 0)),
        out_shape=jax.ShapeDtypeStruct((b, t, d), F32),
        compiler_params=_cparams(("parallel", "parallel")),
        name="ffn",
    )(x1, x1, x1, mod_l, g2, wup, cw, wdn, nf)


def _cols(w, segments):
    parts = [jnp.zeros(w.shape[:-1] + (seg,), w.dtype) if isinstance(seg, int) else w[..., seg[0]:seg[1]]
             for seg in segments]
    return jnp.concatenate(parts, axis=-1)


def _win_segments(d):
    widths = [MLA_Q_RANK, MLA_KV_RANK, MLA_ROPE, SWA_HEADS * SWA_HEAD_DIM, SWA_KV_HEADS * SWA_HEAD_DIM,
              SWA_KV_HEADS * SWA_HEAD_DIM, GDN_HEADS * (2 * GDN_DK + GDN_DV), GDN_HEADS * GDN_DV,
              2 * GDN_HEADS, 2 * GDN_HEADS, 3 * d]
    off = np.cumsum([0] + widths)
    cq, ckv, kr, sq, sk, sv, qkv, z, a, b, gate = [(int(off[i]), int(off[i + 1])) for i in range(11)]
    sq_perm = [(sq[0] + h * SWA_HEAD_DIM, sq[0] + (h + 1) * SWA_HEAD_DIM) for h in _swa_head_order()]
    segs = [cq, ckv, MLA_NOPE, kr, LANES - MLA_NOPE - MLA_ROPE] + sq_perm + \
           [sk, sv, qkv, z, a, b, LANES - 4 * GDN_HEADS, gate]
    return segs


def _swa_head_order():
    group = SWA_HEADS // SWA_KV_HEADS
    return [kv * group + g for g in range(group) for kv in range(SWA_KV_HEADS)]


def _rope_tables(n_ctx, n_lat):
    t = np.arange(n_lat)
    pos = np.stack([t // GRID_W, t % GRID_W]).astype(np.float32)

    def build(head_w, rope_lo, rope_w):
        da = rope_w // 2
        half = da // 2
        inv = ROPE_THETA ** (-np.arange(half, dtype=np.float32) / half)
        c = np.ones((n_lat, LANES), np.float32)
        sa = np.zeros((n_lat, LANES), np.float32)
        sb = np.zeros((n_lat, LANES), np.float32)
        for lane in range(LANES):
            dd = lane % head_w - rope_lo
            if dd < 0 or dd >= rope_w:
                continue
            axis, r = dd // da, dd % da
            ang = pos[axis] * inv[r % half]
            c[:, lane] = np.cos(ang)
            if r < half:
                sa[:, lane] = -np.sin(ang)
            else:
                sb[:, lane] = np.sin(ang)
        ctx = [np.ones((n_ctx, LANES), np.float32), np.zeros((n_ctx, LANES), np.float32),
               np.zeros((n_ctx, LANES), np.float32)]
        return [jnp.asarray(np.concatenate([cx, lt])) for cx, lt in zip(ctx, (c, sa, sb))]

    return build(LANES, MLA_NOPE, MLA_ROPE) + build(SWA_HEAD_DIM, 0, SWA_HEAD_DIM)


def kernel(x, c, ctx, c_ctx, w_mod, b_mod, norm1, norm2, w_in, mla_q_norm, mla_kv_norm, w_uq, w_ukv,
           swa_sink, gdn_conv, gdn_a_log, gdn_dt_bias, gdn_norm, w_branch_a, w_branch_b, w_branch_c,
           w_out, ffn_up, ffn_conv, ffn_down, norm_f):
    b, s, d = x.shape
    n_ctx = ctx.shape[1]
    depth = w_mod.shape[0]
    tm = TOKEN_TILE
    assert n_ctx % tm == 0 and s % tm == 0 and s >= 3 * SWA_TILE
    nctt = n_ctx // tm
    ntt = (n_ctx + s) // tm

    rows = -(-(b + 1) // SUBLANES) * SUBLANES
    cc = jnp.zeros((rows, d), F32).at[0:b].set(c).at[b].set(c_ctx)
    mod = _modulation(cc, w_mod, b_mod).reshape(depth, rows, 1, 6 * d)

    hd_q = MLA_NOPE + MLA_ROPE
    hd_kv = MLA_NOPE + MLA_V
    uq_segs = [s for h in range(MLA_HEADS) for s in ((h * hd_q, (h + 1) * hd_q), LANES - hd_q)]
    uk_segs = [s for h in range(MLA_HEADS) for s in ((h * hd_kv, h * hd_kv + MLA_NOPE), LANES - MLA_NOPE)]
    uv_segs = [(h * hd_kv + MLA_NOPE, (h + 1) * hd_kv) for h in range(MLA_HEADS)]
    win_segs = _win_segments(d)

    def layer_weights(l):
        wpb = jnp.concatenate([w_branch_b[l, h * SWA_HEAD_DIM:(h + 1) * SWA_HEAD_DIM] for h in _swa_head_order()],
                              axis=0)
        gpar = jnp.zeros((SUBLANES, LANES), F32)
        gpar = gpar.at[0, 0:2 * GDN_HEADS].set(gdn_a_log[l].reshape(-1))
        gpar = gpar.at[1, 0:2 * GDN_HEADS].set(gdn_dt_bias[l].reshape(-1))
        pad_rows = lambda a: jnp.pad(a, ((0, SUBLANES - a.shape[0]), (0, 0)))
        return dict(
            win=_cols(w_in[l], win_segs).astype(BF16),
            wuq=_cols(w_uq[l], uq_segs).astype(BF16),
            wukvk=_cols(w_ukv[l], uk_segs).astype(BF16),
            wukvv=_cols(w_ukv[l], uv_segs).astype(BF16),
            wpa=w_branch_a[l].astype(BF16), wpb=wpb.astype(BF16), wpc=w_branch_c[l].astype(BF16),
            wout=w_out[l].astype(BF16), wup=ffn_up[l].astype(BF16), wdn=ffn_down[l].astype(BF16),
            sink=jnp.broadcast_to(swa_sink[l][:, None], (SWA_HEADS, LANES)),
            onorm=jnp.tile(gdn_norm[l], GDN_HEADS).reshape(1, -1),
            gconv=pad_rows(gdn_conv[l]), fconv=pad_rows(ffn_conv[l]), gpar=gpar)

    rope_tabs = _rope_tables(n_ctx, s)

    xs = jnp.concatenate([ctx, x], axis=1)
    row2 = lambda a: a.reshape(1, -1)
    for l in range(depth):
        last = l == depth - 1
        t_off = nctt if last else 0
        p = layer_weights(l)
        qm, km, vm, sq, sk, sv, qkv, z, ab, gate = _proj(
            xs, mod[l], row2(norm1[l]), p["win"], row2(mla_q_norm[l]), row2(mla_kv_norm[l]),
            p["wuq"], p["wukvk"], p["wukvv"], rope_tabs, nctt, b)
        ya = _mla(qm, km, vm, n_ctx, t_off)
        yb = _swa(p["sink"], sq, sk, sv, n_ctx, t_off * (tm // SWA_TILE))
        a_mat, rv, rk, qk, qe, ke, dl = _gdn_prep(qkv, ab, p["gconv"], p["gpar"], (0, nctt), (nctt - 1, ntt - 1))
        u, w = _gdn_apply(_gdn_solve(a_mat), rv, rk)
        oc_f, oc_b = _gdn_scan(u, w, qk, qe, ke, dl, n_ctx)
        x1 = _merge(xs, mod[l], ya, yb, oc_f, oc_b, z, gate, p["onorm"], p["wpa"], p["wpb"], p["wpc"], p["wout"],
                    nctt, b, t_off)
        if last:
            xs = _ffn(x1, mod[l], row2(norm2[l]), p["wup"], p["fconv"], p["wdn"], row2(norm_f),
                      0, b, (0,), (s // tm - 1,), True)
        else:
            xs = _ffn(x1, mod[l], row2(norm2[l]), p["wup"], p["fconv"], p["wdn"], row2(norm_f),
                      nctt, b, (0, nctt), (nctt - 1, ntt - 1), False)
    return xs
```

```python
import functools

import numpy as np
import jax
import jax.numpy as jnp
from jax import lax
from jax.experimental import pallas as pl
from jax.experimental.pallas import tpu as pltpu

F32 = jnp.float32
BF16 = jnp.bfloat16

GRID_W = 64
ROPE_THETA = 10000.0
NEG_INF = -1e30
EPS = 1e-6
LOG2E = 1.4426950408889634

MLA_HEADS = 8
MLA_Q_RANK = 384
MLA_KV_RANK = 256
MLA_NOPE = 64
MLA_ROPE = 32
MLA_V = 64

SWA_HEADS = 8
SWA_KV_HEADS = 2
SWA_HEAD_DIM = 64
WINDOW = 128

GDN_HEADS = 8
GDN_DK = 64
GDN_DV = 64
GDN_CHUNK = 64

FFN_DIM = 2816

LANES = 128
SUBLANES = 8
VMEM_LIMIT_BYTES = 56 * 1024 * 1024

TOKEN_TILE = 256
SWA_TILE = 128
MLA_Q_TILE = 512
PACK = 4
PACK_W = PACK * GDN_DK
FFN_SPLIT = 2


def _sigmoid(x):
    return 1.0 / (1.0 + jnp.exp(-x))


def _silu(x):
    return x * _sigmoid(x)


def _softplus(x):
    return jnp.maximum(x, 0.0) + jnp.log(1.0 + jnp.exp(-jnp.abs(x)))


def _rms(x, g):
    return x * lax.rsqrt(jnp.mean(x * x, axis=-1, keepdims=True) + EPS) * g


def _mm(a, b):
    return jnp.dot(a.astype(BF16), b.astype(BF16), preferred_element_type=F32)


def _mm_nt(a, b):
    return lax.dot_general(a.astype(BF16), b.astype(BF16), (((1,), (1,)), ((), ())),
                           preferred_element_type=F32)


def _mm_tn(a, b):
    return lax.dot_general(a.astype(BF16), b.astype(BF16), (((0,), (0,)), ((), ())),
                           preferred_element_type=F32)


def _split_bf16(x, parts):
    out = []
    for _ in range(parts - 1):
        hi = x.astype(BF16)
        out.append(hi)
        x = x - hi.astype(F32)
    out.append(x.astype(BF16))
    return out


def _group_sum(x, ones_blk):
    n = x.shape[0]
    r = jnp.dot(jnp.concatenate(_split_bf16(x, 2), axis=0), ones_blk, preferred_element_type=F32)
    return r[0:n] + r[n:2 * n]


def _iota(shape, dim):
    return lax.broadcasted_iota(jnp.int32, shape, dim)


def _group_ones(n, group):
    sh = group.bit_length() - 1
    return (jnp.right_shift(_iota((n, n), 0), sh) == jnp.right_shift(_iota((n, n), 1), sh)).astype(F32)


def _rope(x, c, sa, sb, half):
    n = x.shape[-1]
    return x * c + pltpu.roll(x, n - half, 1) * sa + pltpu.roll(x, half, 1) * sb


def _cparams(sem):
    return pltpu.CompilerParams(dimension_semantics=sem, vmem_limit_bytes=VMEM_LIMIT_BYTES)


def _mod_kernel(c_ref, w_ref, b_ref, o_ref):
    s = _silu(c_ref[...])
    o_ref[0] = _mm(s, w_ref[0]) + b_ref[0]


def _modulation(cc, w_mod, b_mod):
    depth, d, n = w_mod.shape
    r = cc.shape[0]
    tn = 1024
    return pl.pallas_call(
        _mod_kernel,
        grid=(depth, n // tn),
        in_specs=[pl.BlockSpec((r, d), lambda l, j: (0, 0)),
                  pl.BlockSpec((1, d, tn), lambda l, j: (l, 0, j)),
                  pl.BlockSpec((1, 1, tn), lambda l, j: (l, 0, j))],
        out_specs=pl.BlockSpec((1, r, tn), lambda l, j: (l, 0, j)),
        out_shape=jax.ShapeDtypeStruct((depth, r, n), F32),
        compiler_params=_cparams(("parallel", "parallel")),
        name="modulation",
    )(cc, w_mod, b_mod.reshape(depth, 1, n))


_P_CQ = (0, 384)
_P_CKV = (384, 640)
_P_KR = (640, 768)
_P_SQ = (768, 1280)
_P_SK = (1280, 1408)
_P_SV = (1408, 1536)
_P_QKV = (1536, 3072)
_P_Z = (3072, 3584)
_P_AB = (3584, 3712)
_P_GATE = (3712, 6784)
_P_WIDTH = 6784


def _proj_kernel(x_ref, mod_ref, g1_ref, win_ref, qn_ref, kvn_ref, wuq_ref, wukvk_ref, wukvv_ref,
                 mc_ref, ma_ref, mb_ref, sc_ref, sa_ref, sb_ref,
                 qm_ref, km_ref, vm_ref, sq_ref, sk_ref, sv_ref, qkv_ref, z_ref, ab_ref, gate_ref):
    x = x_ref[0]
    d = x.shape[-1]
    m = mod_ref[0]
    xb = (_rms(x, g1_ref[...]) * (1.0 + m[:, d:2 * d]) + m[:, 0:d]).astype(BF16)

    def proj(piece):
        return jnp.dot(xb, win_ref[:, piece[0]:piece[1]], preferred_element_type=F32)

    mc, ma, mb = mc_ref[...], ma_ref[...], mb_ref[...]
    sc, sa, sb = sc_ref[...], sa_ref[...], sb_ref[...]

    cq = _rms(proj(_P_CQ), qn_ref[...])
    ckv = _rms(proj(_P_CKV), kvn_ref[...])

    gate_ref[0] = proj(_P_GATE).astype(gate_ref.dtype)
    qkv_ref[0] = proj(_P_QKV)
    z_ref[0] = proj(_P_Z).astype(z_ref.dtype)
    ab_ref[0] = proj(_P_AB)
    sv_ref[0] = proj(_P_SV).astype(sv_ref.dtype)

    q = _mm(cq, wuq_ref[...]) * ((MLA_NOPE + MLA_ROPE) ** -0.5 * LOG2E)
    kn = _mm(ckv, wukvk_ref[...])
    vm_ref[0] = _mm(ckv, wukvv_ref[...]).astype(vm_ref.dtype)
    kr = _rope(proj(_P_KR), mc, ma, mb, MLA_ROPE // 4)
    for h in range(MLA_HEADS):
        sl = slice(h * LANES, (h + 1) * LANES)
        qm_ref[0, :, sl] = _rope(q[:, sl], mc, ma, mb, MLA_ROPE // 4).astype(qm_ref.dtype)
        km_ref[0, :, sl] = (kn[:, sl] + kr).astype(km_ref.dtype)

    sq = proj(_P_SQ) * (SWA_HEAD_DIM ** -0.5 * LOG2E)
    for g in range(SWA_HEADS * SWA_HEAD_DIM // LANES):
        sl = slice(g * LANES, (g + 1) * LANES)
        sq_ref[0, :, sl] = _rope(sq[:, sl], sc, sa, sb, SWA_HEAD_DIM // 4).astype(sq_ref.dtype)
    sk_ref[0] = _rope(proj(_P_SK), sc, sa, sb, SWA_HEAD_DIM // 4).astype(sk_ref.dtype)


def _proj(xs, mod_l, g1, win, qn, kvn, wuq, wukvk, wukvv, rope_tabs, nctt, ctx_row):
    b, t, d = xs.shape
    tm = TOKEN_TILE
    nt = t // tm

    def tok(w):
        return pl.BlockSpec((1, tm, w), lambda i, j: (i, j, 0))

    def full(a):
        return pl.BlockSpec(a.shape, lambda i, j: (0,) * a.ndim)

    tab = pl.BlockSpec((tm, LANES), lambda i, j: (j, 0))
    mod_spec = pl.BlockSpec((1, 1, mod_l.shape[-1]),
                            lambda i, j: (jnp.where(j < nctt, ctx_row, i), 0, 0))
    widths = (1024, 1024, 512, 512, 128, 128, 1536, 512, 128, 3072)
    dtypes = (BF16, BF16, BF16, BF16, BF16, BF16, F32, BF16, F32, BF16)
    return pl.pallas_call(
        _proj_kernel,
        grid=(b, nt),
        in_specs=[tok(d), mod_spec, full(g1), full(win), full(qn), full(kvn), full(wuq),
                  full(wukvk), full(wukvv)] + [tab] * 6,
        out_specs=[tok(w) for w in widths],
        out_shape=[jax.ShapeDtypeStruct((b, t, w), dt) for w, dt in zip(widths, dtypes)],
        compiler_params=_cparams(("parallel", "parallel")),
        name="proj",
    )(xs, mod_l, g1, win, qn, kvn, wuq, wukvk, wukvv, *rope_tabs)


def _mla_kernel(q_ref, k_ref, v_ref, o_ref, *, n_ctx, q_tile, skip_ctx):
    n = pl.program_id(2) + (1 if skip_ctx else 0)

    def run(row0, nq, nk):
        low = _iota((nq, LANES), 1) < MLA_V
        v = v_ref[0, 0:nk, :]
        v_low = _iota((nk, LANES), 1) < MLA_V
        one = jnp.ones_like(v)
        ss = [_mm_nt(q_ref[0, pl.ds(row0, nq), hh * LANES:(hh + 1) * LANES],
                     k_ref[0, 0:nk, hh * LANES:(hh + 1) * LANES]) for hh in range(2)]
        ps = [jnp.exp2(s - jnp.max(s, axis=-1, keepdims=True)) for s in ss]
        outs = []
        for hh in range(2):
            acc = _mm(ps[hh], jnp.where(v_low if hh == 0 else jnp.logical_not(v_low), v, one))
            den = acc[:, MLA_V:MLA_V + 1] if hh == 0 else acc[:, 0:1]
            outs.append(acc / den)
        o_ref[0, pl.ds(row0, nq), :] = jnp.where(low, outs[0], outs[1]).astype(o_ref.dtype)

    if not skip_ctx:
        @pl.when(n == 0)
        def _():
            run(0, n_ctx, n_ctx)

    @pl.when(n > 0)
    def _():
        run(pl.multiple_of(n_ctx + (n - 1) * q_tile, LANES), q_tile, k_ref.shape[1])


def _mla(qm, km, vm, n_ctx, skip_ctx):
    b, t, _ = qm.shape
    q_tile = MLA_Q_TILE
    assert (t - n_ctx) % q_tile == 0
    steps = (t - n_ctx) // q_tile + (0 if skip_ctx else 1)
    return pl.pallas_call(
        functools.partial(_mla_kernel, n_ctx=n_ctx, q_tile=q_tile, skip_ctx=skip_ctx),
        grid=(b, MLA_HEADS // 2, steps),
        in_specs=[pl.BlockSpec((1, t, 2 * LANES), lambda i, j, n: (i, 0, j)),
                  pl.BlockSpec((1, t, 2 * LANES), lambda i, j, n: (i, 0, j)),
                  pl.BlockSpec((1, t, LANES), lambda i, j, n: (i, 0, j))],
        out_specs=pl.BlockSpec((1, t, LANES), lambda i, j, n: (i, 0, j)),
        out_shape=jax.ShapeDtypeStruct((b, t, MLA_HEADS * MLA_V), BF16),
        compiler_params=_cparams(("parallel", "parallel", "arbitrary")),
        name="mla",
    )(qm, km, vm)


def _swa_kernel(sink_ref, q_ref, k_ref, v_ref, o_ref, *, n_ctx, q_off):
    i = pl.program_id(1) + q_off
    blk = q_ref.shape[1]
    t = k_ref.shape[1]
    n_lat = t - n_ctx
    nctt = n_ctx // blk
    low = _iota((blk, LANES), 1) < SWA_HEAD_DIM
    n_slab = SWA_HEADS * SWA_HEAD_DIM // LANES
    half = n_slab * blk

    parts, sinks = [], []
    for hh in range(2):
        for g in range(n_slab):
            qs = q_ref[0, :, g * LANES:(g + 1) * LANES]
            parts.append(jnp.where(low if hh == 0 else jnp.logical_not(low), qs, jnp.zeros_like(qs)))
            head = hh * n_slab + g
            sinks.append(jnp.broadcast_to(sink_ref[head:head + 1, 0:1] * LOG2E, (blk, 1)))
    q_all = jnp.concatenate(parts, axis=0)
    sink = jnp.concatenate(sinks, axis=0)

    def with_ones(v, hh):
        v_low = _iota(v.shape, 1) < SWA_HEAD_DIM
        return jnp.where(v_low if hh == 0 else jnp.logical_not(v_low), v, jnp.ones_like(v))

    def attend(loc):
        kc = k_ref[0, 0:n_ctx, :]
        vc = v_ref[0, 0:n_ctx, :]
        s_c = _mm_nt(q_all, kc)
        mx = jnp.maximum(jnp.max(s_c, axis=-1, keepdims=True), sink)
        if loc is not None:
            kl, vl, valid = loc
            s_l = jnp.where(valid, _mm_nt(q_all, kl), NEG_INF)
            mx = jnp.maximum(mx, jnp.max(s_l, axis=-1, keepdims=True))
        p_c = jnp.exp2(s_c - mx).astype(BF16)
        p_sink = jnp.exp2(sink - mx)
        if loc is not None:
            p_l = jnp.exp2(s_l - mx).astype(BF16)
        outs = []
        for hh in range(2):
            rows = slice(hh * half, (hh + 1) * half)
            acc = _mm(p_c[rows], with_ones(vc, hh))
            if loc is not None:
                acc = acc + _mm(p_l[rows], with_ones(vl, hh))
            den = (acc[:, SWA_HEAD_DIM:SWA_HEAD_DIM + 1] if hh == 0 else acc[:, 0:1]) + p_sink[rows]
            outs.append(acc / den)
        for g in range(n_slab):
            rows = slice(g * blk, (g + 1) * blk)
            o_ref[0, :, g * LANES:(g + 1) * LANES] = jnp.where(low, outs[0][rows], outs[1][rows]).astype(o_ref.dtype)

    if q_off < nctt:
        @pl.when(i < nctt)
        def _():
            attend(None)

    @pl.when(i >= nctt)
    def _():
        n = i - nctt
        span = 3 * blk
        start = jnp.clip((n - 1) * blk, 0, n_lat - span)
        row0 = pl.multiple_of(n_ctx + start, blk)
        kl = k_ref[0, pl.ds(row0, span), :]
        vl = v_ref[0, pl.ds(row0, span), :]
        kpos = start + _iota((blk, span), 1)
        qpos = n * blk + _iota((blk, span), 0)
        valid = jnp.abs(kpos - qpos) <= WINDOW
        attend((kl, vl, jnp.concatenate([valid] * (2 * n_slab), axis=0)))


def _swa(sink, sq, sk, sv, n_ctx, q_off):
    b, t, w = sq.shape
    blk = SWA_TILE
    nt = t // blk - q_off
    return pl.pallas_call(
        functools.partial(_swa_kernel, n_ctx=n_ctx, q_off=q_off),
        grid=(b, nt),
        in_specs=[pl.BlockSpec(sink.shape, lambda i, n: (0, 0)),
                  pl.BlockSpec((1, blk, w), lambda i, n: (i, n + q_off, 0)),
                  pl.BlockSpec((1, t, LANES), lambda i, n: (i, 0, 0)),
                  pl.BlockSpec((1, t, LANES), lambda i, n: (i, 0, 0))],
        out_specs=pl.BlockSpec((1, blk, w), lambda i, n: (i, n + q_off, 0)),
        out_shape=jax.ShapeDtypeStruct((b, t, w), BF16),
        compiler_params=_cparams(("parallel", "parallel")),
        name="swa",
    )(sink, sq, sk, sv)


def _block_diag(x, bd_mask):
    return jnp.where(bd_mask, jnp.concatenate([x] * PACK, axis=0), 0.0)


def _gdn_prep_kernel(qkv_ref, prev_ref, next_ref, ab_ref, cw_ref, gp_ref,
                     a_ref, rv_ref, rk_ref, qk_ref, qe_ref, ke_ref, dl_ref, *, seg_starts, seg_ends):
    ti = pl.program_id(1)
    tm = qkv_ref.shape[1]
    hw = GDN_HEADS * GDN_DK
    x = qkv_ref[0]
    row = _iota((tm, 1), 0)
    is_start = functools.reduce(jnp.logical_or, [ti == s for s in seg_starts])
    is_end = functools.reduce(jnp.logical_or, [ti == s for s in seg_ends])
    prev_row = jnp.where(is_start, 0.0, prev_ref[0, SUBLANES - 1:SUBLANES, :])
    next_row = jnp.where(is_end, 0.0, next_ref[0, 0:1, :])
    x_prev = jnp.where(row == 0, prev_row, pltpu.roll(x, 1, 0))
    x_next = jnp.where(row == tm - 1, next_row, pltpu.roll(x, tm - 1, 0))
    y = _silu(x_prev * cw_ref[0:1, :] + x * cw_ref[1:2, :] + x_next * cw_ref[2:3, :])

    ones_blk = _group_ones(LANES, GDN_DK).astype(BF16)

    def l2(slab):
        return slab * lax.rsqrt(_group_sum(slab * slab, ones_blk) + EPS)

    q = jnp.concatenate([l2(y[:, s * LANES:(s + 1) * LANES]) for s in range(hw // LANES)], axis=1)
    q = q * (GDN_DK ** -0.5)
    k = jnp.concatenate([l2(y[:, hw + s * LANES:hw + (s + 1) * LANES]) for s in range(hw // LANES)], axis=1)
    v = y[:, 2 * hw:3 * hw]

    ab = ab_ref[0]
    g_all = -jnp.exp(gp_ref[0:1, :]) * _softplus(ab + gp_ref[1:2, :])
    beta_all = _sigmoid(ab)
    ri = _iota((tm, tm), 0)
    ci = _iota((tm, tm), 1)
    same = jnp.right_shift(ri, 6) == jnp.right_shift(ci, 6)
    g_parts = jnp.concatenate(_split_bf16(g_all, 3), axis=1)

    def cumsum(tri):
        r = jnp.dot(jnp.logical_and(same, tri).astype(BF16), g_parts, preferred_element_type=F32)
        return r[:, 0:LANES] + r[:, LANES:2 * LANES] + r[:, 2 * LANES:3 * LANES]

    gam_all = jnp.where(_iota((tm, LANES), 1) < GDN_HEADS, cumsum(ci <= ri), cumsum(ci >= ri))
    col_head = jnp.right_shift(_iota((LANES, 2 * hw), 1), GDN_DK.bit_length() - 1)
    src = _iota((LANES, 2 * hw), 0)

    def expand(x, parts, sel):
        r = jnp.dot(jnp.concatenate(_split_bf16(x, parts), axis=0), sel.astype(BF16), preferred_element_type=F32)
        return functools.reduce(lambda a, b: a + b, [r[p * tm:(p + 1) * tm] for p in range(parts)])

    gam_exp = expand(gam_all, 3, src == col_head)
    b_exp = expand(beta_all, 2, src == col_head + 2 * GDN_HEADS)
    gam_f, gam_b = gam_exp[:, 0:hw], gam_exp[:, hw:2 * hw]

    c = GDN_CHUNK
    r64 = _iota((c, PACK_W), 0)
    c64 = jnp.bitwise_and(_iota((c, PACK_W), 1), c - 1)
    eye = (r64 == c64).astype(F32)
    bd_mask = jnp.right_shift(_iota((PACK_W, PACK_W), 0), 6) == jnp.right_shift(_iota((PACK_W, PACK_W), 1), 6)

    for ch in range(tm // c):
        rows = slice(ch * c, (ch + 1) * c)
        for gi in range(hw // PACK_W):
            lanes = slice(gi * PACK_W, (gi + 1) * PACK_W)
            kc, qc, vc = k[rows, lanes], q[rows, lanes], v[rows, lanes]
            kbd = _block_diag(kc, bd_mask)
            kq = _mm_nt(jnp.concatenate([kc, qc], axis=0), kbd)
            kk, qkm = kq[0:c], kq[c:2 * c]
            for d in range(2):
                gam = (gam_f if d == 0 else gam_b)[rows, lanes]
                beta = b_exp[rows, d * hw + gi * PACK_W:d * hw + (gi + 1) * PACK_W]
                gam_row = jnp.sum(gam * eye, axis=0, keepdims=True)
                decay = jnp.exp(jnp.minimum(gam - gam_row, 0.0))
                strict = (r64 > c64) if d == 0 else (r64 < c64)
                incl = (r64 >= c64) if d == 0 else (r64 <= c64)
                a_mat = jnp.where(strict, beta * kk * decay, 0.0)
                for pp in range(PACK_W // LANES):
                    a_ref[d, :, ch * (hw // LANES) + gi * (PACK_W // LANES) + pp, :] = \
                        a_mat[:, pp * LANES:(pp + 1) * LANES]
                e_gam = jnp.exp(gam)
                g_last = gam[c - 1:c, :] if d == 0 else gam[0:1, :]
                out = slice(d * hw + gi * PACK_W, d * hw + (gi + 1) * PACK_W)
                rv_ref[0, rows, out] = vc * beta
                rk_ref[0, rows, out] = kc * beta * e_gam
                qk_ref[0, rows, out] = jnp.where(incl, qkm * decay, 0.0).astype(qk_ref.dtype)
                qe_ref[0, rows, out] = (qc * e_gam).astype(qe_ref.dtype)
                ke_ref[0, rows, out] = (kc * jnp.exp(g_last - gam)).astype(ke_ref.dtype)
                dl_ref[0, ch, :, out] = jnp.exp(g_last)


def _gdn_prep(qkv, ab, cw, gp, seg_starts, seg_ends):
    b, t, w = qkv.shape
    tm = TOKEN_TILE
    nt = t // tm
    hb = tm // SUBLANES
    nh = t // SUBLANES
    wide = 2 * GDN_HEADS * GDN_DK
    c = GDN_CHUNK
    cpt = tm // c
    spt = cpt * (GDN_HEADS * GDN_DK // LANES)
    tok = lambda width: pl.BlockSpec((1, tm, width), lambda i, j: (i, j, 0))
    out_shape = [jax.ShapeDtypeStruct((2, c, b * nt * spt, LANES), F32)] + \
                [jax.ShapeDtypeStruct((b, t, wide), F32)] * 2 + \
                [jax.ShapeDtypeStruct((b, t, wide), BF16)] * 3 + \
                [jax.ShapeDtypeStruct((b, t // c, 1, wide), F32)]
    return pl.pallas_call(
        functools.partial(_gdn_prep_kernel, seg_starts=seg_starts, seg_ends=seg_ends),
        grid=(b, nt),
        in_specs=[tok(w),
                  pl.BlockSpec((1, SUBLANES, w), lambda i, j: (i, jnp.maximum(j * hb - 1, 0), 0)),
                  pl.BlockSpec((1, SUBLANES, w), lambda i, j: (i, jnp.minimum((j + 1) * hb, nh - 1), 0)),
                  tok(LANES),
                  pl.BlockSpec(cw.shape, lambda i, j: (0, 0)),
                  pl.BlockSpec(gp.shape, lambda i, j: (0, 0))],
        out_specs=[pl.BlockSpec((2, c, spt, LANES), lambda i, j: (0, 0, i * nt + j, 0))] +
                  [tok(wide)] * 5 + [pl.BlockSpec((1, cpt, 1, wide), lambda i, j: (i, j, 0, 0))],
        out_shape=out_shape,
        compiler_params=_cparams(("parallel", "parallel")),
        name="gdn_prep",
    )(qkv, qkv, qkv, ab, cw, gp)


def _gdn_solve_kernel(a_ref, o_ref, at_ref, x_ref, acc_ref):
    c = GDN_CHUNK
    sub = SUBLANES
    nb = c // sub
    for i in range(c):
        at_ref[i * LANES:(i + 1) * LANES, :] = a_ref[0, i].T
    x_ref[...] = jnp.zeros_like(x_ref)
    row_iota = _iota((LANES, LANES), 0)

    def solve(upper):
        def row(step, carry):
            i = (c - 1 - step) if upper else step
            base = pl.multiple_of(i * LANES, LANES)
            acc_ref[...] = (jnp.bitwise_and(row_iota, c - 1) == i).astype(F32)
            for kb in range(nb):
                active = (kb * sub + sub - 1 > i) if upper else (kb * sub < i)

                @pl.when(active)
                def _():
                    for h2 in range(LANES // c):
                        coefs = [at_ref[pl.ds(base + h2 * c + kb * sub + kk, 1), :] for kk in range(sub)]
                        for jb in (range(kb, nb) if upper else range(kb + 1)):
                            dst = slice(h2 * c + jb * sub, h2 * c + (jb + 1) * sub)
                            acc = acc_ref[dst, :]
                            for kk in range(sub):
                                lo = (kb * sub + kk) * LANES + h2 * c + jb * sub
                                acc = acc - coefs[kk] * x_ref[lo:lo + sub, :]
                            acc_ref[dst, :] = acc
            x_ref[pl.ds(base, LANES), :] = acc_ref[...]
            return carry

        lax.fori_loop(0, c, row, 0)

    @pl.when(pl.program_id(0) == 0)
    def _():
        solve(False)

    @pl.when(pl.program_id(0) == 1)
    def _():
        solve(True)

    for i in range(c):
        o_ref[0, i] = x_ref[i * LANES:(i + 1) * LANES, :].T


def _gdn_solve(a):
    _, c, n_slab, _ = a.shape
    assert n_slab % LANES == 0
    blk = pl.BlockSpec((1, c, LANES, LANES), lambda d, g: (d, 0, g, 0))
    return pl.pallas_call(
        _gdn_solve_kernel,
        grid=(2, n_slab // LANES),
        in_specs=[blk],
        out_specs=blk,
        out_shape=jax.ShapeDtypeStruct(a.shape, F32),
        scratch_shapes=[pltpu.VMEM((c * LANES, LANES), F32), pltpu.VMEM((c * LANES, LANES), F32),
                        pltpu.VMEM((LANES, LANES), F32)],
        compiler_params=_cparams(("parallel", "parallel")),
        name="gdn_solve",
    )(a)


def _gdn_apply_kernel(t_ref, rv_ref, rk_ref, u_ref, w_ref):
    tm = rv_ref.shape[1]
    c = GDN_CHUNK
    hw = GDN_HEADS * GDN_DK
    ppg = PACK_W // LANES
    bd_mask = jnp.right_shift(_iota((PACK_W, PACK_W), 0), 6) == jnp.right_shift(_iota((PACK_W, PACK_W), 1), 6)
    for ch in range(tm // c):
        rows = slice(ch * c, (ch + 1) * c)
        for gi in range(hw // PACK_W):
            for d in range(2):
                out = slice(d * hw + gi * PACK_W, d * hw + (gi + 1) * PACK_W)
                tinv = jnp.concatenate([t_ref[d, :, ch * (hw // LANES) + gi * ppg + pp, :] for pp in range(ppg)],
                                       axis=1)
                t_hi, t_lo = _split_bf16(tinv, 2)
                v_hi, v_lo = _split_bf16(rv_ref[0, rows, out], 2)
                top = jnp.dot(jnp.concatenate([t_hi, t_lo], axis=0), _block_diag(v_hi, bd_mask),
                              preferred_element_type=F32)
                u_ref[0, rows, out] = (top[0:c] + top[c:2 * c]
                                       + jnp.dot(t_hi, _block_diag(v_lo, bd_mask), preferred_element_type=F32))
                w_ref[0, rows, out] = jnp.dot(t_hi, _block_diag(rk_ref[0, rows, out].astype(BF16), bd_mask),
                                              preferred_element_type=F32).astype(w_ref.dtype)


def _gdn_apply(tinv, rv, rk):
    b, t, wide = rv.shape
    tm = TOKEN_TILE
    c = GDN_CHUNK
    nt = t // tm
    spt = (tm // c) * (GDN_HEADS * GDN_DK // LANES)
    tok = pl.BlockSpec((1, tm, wide), lambda i, j: (i, j, 0))
    return pl.pallas_call(
        _gdn_apply_kernel,
        grid=(b, nt),
        in_specs=[pl.BlockSpec((2, c, spt, LANES), lambda i, j: (0, 0, i * nt + j, 0)), tok, tok],
        out_specs=[tok, tok],
        out_shape=[jax.ShapeDtypeStruct((b, t, wide), F32), jax.ShapeDtypeStruct((b, t, wide), BF16)],
        compiler_params=_cparams(("parallel", "parallel")),
        name="gdn_apply",
    )(tinv, rv, rk)


def _gdn_scan_kernel(uf, wf, qkf, qef, kef, dlf, ub, wb, qkb, qeb, keb, dlb, of_ref, ob_ref, s_ref):
    c = GDN_CHUNK
    cpt = uf.shape[1] // c
    ng = uf.shape[2] // PACK_W
    bd_mask = jnp.right_shift(_iota((PACK_W, PACK_W), 0), 6) == jnp.right_shift(_iota((PACK_W, PACK_W), 1), 6)

    @pl.when(pl.program_id(1) == 0)
    def _():
        s_ref[...] = jnp.zeros_like(s_ref)

    chains = ((uf, wf, qkf, qef, kef, dlf, of_ref), (ub, wb, qkb, qeb, keb, dlb, ob_ref))
    for j in range(cpt):
        where = []
        for d in range(2):
            ch = j if d == 0 else cpt - 1 - j
            for gi in range(ng):
                where.append((d, gi, ch, slice(ch * c, (ch + 1) * c), slice(gi * PACK_W, (gi + 1) * PACK_W)))
        states = [s_ref[d * ng + gi] for d, gi, _, _, _ in where]
        wqs = [_mm(jnp.concatenate([chains[d][1][0, rows, lanes], chains[d][3][0, rows, lanes]], axis=0), st)
               for (d, gi, ch, rows, lanes), st in zip(where, states)]
        v_news = [chains[d][0][0, rows, lanes] - wq[0:c] for (d, gi, ch, rows, lanes), wq in zip(where, wqs)]
        kvs = [_mm_tn(chains[d][4][0, rows, lanes], v_new) for (d, gi, ch, rows, lanes), v_new in zip(where, v_news)]
        for (d, gi, ch, rows, lanes), st, kv in zip(where, states, kvs):
            s_ref[d * ng + gi] = st * chains[d][5][0, ch, :, lanes] + jnp.where(bd_mask, kv, 0.0)
        for (d, gi, ch, rows, lanes), wq, v_new in zip(where, wqs, v_news):
            chains[d][6][0, rows, lanes] = wq[c:2 * c] + _mm(chains[d][2][0, rows, lanes],
                                                             _block_diag(v_new, bd_mask))


def _gdn_scan(u, w, qk, qe, ke, dl, n_ctx):
    b, t, wide = u.shape
    tm = TOKEN_TILE
    nt = t // tm
    nctt = n_ctx // tm
    hw = wide // 2
    cpt = tm // GDN_CHUNK

    def mirror(g):
        return jnp.where(g < nctt, nctt - 1 - g, nt - 1 - (g - nctt))

    fwd = pl.BlockSpec((1, tm, hw), lambda i, g: (i, g, 0))
    bwd = pl.BlockSpec((1, tm, hw), lambda i, g: (i, mirror(g), 1))
    dfwd = pl.BlockSpec((1, cpt, 1, hw), lambda i, g: (i, g, 0, 0))
    dbwd = pl.BlockSpec((1, cpt, 1, hw), lambda i, g: (i, mirror(g), 0, 1))
    out_sds = jax.ShapeDtypeStruct((b, t, hw), F32)
    return pl.pallas_call(
        _gdn_scan_kernel,
        grid=(b, nt),
        in_specs=[fwd] * 5 + [dfwd] + [bwd] * 5 + [dbwd],
        out_specs=[pl.BlockSpec((1, tm, hw), lambda i, g: (i, g, 0)),
                   pl.BlockSpec((1, tm, hw), lambda i, g: (i, mirror(g), 0))],
        out_shape=[out_sds, out_sds],
        scratch_shapes=[pltpu.VMEM((2 * hw // PACK_W, PACK_W, PACK_W), F32)],
        compiler_params=_cparams(("parallel", "arbitrary")),
        name="gdn_scan",
    )(u, w, qk, qe, ke, dl, u, w, qk, qe, ke, dl)


def _merge_kernel(x_ref, mod_ref, ya_ref, yb_ref, of_ref, ob_ref, z_ref, gate_ref, on_ref,
                  wpa_ref, wpb_ref, wpc_ref, wout_ref, o_ref):
    x = x_ref[0]
    d = x.shape[-1]
    m = mod_ref[0]
    oc = of_ref[0] + ob_ref[0]
    ones_blk = _group_ones(LANES, GDN_DV).astype(BF16)
    ms = jnp.concatenate(
        [_group_sum(oc[:, s * LANES:(s + 1) * LANES] ** 2, ones_blk) for s in range(oc.shape[-1] // LANES)],
        axis=1) * (1.0 / GDN_DV)
    yc = oc * lax.rsqrt(ms + EPS) * on_ref[...] * _silu(z_ref[0].astype(F32))
    gate = gate_ref[0].astype(F32)
    mix = (_sigmoid(gate[:, 0:d]) * _mm(ya_ref[0], wpa_ref[...])
           + _sigmoid(gate[:, d:2 * d]) * _mm(yb_ref[0], wpb_ref[...])
           + _sigmoid(gate[:, 2 * d:3 * d]) * _mm(yc, wpc_ref[...]))
    o_ref[0] = x + m[:, 2 * d:3 * d] * _mm(mix, wout_ref[...])


def _merge(xs, mod_l, ya, yb, oc_f, oc_b, z, gate, onorm, wpa, wpb, wpc, wout, nctt, ctx_row, t_off):
    b, t, d = xs.shape
    tm = TOKEN_TILE
    nt = t // tm - t_off
    tok = lambda width: pl.BlockSpec((1, tm, width), lambda i, j: (i, j + t_off, 0))
    full = lambda a: pl.BlockSpec(a.shape, lambda i, j: (0,) * a.ndim)
    mod_spec = pl.BlockSpec((1, 1, mod_l.shape[-1]),
                            lambda i, j: (jnp.where(j + t_off < nctt, ctx_row, i), 0, 0))
    return pl.pallas_call(
        _merge_kernel,
        grid=(b, nt),
        in_specs=[tok(d), mod_spec, tok(512), tok(512), tok(512), tok(512), tok(512), tok(3 * d),
                  full(onorm), full(wpa), full(wpb), full(wpc), full(wout)],
        out_specs=pl.BlockSpec((1, tm, d), lambda i, j: (i, j, 0)),
        out_shape=jax.ShapeDtypeStruct((b, nt * tm, d), F32),
        compiler_params=_cparams(("parallel", "parallel")),
        name="merge",
    )(xs, mod_l, ya, yb, oc_f, oc_b, z, gate, onorm, wpa, wpb, wpc, wout)


def _ffn_kernel(x_ref, prev_ref, next_ref, mod_ref, g2_ref, wup_ref, cw_ref, wdn_ref, nf_ref, o_ref,
                *, seg_starts, seg_ends, final):
    ti = pl.program_id(1)
    x = x_ref[0]
    tm, d = x.shape
    m = mod_ref[0]
    xall = jnp.concatenate([prev_ref[0], x, next_ref[0]], axis=0)
    xa = _rms(xall, g2_ref[...]) * (1.0 + m[:, 4 * d:5 * d]) + m[:, 3 * d:4 * d]
    n_all = tm + 2 * SUBLANES
    row = _iota((n_all, 1), 0)
    is_start = functools.reduce(jnp.logical_or, [ti == s for s in seg_starts])
    is_end = functools.reduce(jnp.logical_or, [ti == s for s in seg_ends])
    drop = jnp.logical_or(jnp.logical_and(row == SUBLANES - 1, is_start),
                          jnp.logical_and(row == SUBLANES + tm, is_end))
    xa = jnp.where(drop, 0.0, xa).astype(BF16)

    def conv(h, cols):
        hp = pltpu.roll(h, 1, 0)[SUBLANES:SUBLANES + tm]
        hn = pltpu.roll(h, n_all - 1, 0)[SUBLANES:SUBLANES + tm]
        return (hp * cw_ref[0:1, cols] + h[SUBLANES:SUBLANES + tm] * cw_ref[1:2, cols]
                + hn * cw_ref[2:3, cols])

    cwid = FFN_DIM // FFN_SPLIT
    acc = jnp.zeros((tm, d), F32)
    for j in range(FFN_SPLIT):
        ca = slice(j * cwid, (j + 1) * cwid)
        cb = slice(FFN_DIM + j * cwid, FFN_DIM + (j + 1) * cwid)
        ha = conv(jnp.dot(xa, wup_ref[:, ca], preferred_element_type=F32), ca)
        hb = conv(jnp.dot(xa, wup_ref[:, cb], preferred_element_type=F32), cb)
        acc = acc + _mm(_silu(ha) * hb, wdn_ref[ca, :])
    y = x + m[:, 5 * d:6 * d] * acc
    if final:
        y = _rms(y, nf_ref[...])
    o_ref[0] = y


def _ffn(x1, mod_l, g2, wup, cw, wdn, nf, nctt_mod, ctx_row, seg_starts, seg_ends, final):
    b, t, d = x1.shape
    tm = TOKEN_TILE
    nt = t // tm
    hb = tm // SUBLANES
    nh = t // SUBLANES
    full = lambda a: pl.BlockSpec(a.shape, lambda i, j: (0,) * a.ndim)
    mod_spec = pl.BlockSpec((1, 1, mod_l.shape[-1]),
                            lambda i, j: (jnp.where(j < nctt_mod, ctx_row, i), 0, 0))
    return pl.pallas_call(
        functools.partial(_ffn_kernel, seg_starts=seg_starts, seg_ends=seg_ends, final=final),
        grid=(b, nt),
        in_specs=[pl.BlockSpec((1, tm, d), lambda i, j: (i, j, 0)),
                  pl.BlockSpec((1, SUBLANES, d), lambda i, j: (i, jnp.maximum(j * hb - 1, 0), 0)),
                  pl.BlockSpec((1, SUBLANES, d), lambda i, j: (i, jnp.minimum((j + 1) * hb, nh - 1), 0)),
                  mod_spec, full(g2), full(wup), full(cw), full(wdn), full(nf)],
        out_specs=pl.BlockSpec((1, tm, d), lambda i, j: (i, j, 0)),
        out_shape=jax.ShapeDtypeStruct((b, t, d), F32),
        compiler_params=_cparams(("parallel", "parallel")),
        name="ffn",
    )(x1, x1, x1, mod_l, g2, wup, cw, wdn, nf)


def _cols(w, segments):
    parts = [jnp.zeros(w.shape[:-1] + (seg,), w.dtype) if isinstance(seg, int) else w[..., seg[0]:seg[1]]
             for seg in segments]
    return jnp.concatenate(parts, axis=-1)


def _win_segments(d):
    widths = [MLA_Q_RANK, MLA_KV_RANK, MLA_ROPE, SWA_HEADS * SWA_HEAD_DIM, SWA_KV_HEADS * SWA_HEAD_DIM,
              SWA_KV_HEADS * SWA_HEAD_DIM, GDN_HEADS * (2 * GDN_DK + GDN_DV), GDN_HEADS * GDN_DV,
              2 * GDN_HEADS, 2 * GDN_HEADS, 3 * d]
    off = np.cumsum([0] + widths)
    cq, ckv, kr, sq, sk, sv, qkv, z, a, b, gate = [(int(off[i]), int(off[i + 1])) for i in range(11)]
    sq_perm = [(sq[0] + h * SWA_HEAD_DIM, sq[0] + (h + 1) * SWA_HEAD_DIM) for h in _swa_head_order()]
    segs = [cq, ckv, MLA_NOPE, kr, LANES - MLA_NOPE - MLA_ROPE] + sq_perm + \
           [sk, sv, qkv, z, a, b, LANES - 4 * GDN_HEADS, gate]
    return segs


def _swa_head_order():
    group = SWA_HEADS // SWA_KV_HEADS
    return [kv * group + g for g in range(group) for kv in range(SWA_KV_HEADS)]


def _rope_tables(n_ctx, n_lat):
    t = np.arange(n_lat)
    pos = np.stack([t // GRID_W, t % GRID_W]).astype(np.float32)

    def build(head_w, rope_lo, rope_w):
        da = rope_w // 2
        half = da // 2
        inv = ROPE_THETA ** (-np.arange(half, dtype=np.float32) / half)
        c = np.ones((n_lat, LANES), np.float32)
        sa = np.zeros((n_lat, LANES), np.float32)
        sb = np.zeros((n_lat, LANES), np.float32)
        for lane in range(LANES):
            dd = lane % head_w - rope_lo
            if dd < 0 or dd >= rope_w:
                continue
            axis, r = dd // da, dd % da
            ang = pos[axis] * inv[r % half]
            c[:, lane] = np.cos(ang)
            if r < half:
                sa[:, lane] = -np.sin(ang)
            else:
                sb[:, lane] = np.sin(ang)
        ctx = [np.ones((n_ctx, LANES), np.float32), np.zeros((n_ctx, LANES), np.float32),
               np.zeros((n_ctx, LANES), np.float32)]
        return [jnp.asarray(np.concatenate([cx, lt])) for cx, lt in zip(ctx, (c, sa, sb))]

    return build(LANES, MLA_NOPE, MLA_ROPE) + build(SWA_HEAD_DIM, 0, SWA_HEAD_DIM)


def kernel(x, c, ctx, c_ctx, w_mod, b_mod, norm1, norm2, w_in, mla_q_norm, mla_kv_norm, w_uq, w_ukv,
           swa_sink, gdn_conv, gdn_a_log, gdn_dt_bias, gdn_norm, w_branch_a, w_branch_b, w_branch_c,
           w_out, ffn_up, ffn_conv, ffn_down, norm_f):
    b, s, d = x.shape
    n_ctx = ctx.shape[1]
    depth = w_mod.shape[0]
    tm = TOKEN_TILE
    assert n_ctx % tm == 0 and s % tm == 0 and s >= 3 * SWA_TILE
    nctt = n_ctx // tm
    ntt = (n_ctx + s) // tm

    rows = -(-(b + 1) // SUBLANES) * SUBLANES
    cc = jnp.zeros((rows, d), F32).at[0:b].set(c).at[b].set(c_ctx)
    mod = _modulation(cc, w_mod, b_mod).reshape(depth, rows, 1, 6 * d)

    hd_q = MLA_NOPE + MLA_ROPE
    hd_kv = MLA_NOPE + MLA_V
    uq_segs = [s for h in range(MLA_HEADS) for s in ((h * hd_q, (h + 1) * hd_q), LANES - hd_q)]
    uk_segs = [s for h in range(MLA_HEADS) for s in ((h * hd_kv, h * hd_kv + MLA_NOPE), LANES - MLA_NOPE)]
    uv_segs = [(h * hd_kv + MLA_NOPE, (h + 1) * hd_kv) for h in range(MLA_HEADS)]
    win_segs = _win_segments(d)

    def layer_weights(l):
        wpb = jnp.concatenate([w_branch_b[l, h * SWA_HEAD_DIM:(h + 1) * SWA_HEAD_DIM] for h in _swa_head_order()],
                              axis=0)
        gpar = jnp.zeros((SUBLANES, LANES), F32)
        gpar = gpar.at[0, 0:2 * GDN_HEADS].set(gdn_a_log[l].reshape(-1))
        gpar = gpar.at[1, 0:2 * GDN_HEADS].set(gdn_dt_bias[l].reshape(-1))
        pad_rows = lambda a: jnp.pad(a, ((0, SUBLANES - a.shape[0]), (0, 0)))
        return dict(
            win=_cols(w_in[l], win_segs).astype(BF16),
            wuq=_cols(w_uq[l], uq_segs).astype(BF16),
            wukvk=_cols(w_ukv[l], uk_segs).astype(BF16),
            wukvv=_cols(w_ukv[l], uv_segs).astype(BF16),
            wpa=w_branch_a[l].astype(BF16), wpb=wpb.astype(BF16), wpc=w_branch_c[l].astype(BF16),
            wout=w_out[l].astype(BF16), wup=ffn_up[l].astype(BF16), wdn=ffn_down[l].astype(BF16),
            sink=jnp.broadcast_to(swa_sink[l][:, None], (SWA_HEADS, LANES)),
            onorm=jnp.tile(gdn_norm[l], GDN_HEADS).reshape(1, -1),
            gconv=pad_rows(gdn_conv[l]), fconv=pad_rows(ffn_conv[l]), gpar=gpar)

    rope_tabs = _rope_tables(n_ctx, s)

    xs = jnp.concatenate([ctx, x], axis=1)
    row2 = lambda a: a.reshape(1, -1)
    for l in range(depth):
        last = l == depth - 1
        t_off = nctt if last else 0
        p = layer_weights(l)
        qm, km, vm, sq, sk, sv, qkv, z, ab, gate = _proj(
            xs, mod[l], row2(norm1[l]), p["win"], row2(mla_q_norm[l]), row2(mla_kv_norm[l]),
            p["wuq"], p["wukvk"], p["wukvv"], rope_tabs, nctt, b)
        ya = _mla(qm, km, vm, n_ctx, last)
        yb = _swa(p["sink"], sq, sk, sv, n_ctx, t_off * (tm // SWA_TILE))
        a_mat, rv, rk, qk, qe, ke, dl = _gdn_prep(qkv, ab, p["gconv"], p["gpar"], (0, nctt), (nctt - 1, ntt - 1))
        u, w = _gdn_apply(_gdn_solve(a_mat), rv, rk)
        oc_f, oc_b = _gdn_scan(u, w, qk, qe, ke, dl, n_ctx)
        x1 = _merge(xs, mod[l], ya, yb, oc_f, oc_b, z, gate, p["onorm"], p["wpa"], p["wpb"], p["wpc"], p["wout"],
                    nctt, b, t_off)
        if last:
            xs = _ffn(x1, mod[l], row2(norm2[l]), p["wup"], p["fconv"], p["wdn"], row2(norm_f),
                      0, b, (0,), (s // tm - 1,), True)
        else:
            xs = _ffn(x1, mod[l], row2(norm2[l]), p["wup"], p["fconv"], p["wdn"], row2(norm_f),
                      nctt, b, (0, nctt), (nctt - 1, ntt - 1), False)
    return xs
```

```python
import functools

import numpy as np
import jax
import jax.numpy as jnp
from jax import lax
from jax.experimental import pallas as pl
from jax.experimental.pallas import tpu as pltpu

F32 = jnp.float32
BF16 = jnp.bfloat16

GRID_W = 64
ROPE_THETA = 10000.0
NEG_INF = -1e30
EPS = 1e-6
LOG2E = 1.4426950408889634

MLA_HEADS = 8
MLA_Q_RANK = 384
MLA_KV_RANK = 256
MLA_NOPE = 64
MLA_ROPE = 32
MLA_V = 64

SWA_HEADS = 8
SWA_KV_HEADS = 2
SWA_HEAD_DIM = 64
WINDOW = 128

GDN_HEADS = 8
GDN_DK = 64
GDN_DV = 64
GDN_CHUNK = 64

FFN_DIM = 2816

LANES = 128
SUBLANES = 8
VMEM_LIMIT_BYTES = 56 * 1024 * 1024

TOKEN_TILE = 256
SWA_TILE = 128
MLA_Q_TILE = 512
PACK = 4
PACK_W = PACK * GDN_DK
FFN_SPLIT = 2


def _sigmoid(x):
    return 1.0 / (1.0 + jnp.exp(-x))


def _silu(x):
    return x * _sigmoid(x)


def _softplus(x):
    return jnp.maximum(x, 0.0) + jnp.log(1.0 + jnp.exp(-jnp.abs(x)))


def _rms(x, g):
    return x * lax.rsqrt(jnp.mean(x * x, axis=-1, keepdims=True) + EPS) * g


def _mm(a, b):
    return jnp.dot(a.astype(BF16), b.astype(BF16), preferred_element_type=F32)


def _mm_nt(a, b):
    return lax.dot_general(a.astype(BF16), b.astype(BF16), (((1,), (1,)), ((), ())),
                           preferred_element_type=F32)


def _mm_tn(a, b):
    return lax.dot_general(a.astype(BF16), b.astype(BF16), (((0,), (0,)), ((), ())),
                           preferred_element_type=F32)


def _split_bf16(x, parts):
    out = []
    for _ in range(parts - 1):
        hi = x.astype(BF16)
        out.append(hi)
        x = x - hi.astype(F32)
    out.append(x.astype(BF16))
    return out


def _group_sum(x, ones_blk):
    n = x.shape[0]
    r = jnp.dot(jnp.concatenate(_split_bf16(x, 2), axis=0), ones_blk, preferred_element_type=F32)
    return r[0:n] + r[n:2 * n]


def _iota(shape, dim):
    return lax.broadcasted_iota(jnp.int32, shape, dim)


def _group_ones(n, group):
    sh = group.bit_length() - 1
    return (jnp.right_shift(_iota((n, n), 0), sh) == jnp.right_shift(_iota((n, n), 1), sh)).astype(F32)


def _rope(x, c, sa, sb, half):
    n = x.shape[-1]
    return x * c + pltpu.roll(x, n - half, 1) * sa + pltpu.roll(x, half, 1) * sb


def _cparams(sem):
    return pltpu.CompilerParams(dimension_semantics=sem, vmem_limit_bytes=VMEM_LIMIT_BYTES)


def _mod_kernel(c_ref, w_ref, b_ref, o_ref):
    s = _silu(c_ref[...])
    o_ref[0] = _mm(s, w_ref[0]) + b_ref[0]


def _modulation(cc, w_mod, b_mod):
    depth, d, n = w_mod.shape
    r = cc.shape[0]
    tn = 2048
    return pl.pallas_call(
        _mod_kernel,
        grid=(depth, n // tn),
        in_specs=[pl.BlockSpec((r, d), lambda l, j: (0, 0)),
                  pl.BlockSpec((1, d, tn), lambda l, j: (l, 0, j)),
                  pl.BlockSpec((1, 1, tn), lambda l, j: (l, 0, j))],
        out_specs=pl.BlockSpec((1, r, tn), lambda l, j: (l, 0, j)),
        out_shape=jax.ShapeDtypeStruct((depth, r, n), F32),
        compiler_params=_cparams(("parallel", "parallel")),
        name="modulation",
    )(cc, w_mod, b_mod.reshape(depth, 1, n))


_P_CQ = (0, 384)
_P_CKV = (384, 640)
_P_KR = (640, 768)
_P_SQ = (768, 1280)
_P_SK = (1280, 1408)
_P_SV = (1408, 1536)
_P_QKV = (1536, 3072)
_P_Z = (3072, 3584)
_P_AB = (3584, 3712)
_P_GATE = (3712, 6784)
_P_WIDTH = 6784


def _proj_kernel(x_ref, mod_ref, g1_ref, win_ref, qn_ref, kvn_ref, wuq_ref, wukvk_ref, wukvv_ref,
                 mc_ref, ma_ref, mb_ref, sc_ref, sa_ref, sb_ref,
                 qm_ref, km_ref, vm_ref, sq_ref, sk_ref, sv_ref, qkv_ref, z_ref, ab_ref, gate_ref):
    x = x_ref[0]
    d = x.shape[-1]
    m = mod_ref[0]
    xb = (_rms(x, g1_ref[...]) * (1.0 + m[:, d:2 * d]) + m[:, 0:d]).astype(BF16)

    def proj(piece):
        return jnp.dot(xb, win_ref[:, piece[0]:piece[1]], preferred_element_type=F32)

    mc, ma, mb = mc_ref[...], ma_ref[...], mb_ref[...]
    sc, sa, sb = sc_ref[...], sa_ref[...], sb_ref[...]

    cq = _rms(proj(_P_CQ), qn_ref[...])
    ckv = _rms(proj(_P_CKV), kvn_ref[...])

    gate_ref[0] = proj(_P_GATE).astype(gate_ref.dtype)
    qkv_ref[0] = proj(_P_QKV)
    z_ref[0] = proj(_P_Z).astype(z_ref.dtype)
    ab_ref[0] = proj(_P_AB)
    sv_ref[0] = proj(_P_SV).astype(sv_ref.dtype)

    q = _mm(cq, wuq_ref[...]) * ((MLA_NOPE + MLA_ROPE) ** -0.5 * LOG2E)
    kn = _mm(ckv, wukvk_ref[...])
    vm_ref[0] = _mm(ckv, wukvv_ref[...]).astype(vm_ref.dtype)
    kr = _rope(proj(_P_KR), mc, ma, mb, MLA_ROPE // 4)
    for h in range(MLA_HEADS):
        sl = slice(h * LANES, (h + 1) * LANES)
        qm_ref[0, :, sl] = _rope(q[:, sl], mc, ma, mb, MLA_ROPE // 4).astype(qm_ref.dtype)
        km_ref[0, :, sl] = (kn[:, sl] + kr).astype(km_ref.dtype)

    sq = proj(_P_SQ) * (SWA_HEAD_DIM ** -0.5 * LOG2E)
    for g in range(SWA_HEADS * SWA_HEAD_DIM // LANES):
        sl = slice(g * LANES, (g + 1) * LANES)
        sq_ref[0, :, sl] = _rope(sq[:, sl], sc, sa, sb, SWA_HEAD_DIM // 4).astype(sq_ref.dtype)
    sk_ref[0] = _rope(proj(_P_SK), sc, sa, sb, SWA_HEAD_DIM // 4).astype(sk_ref.dtype)


def _proj(xs, mod_l, g1, win, qn, kvn, wuq, wukvk, wukvv, rope_tabs, nctt, ctx_row):
    b, t, d = xs.shape
    tm = TOKEN_TILE
    nt = t // tm

    def tok(w):
        return pl.BlockSpec((1, tm, w), lambda i, j: (i, j, 0))

    def full(a):
        return pl.BlockSpec(a.shape, lambda i, j: (0,) * a.ndim)

    tab = pl.BlockSpec((tm, LANES), lambda i, j: (j, 0))
    mod_spec = pl.BlockSpec((1, 1, mod_l.shape[-1]),
                            lambda i, j: (jnp.where(j < nctt, ctx_row, i), 0, 0))
    widths = (1024, 1024, 512, 512, 128, 128, 1536, 512, 128, 3072)
    dtypes = (BF16, BF16, BF16, BF16, BF16, BF16, F32, BF16, F32, BF16)
    return pl.pallas_call(
        _proj_kernel,
        grid=(b, nt),
        in_specs=[tok(d), mod_spec, full(g1), full(win), full(qn), full(kvn), full(wuq),
                  full(wukvk), full(wukvv)] + [tab] * 6,
        out_specs=[tok(w) for w in widths],
        out_shape=[jax.ShapeDtypeStruct((b, t, w), dt) for w, dt in zip(widths, dtypes)],
        compiler_params=_cparams(("parallel", "parallel")),
        name="proj",
    )(xs, mod_l, g1, win, qn, kvn, wuq, wukvk, wukvv, *rope_tabs)


def _mla_kernel(q_ref, k_ref, v_ref, o_ref, *, n_ctx, q_tile, skip_ctx):
    n = pl.program_id(2) + (1 if skip_ctx else 0)

    def run(row0, nq, nk):
        low = _iota((nq, LANES), 1) < MLA_V
        v = v_ref[0, 0:nk, :]
        v_low = _iota((nk, LANES), 1) < MLA_V
        one = jnp.ones_like(v)
        ss = [_mm_nt(q_ref[0, pl.ds(row0, nq), hh * LANES:(hh + 1) * LANES],
                     k_ref[0, 0:nk, hh * LANES:(hh + 1) * LANES]) for hh in range(2)]
        ps = [jnp.exp2(s - jnp.max(s, axis=-1, keepdims=True)) for s in ss]
        outs = []
        for hh in range(2):
            acc = _mm(ps[hh], jnp.where(v_low if hh == 0 else jnp.logical_not(v_low), v, one))
            den = acc[:, MLA_V:MLA_V + 1] if hh == 0 else acc[:, 0:1]
            outs.append(acc / den)
        o_ref[0, pl.ds(row0, nq), :] = jnp.where(low, outs[0], outs[1]).astype(o_ref.dtype)

    if not skip_ctx:
        @pl.when(n == 0)
        def _():
            run(0, n_ctx, n_ctx)

    @pl.when(n > 0)
    def _():
        run(pl.multiple_of(n_ctx + (n - 1) * q_tile, LANES), q_tile, k_ref.shape[1])


def _mla(qm, km, vm, n_ctx, skip_ctx):
    b, t, _ = qm.shape
    q_tile = MLA_Q_TILE
    assert (t - n_ctx) % q_tile == 0
    steps = (t - n_ctx) // q_tile + (0 if skip_ctx else 1)
    return pl.pallas_call(
        functools.partial(_mla_kernel, n_ctx=n_ctx, q_tile=q_tile, skip_ctx=skip_ctx),
        grid=(b, MLA_HEADS // 2, steps),
        in_specs=[pl.BlockSpec((1, t, 2 * LANES), lambda i, j, n: (i, 0, j)),
                  pl.BlockSpec((1, t, 2 * LANES), lambda i, j, n: (i, 0, j)),
                  pl.BlockSpec((1, t, LANES), lambda i, j, n: (i, 0, j))],
        out_specs=pl.BlockSpec((1, t, LANES), lambda i, j, n: (i, 0, j)),
        out_shape=jax.ShapeDtypeStruct((b, t, MLA_HEADS * MLA_V), BF16),
        compiler_params=_cparams(("parallel", "parallel", "arbitrary")),
        name="mla",
    )(qm, km, vm)


def _swa_kernel(sink_ref, q_ref, k_ref, v_ref, o_ref, *, n_ctx, q_off):
    i = pl.program_id(1) + q_off
    blk = q_ref.shape[1]
    t = k_ref.shape[1]
    n_lat = t - n_ctx
    nctt = n_ctx // blk
    low = _iota((blk, LANES), 1) < SWA_HEAD_DIM
    n_slab = SWA_HEADS * SWA_HEAD_DIM // LANES
    half = n_slab * blk

    parts, sinks = [], []
    for hh in range(2):
        for g in range(n_slab):
            qs = q_ref[0, :, g * LANES:(g + 1) * LANES]
            parts.append(jnp.where(low if hh == 0 else jnp.logical_not(low), qs, jnp.zeros_like(qs)))
            head = hh * n_slab + g
            sinks.append(jnp.broadcast_to(sink_ref[head:head + 1, 0:1] * LOG2E, (blk, 1)))
    q_all = jnp.concatenate(parts, axis=0)
    sink = jnp.concatenate(sinks, axis=0)

    def with_ones(v, hh):
        v_low = _iota(v.shape, 1) < SWA_HEAD_DIM
        return jnp.where(v_low if hh == 0 else jnp.logical_not(v_low), v, jnp.ones_like(v))

    def attend(loc):
        kc = k_ref[0, 0:n_ctx, :]
        vc = v_ref[0, 0:n_ctx, :]
        s_c = _mm_nt(q_all, kc)
        mx = jnp.maximum(jnp.max(s_c, axis=-1, keepdims=True), sink)
        if loc is not None:
            kl, vl, valid = loc
            s_l = jnp.where(valid, _mm_nt(q_all, kl), NEG_INF)
            mx = jnp.maximum(mx, jnp.max(s_l, axis=-1, keepdims=True))
        p_c = jnp.exp2(s_c - mx).astype(BF16)
        p_sink = jnp.exp2(sink - mx)
        if loc is not None:
            p_l = jnp.exp2(s_l - mx).astype(BF16)
        outs = []
        for hh in range(2):
            rows = slice(hh * half, (hh + 1) * half)
            acc = _mm(p_c[rows], with_ones(vc, hh))
            if loc is not None:
                acc = acc + _mm(p_l[rows], with_ones(vl, hh))
            den = (acc[:, SWA_HEAD_DIM:SWA_HEAD_DIM + 1] if hh == 0 else acc[:, 0:1]) + p_sink[rows]
            outs.append(acc / den)
        for g in range(n_slab):
            rows = slice(g * blk, (g + 1) * blk)
            o_ref[0, :, g * LANES:(g + 1) * LANES] = jnp.where(low, outs[0][rows], outs[1][rows]).astype(o_ref.dtype)

    if q_off < nctt:
        @pl.when(i < nctt)
        def _():
            attend(None)

    @pl.when(i >= nctt)
    def _():
        n = i - nctt
        span = 3 * blk
        start = jnp.clip((n - 1) * blk, 0, n_lat - span)
        row0 = pl.multiple_of(n_ctx + start, blk)
        kl = k_ref[0, pl.ds(row0, span), :]
        vl = v_ref[0, pl.ds(row0, span), :]
        kpos = start + _iota((blk, span), 1)
        qpos = n * blk + _iota((blk, span), 0)
        valid = jnp.abs(kpos - qpos) <= WINDOW
        attend((kl, vl, jnp.concatenate([valid] * (2 * n_slab), axis=0)))


def _swa(sink, sq, sk, sv, n_ctx, q_off):
    b, t, w = sq.shape
    blk = SWA_TILE
    nt = t // blk - q_off
    return pl.pallas_call(
        functools.partial(_swa_kernel, n_ctx=n_ctx, q_off=q_off),
        grid=(b, nt),
        in_specs=[pl.BlockSpec(sink.shape, lambda i, n: (0, 0)),
                  pl.BlockSpec((1, blk, w), lambda i, n: (i, n + q_off, 0)),
                  pl.BlockSpec((1, t, LANES), lambda i, n: (i, 0, 0)),
                  pl.BlockSpec((1, t, LANES), lambda i, n: (i, 0, 0))],
        out_specs=pl.BlockSpec((1, blk, w), lambda i, n: (i, n + q_off, 0)),
        out_shape=jax.ShapeDtypeStruct((b, t, w), BF16),
        compiler_params=_cparams(("parallel", "parallel")),
        name="swa",
    )(sink, sq, sk, sv)


def _block_diag(x, bd_mask):
    return jnp.where(bd_mask, jnp.concatenate([x] * PACK, axis=0), 0.0)


def _gdn_prep_kernel(qkv_ref, prev_ref, next_ref, ab_ref, cw_ref, gp_ref,
                     a_ref, rv_ref, rk_ref, qk_ref, qe_ref, ke_ref, dl_ref, *, seg_starts, seg_ends):
    ti = pl.program_id(1)
    tm = qkv_ref.shape[1]
    hw = GDN_HEADS * GDN_DK
    x = qkv_ref[0]
    row = _iota((tm, 1), 0)
    is_start = functools.reduce(jnp.logical_or, [ti == s for s in seg_starts])
    is_end = functools.reduce(jnp.logical_or, [ti == s for s in seg_ends])
    prev_row = jnp.where(is_start, 0.0, prev_ref[0, SUBLANES - 1:SUBLANES, :])
    next_row = jnp.where(is_end, 0.0, next_ref[0, 0:1, :])
    x_prev = jnp.where(row == 0, prev_row, pltpu.roll(x, 1, 0))
    x_next = jnp.where(row == tm - 1, next_row, pltpu.roll(x, tm - 1, 0))
    y = _silu(x_prev * cw_ref[0:1, :] + x * cw_ref[1:2, :] + x_next * cw_ref[2:3, :])

    ones_blk = _group_ones(LANES, GDN_DK).astype(BF16)

    def l2(slab):
        return slab * lax.rsqrt(_group_sum(slab * slab, ones_blk) + EPS)

    q = jnp.concatenate([l2(y[:, s * LANES:(s + 1) * LANES]) for s in range(hw // LANES)], axis=1)
    q = q * (GDN_DK ** -0.5)
    k = jnp.concatenate([l2(y[:, hw + s * LANES:hw + (s + 1) * LANES]) for s in range(hw // LANES)], axis=1)
    v = y[:, 2 * hw:3 * hw]

    ab = ab_ref[0]
    g_all = -jnp.exp(gp_ref[0:1, :]) * _softplus(ab + gp_ref[1:2, :])
    beta_all = _sigmoid(ab)
    ri = _iota((tm, tm), 0)
    ci = _iota((tm, tm), 1)
    same = jnp.right_shift(ri, 6) == jnp.right_shift(ci, 6)
    g_parts = jnp.concatenate(_split_bf16(g_all, 3), axis=1)

    def cumsum(tri):
        r = jnp.dot(jnp.logical_and(same, tri).astype(BF16), g_parts, preferred_element_type=F32)
        return r[:, 0:LANES] + r[:, LANES:2 * LANES] + r[:, 2 * LANES:3 * LANES]

    gam_all = jnp.where(_iota((tm, LANES), 1) < GDN_HEADS, cumsum(ci <= ri), cumsum(ci >= ri))
    col_head = jnp.right_shift(_iota((LANES, 2 * hw), 1), GDN_DK.bit_length() - 1)
    src = _iota((LANES, 2 * hw), 0)

    def expand(x, parts, sel):
        r = jnp.dot(jnp.concatenate(_split_bf16(x, parts), axis=0), sel.astype(BF16), preferred_element_type=F32)
        return functools.reduce(lambda a, b: a + b, [r[p * tm:(p + 1) * tm] for p in range(parts)])

    gam_exp = expand(gam_all, 3, src == col_head)
    b_exp = expand(beta_all, 2, src == col_head + 2 * GDN_HEADS)
    gam_f, gam_b = gam_exp[:, 0:hw], gam_exp[:, hw:2 * hw]

    c = GDN_CHUNK
    r64 = _iota((c, PACK_W), 0)
    c64 = jnp.bitwise_and(_iota((c, PACK_W), 1), c - 1)
    eye = (r64 == c64).astype(F32)
    bd_mask = jnp.right_shift(_iota((PACK_W, PACK_W), 0), 6) == jnp.right_shift(_iota((PACK_W, PACK_W), 1), 6)

    for ch in range(tm // c):
        rows = slice(ch * c, (ch + 1) * c)
        for gi in range(hw // PACK_W):
            lanes = slice(gi * PACK_W, (gi + 1) * PACK_W)
            kc, qc, vc = k[rows, lanes], q[rows, lanes], v[rows, lanes]
            kbd = _block_diag(kc, bd_mask)
            kq = _mm_nt(jnp.concatenate([kc, qc], axis=0), kbd)
            kk, qkm = kq[0:c], kq[c:2 * c]
            for d in range(2):
                gam = (gam_f if d == 0 else gam_b)[rows, lanes]
                beta = b_exp[rows, d * hw + gi * PACK_W:d * hw + (gi + 1) * PACK_W]
                gam_row = jnp.sum(gam * eye, axis=0, keepdims=True)
                decay = jnp.exp(jnp.minimum(gam - gam_row, 0.0))
                strict = (r64 > c64) if d == 0 else (r64 < c64)
                incl = (r64 >= c64) if d == 0 else (r64 <= c64)
                a_mat = jnp.where(strict, beta * kk * decay, 0.0)
                for pp in range(PACK_W // LANES):
                    a_ref[d, ch * (hw // LANES) + gi * (PACK_W // LANES) + pp] = \
                        a_mat[:, pp * LANES:(pp + 1) * LANES]
                e_gam = jnp.exp(gam)
                g_last = gam[c - 1:c, :] if d == 0 else gam[0:1, :]
                out = slice(d * hw + gi * PACK_W, d * hw + (gi + 1) * PACK_W)
                rv_ref[0, rows, out] = (vc * beta).astype(rv_ref.dtype)
                rk_ref[0, rows, out] = (kc * beta * e_gam).astype(rk_ref.dtype)
                qk_ref[0, rows, out] = jnp.where(incl, qkm * decay, 0.0).astype(qk_ref.dtype)
                qe_ref[0, rows, out] = (qc * e_gam).astype(qe_ref.dtype)
                ke_ref[0, rows, out] = (kc * jnp.exp(g_last - gam)).astype(ke_ref.dtype)
                dl_ref[0, ch, :, out] = jnp.exp(g_last)


def _gdn_prep(qkv, ab, cw, gp, seg_starts, seg_ends):
    b, t, w = qkv.shape
    tm = TOKEN_TILE
    nt = t // tm
    hb = tm // SUBLANES
    nh = t // SUBLANES
    wide = 2 * GDN_HEADS * GDN_DK
    c = GDN_CHUNK
    cpt = tm // c
    spt = cpt * (GDN_HEADS * GDN_DK // LANES)
    tok = lambda width: pl.BlockSpec((1, tm, width), lambda i, j: (i, j, 0))
    out_shape = [jax.ShapeDtypeStruct((2, b * nt * spt, c, LANES), F32)] + \
                [jax.ShapeDtypeStruct((b, t, wide), BF16)] * 5 + \
                [jax.ShapeDtypeStruct((b, t // c, 1, wide), F32)]
    return pl.pallas_call(
        functools.partial(_gdn_prep_kernel, seg_starts=seg_starts, seg_ends=seg_ends),
        grid=(b, nt),
        in_specs=[tok(w),
                  pl.BlockSpec((1, SUBLANES, w), lambda i, j: (i, jnp.maximum(j * hb - 1, 0), 0)),
                  pl.BlockSpec((1, SUBLANES, w), lambda i, j: (i, jnp.minimum((j + 1) * hb, nh - 1), 0)),
                  tok(LANES),
                  pl.BlockSpec(cw.shape, lambda i, j: (0, 0)),
                  pl.BlockSpec(gp.shape, lambda i, j: (0, 0))],
        out_specs=[pl.BlockSpec((2, spt, c, LANES), lambda i, j: (0, i * nt + j, 0, 0))] +
                  [tok(wide)] * 5 + [pl.BlockSpec((1, cpt, 1, wide), lambda i, j: (i, j, 0, 0))],
        out_shape=out_shape,
        compiler_params=_cparams(("parallel", "parallel")),
        name="gdn_prep",
    )(qkv, qkv, qkv, ab, cw, gp)


def _gdn_solve_kernel(a_ref, o_ref, at_ref, x_ref, acc_ref, ot_ref):
    c = GDN_CHUNK
    sub = SUBLANES
    nb = c // sub
    a_rows = pltpu.einshape("sil->isl", a_ref[0])
    for i in range(c):
        at_ref[i * LANES:(i + 1) * LANES, :] = a_rows[i].T
    x_ref[...] = jnp.zeros_like(x_ref)
    row_iota = _iota((LANES, LANES), 0)

    def solve(upper):
        def row(step, carry):
            i = (c - 1 - step) if upper else step
            base = pl.multiple_of(i * LANES, LANES)
            acc_ref[...] = (jnp.bitwise_and(row_iota, c - 1) == i).astype(F32)
            for kb in range(nb):
                active = (kb * sub + sub - 1 > i) if upper else (kb * sub < i)

                @pl.when(active)
                def _():
                    for h2 in range(LANES // c):
                        coefs = [at_ref[pl.ds(base + h2 * c + kb * sub + kk, 1), :] for kk in range(sub)]
                        for jb in (range(kb, nb) if upper else range(kb + 1)):
                            dst = slice(h2 * c + jb * sub, h2 * c + (jb + 1) * sub)
                            acc = acc_ref[dst, :]
                            for kk in range(sub):
                                lo = (kb * sub + kk) * LANES + h2 * c + jb * sub
                                acc = acc - coefs[kk] * x_ref[lo:lo + sub, :]
                            acc_ref[dst, :] = acc
            x_ref[pl.ds(base, LANES), :] = acc_ref[...]
            return carry

        lax.fori_loop(0, c, row, 0)

    @pl.when(pl.program_id(0) == 0)
    def _():
        solve(False)

    @pl.when(pl.program_id(0) == 1)
    def _():
        solve(True)

    for i in range(c):
        ot_ref[i] = x_ref[i * LANES:(i + 1) * LANES, :].T
    o_ref[0] = pltpu.einshape("isl->sil", ot_ref[...])


def _gdn_solve(a):
    _, n_slab, c, _ = a.shape
    assert n_slab % LANES == 0
    blk = pl.BlockSpec((1, LANES, c, LANES), lambda d, g: (d, g, 0, 0))
    return pl.pallas_call(
        _gdn_solve_kernel,
        grid=(2, n_slab // LANES),
        in_specs=[blk],
        out_specs=blk,
        out_shape=jax.ShapeDtypeStruct(a.shape, F32),
        scratch_shapes=[pltpu.VMEM((c * LANES, LANES), F32), pltpu.VMEM((c * LANES, LANES), F32),
                        pltpu.VMEM((LANES, LANES), F32), pltpu.VMEM((c, LANES, LANES), F32)],
        compiler_params=_cparams(("parallel", "parallel")),
        name="gdn_solve",
    )(a)


def _gdn_apply_kernel(t_ref, rv_ref, rk_ref, u_ref, w_ref):
    tm = rv_ref.shape[1]
    c = GDN_CHUNK
    hw = GDN_HEADS * GDN_DK
    ppg = PACK_W // LANES
    bd_mask = jnp.right_shift(_iota((PACK_W, PACK_W), 0), 6) == jnp.right_shift(_iota((PACK_W, PACK_W), 1), 6)
    for ch in range(tm // c):
        rows = slice(ch * c, (ch + 1) * c)
        for gi in range(hw // PACK_W):
            for d in range(2):
                out = slice(d * hw + gi * PACK_W, d * hw + (gi + 1) * PACK_W)
                tinv = jnp.concatenate([t_ref[d, ch * (hw // LANES) + gi * ppg + pp] for pp in range(ppg)],
                                       axis=1)
                t16 = tinv.astype(BF16)
                u_ref[0, rows, out] = jnp.dot(t16, _block_diag(rv_ref[0, rows, out], bd_mask),
                                              preferred_element_type=F32)
                w_ref[0, rows, out] = jnp.dot(t16, _block_diag(rk_ref[0, rows, out], bd_mask),
                                              preferred_element_type=F32).astype(w_ref.dtype)


def _gdn_apply(tinv, rv, rk):
    b, t, wide = rv.shape
    tm = TOKEN_TILE
    c = GDN_CHUNK
    nt = t // tm
    spt = (tm // c) * (GDN_HEADS * GDN_DK // LANES)
    tok = pl.BlockSpec((1, tm, wide), lambda i, j: (i, j, 0))
    return pl.pallas_call(
        _gdn_apply_kernel,
        grid=(b, nt),
        in_specs=[pl.BlockSpec((2, spt, c, LANES), lambda i, j: (0, i * nt + j, 0, 0)), tok, tok],
        out_specs=[tok, tok],
        out_shape=[jax.ShapeDtypeStruct((b, t, wide), F32), jax.ShapeDtypeStruct((b, t, wide), BF16)],
        compiler_params=_cparams(("parallel", "parallel")),
        name="gdn_apply",
    )(tinv, rv, rk)


def _gdn_scan_kernel(uf, wf, qkf, qef, kef, dlf, ub, wb, qkb, qeb, keb, dlb, of_ref, ob_ref, s_ref):
    c = GDN_CHUNK
    cpt = uf.shape[1] // c
    ng = uf.shape[2] // PACK_W
    bd_mask = jnp.right_shift(_iota((PACK_W, PACK_W), 0), 6) == jnp.right_shift(_iota((PACK_W, PACK_W), 1), 6)

    @pl.when(pl.program_id(1) == 0)
    def _():
        s_ref[...] = jnp.zeros_like(s_ref)

    chains = ((uf, wf, qkf, qef, kef, dlf, of_ref), (ub, wb, qkb, qeb, keb, dlb, ob_ref))
    for j in range(cpt):
        where = []
        for d in range(2):
            ch = j if d == 0 else cpt - 1 - j
            for gi in range(ng):
                where.append((d, gi, ch, slice(ch * c, (ch + 1) * c), slice(gi * PACK_W, (gi + 1) * PACK_W)))
        states = [s_ref[d * ng + gi] for d, gi, _, _, _ in where]
        wqs = [_mm(jnp.concatenate([chains[d][1][0, rows, lanes], chains[d][3][0, rows, lanes]], axis=0), st)
               for (d, gi, ch, rows, lanes), st in zip(where, states)]
        v_news = [chains[d][0][0, rows, lanes] - wq[0:c] for (d, gi, ch, rows, lanes), wq in zip(where, wqs)]
        kvs = [_mm_tn(chains[d][4][0, rows, lanes], v_new) for (d, gi, ch, rows, lanes), v_new in zip(where, v_news)]
        for (d, gi, ch, rows, lanes), st, kv in zip(where, states, kvs):
            s_ref[d * ng + gi] = st * chains[d][5][0, ch, :, lanes] + jnp.where(bd_mask, kv, 0.0)
        for (d, gi, ch, rows, lanes), wq, v_new in zip(where, wqs, v_news):
            chains[d][6][0, rows, lanes] = wq[c:2 * c] + _mm(chains[d][2][0, rows, lanes],
                                                             _block_diag(v_new, bd_mask))


def _gdn_scan(u, w, qk, qe, ke, dl, n_ctx):
    b, t, wide = u.shape
    tm = TOKEN_TILE
    nt = t // tm
    nctt = n_ctx // tm
    hw = wide // 2
    cpt = tm // GDN_CHUNK

    def mirror(g):
        return jnp.where(g < nctt, nctt - 1 - g, nt - 1 - (g - nctt))

    fwd = pl.BlockSpec((1, tm, hw), lambda i, g: (i, g, 0))
    bwd = pl.BlockSpec((1, tm, hw), lambda i, g: (i, mirror(g), 1))
    dfwd = pl.BlockSpec((1, cpt, 1, hw), lambda i, g: (i, g, 0, 0))
    dbwd = pl.BlockSpec((1, cpt, 1, hw), lambda i, g: (i, mirror(g), 0, 1))
    out_sds = jax.ShapeDtypeStruct((b, t, hw), F32)
    return pl.pallas_call(
        _gdn_scan_kernel,
        grid=(b, nt),
        in_specs=[fwd] * 5 + [dfwd] + [bwd] * 5 + [dbwd],
        out_specs=[pl.BlockSpec((1, tm, hw), lambda i, g: (i, g, 0)),
                   pl.BlockSpec((1, tm, hw), lambda i, g: (i, mirror(g), 0))],
        out_shape=[out_sds, out_sds],
        scratch_shapes=[pltpu.VMEM((2 * hw // PACK_W, PACK_W, PACK_W), F32)],
        compiler_params=_cparams(("parallel", "arbitrary")),
        name="gdn_scan",
    )(u, w, qk, qe, ke, dl, u, w, qk, qe, ke, dl)


def _merge_kernel(x_ref, mod_ref, ya_ref, yb_ref, of_ref, ob_ref, z_ref, gate_ref, on_ref,
                  wpa_ref, wpb_ref, wpc_ref, wout_ref, o_ref):
    x = x_ref[0]
    d = x.shape[-1]
    m = mod_ref[0]
    oc = of_ref[0] + ob_ref[0]
    ones_blk = _group_ones(LANES, GDN_DV).astype(BF16)
    ms = jnp.concatenate(
        [_group_sum(oc[:, s * LANES:(s + 1) * LANES] ** 2, ones_blk) for s in range(oc.shape[-1] // LANES)],
        axis=1) * (1.0 / GDN_DV)
    yc = oc * lax.rsqrt(ms + EPS) * on_ref[...] * _silu(z_ref[0].astype(F32))
    gate = gate_ref[0].astype(F32)
    mix = (_sigmoid(gate[:, 0:d]) * _mm(ya_ref[0], wpa_ref[...])
           + _sigmoid(gate[:, d:2 * d]) * _mm(yb_ref[0], wpb_ref[...])
           + _sigmoid(gate[:, 2 * d:3 * d]) * _mm(yc, wpc_ref[...]))
    o_ref[0] = x + m[:, 2 * d:3 * d] * _mm(mix, wout_ref[...])


def _merge(xs, mod_l, ya, yb, oc_f, oc_b, z, gate, onorm, wpa, wpb, wpc, wout, nctt, ctx_row, t_off):
    b, t, d = xs.shape
    tm = TOKEN_TILE
    nt = t // tm - t_off
    tok = lambda width: pl.BlockSpec((1, tm, width), lambda i, j: (i, j + t_off, 0))
    full = lambda a: pl.BlockSpec(a.shape, lambda i, j: (0,) * a.ndim)
    mod_spec = pl.BlockSpec((1, 1, mod_l.shape[-1]),
                            lambda i, j: (jnp.where(j + t_off < nctt, ctx_row, i), 0, 0))
    return pl.pallas_call(
        _merge_kernel,
        grid=(b, nt),
        in_specs=[tok(d), mod_spec, tok(512), tok(512), tok(512), tok(512), tok(512), tok(3 * d),
                  full(onorm), full(wpa), full(wpb), full(wpc), full(wout)],
        out_specs=pl.BlockSpec((1, tm, d), lambda i, j: (i, j, 0)),
        out_shape=jax.ShapeDtypeStruct((b, nt * tm, d), F32),
        compiler_params=_cparams(("parallel", "parallel")),
        name="merge",
    )(xs, mod_l, ya, yb, oc_f, oc_b, z, gate, onorm, wpa, wpb, wpc, wout)


def _ffn_kernel(x_ref, prev_ref, next_ref, mod_ref, g2_ref, wup_ref, cw_ref, wdn_ref, nf_ref, o_ref,
                *, seg_starts, seg_ends, final):
    ti = pl.program_id(1)
    x = x_ref[0]
    tm, d = x.shape
    m = mod_ref[0]
    xall = jnp.concatenate([prev_ref[0], x, next_ref[0]], axis=0)
    xa = _rms(xall, g2_ref[...]) * (1.0 + m[:, 4 * d:5 * d]) + m[:, 3 * d:4 * d]
    n_all = tm + 2 * SUBLANES
    row = _iota((n_all, 1), 0)
    is_start = functools.reduce(jnp.logical_or, [ti == s for s in seg_starts])
    is_end = functools.reduce(jnp.logical_or, [ti == s for s in seg_ends])
    drop = jnp.logical_or(jnp.logical_and(row == SUBLANES - 1, is_start),
                          jnp.logical_and(row == SUBLANES + tm, is_end))
    xa = jnp.where(drop, 0.0, xa).astype(BF16)

    def conv(h, cols):
        hp = pltpu.roll(h, 1, 0)[SUBLANES:SUBLANES + tm]
        hn = pltpu.roll(h, n_all - 1, 0)[SUBLANES:SUBLANES + tm]
        return (hp * cw_ref[0:1, cols] + h[SUBLANES:SUBLANES + tm] * cw_ref[1:2, cols]
                + hn * cw_ref[2:3, cols])

    cwid = FFN_DIM // FFN_SPLIT
    acc = jnp.zeros((tm, d), F32)
    for j in range(FFN_SPLIT):
        ca = slice(j * cwid, (j + 1) * cwid)
        cb = slice(FFN_DIM + j * cwid, FFN_DIM + (j + 1) * cwid)
        ha = conv(jnp.dot(xa, wup_ref[:, ca], preferred_element_type=F32), ca)
        hb = conv(jnp.dot(xa, wup_ref[:, cb], preferred_element_type=F32), cb)
        acc = acc + _mm(_silu(ha) * hb, wdn_ref[ca, :])
    y = x + m[:, 5 * d:6 * d] * acc
    if final:
        y = _rms(y, nf_ref[...])
    o_ref[0] = y


def _ffn(x1, mod_l, g2, wup, cw, wdn, nf, nctt_mod, ctx_row, seg_starts, seg_ends, final):
    b, t, d = x1.shape
    tm = TOKEN_TILE
    nt = t // tm
    hb = tm // SUBLANES
    nh = t // SUBLANES
    full = lambda a: pl.BlockSpec(a.shape, lambda i, j: (0,) * a.ndim)
    mod_spec = pl.BlockSpec((1, 1, mod_l.shape[-1]),
                            lambda i, j: (jnp.where(j < nctt_mod, ctx_row, i), 0, 0))
    return pl.pallas_call(
        functools.partial(_ffn_kernel, seg_starts=seg_starts, seg_ends=seg_ends, final=final),
        grid=(b, nt),
        in_specs=[pl.BlockSpec((1, tm, d), lambda i, j: (i, j, 0)),
                  pl.BlockSpec((1, SUBLANES, d), lambda i, j: (i, jnp.maximum(j * hb - 1, 0), 0)),
                  pl.BlockSpec((1, SUBLANES, d), lambda i, j: (i, jnp.minimum((j + 1) * hb, nh - 1), 0)),
                  mod_spec, full(g2), full(wup), full(cw), full(wdn), full(nf)],
        out_specs=pl.BlockSpec((1, tm, d), lambda i, j: (i, j, 0)),
        out_shape=jax.ShapeDtypeStruct((b, t, d), F32),
        compiler_params=_cparams(("parallel", "parallel")),
        name="ffn",
    )(x1, x1, x1, mod_l, g2, wup, cw, wdn, nf)


def _cols(w, segments):
    parts = [jnp.zeros(w.shape[:-1] + (seg,), w.dtype) if isinstance(seg, int) else w[..., seg[0]:seg[1]]
             for seg in segments]
    return jnp.concatenate(parts, axis=-1)


def _win_segments(d):
    widths = [MLA_Q_RANK, MLA_KV_RANK, MLA_ROPE, SWA_HEADS * SWA_HEAD_DIM, SWA_KV_HEADS * SWA_HEAD_DIM,
              SWA_KV_HEADS * SWA_HEAD_DIM, GDN_HEADS * (2 * GDN_DK + GDN_DV), GDN_HEADS * GDN_DV,
              2 * GDN_HEADS, 2 * GDN_HEADS, 3 * d]
    off = np.cumsum([0] + widths)
    cq, ckv, kr, sq, sk, sv, qkv, z, a, b, gate = [(int(off[i]), int(off[i + 1])) for i in range(11)]
    sq_perm = [(sq[0] + h * SWA_HEAD_DIM, sq[0] + (h + 1) * SWA_HEAD_DIM) for h in _swa_head_order()]
    segs = [cq, ckv, MLA_NOPE, kr, LANES - MLA_NOPE - MLA_ROPE] + sq_perm + \
           [sk, sv, qkv, z, a, b, LANES - 4 * GDN_HEADS, gate]
    return segs


def _swa_head_order():
    group = SWA_HEADS // SWA_KV_HEADS
    return [kv * group + g for g in range(group) for kv in range(SWA_KV_HEADS)]


def _rope_tables(n_ctx, n_lat):
    t = np.arange(n_lat)
    pos = np.stack([t // GRID_W, t % GRID_W]).astype(np.float32)

    def build(head_w, rope_lo, rope_w):
        da = rope_w // 2
        half = da // 2
        inv = ROPE_THETA ** (-np.arange(half, dtype=np.float32) / half)
        c = np.ones((n_lat, LANES), np.float32)
        sa = np.zeros((n_lat, LANES), np.float32)
        sb = np.zeros((n_lat, LANES), np.float32)
        for lane in range(LANES):
            dd = lane % head_w - rope_lo
            if dd < 0 or dd >= rope_w:
                continue
            axis, r = dd // da, dd % da
            ang = pos[axis] * inv[r % half]
            c[:, lane] = np.cos(ang)
            if r < half:
                sa[:, lane] = -np.sin(ang)
            else:
                sb[:, lane] = np.sin(ang)
        ctx = [np.ones((n_ctx, LANES), np.float32), np.zeros((n_ctx, LANES), np.float32),
               np.zeros((n_ctx, LANES), np.float32)]
        return [jnp.asarray(np.concatenate([cx, lt])) for cx, lt in zip(ctx, (c, sa, sb))]

    return build(LANES, MLA_NOPE, MLA_ROPE) + build(SWA_HEAD_DIM, 0, SWA_HEAD_DIM)


def kernel(x, c, ctx, c_ctx, w_mod, b_mod, norm1, norm2, w_in, mla_q_norm, mla_kv_norm, w_uq, w_ukv,
           swa_sink, gdn_conv, gdn_a_log, gdn_dt_bias, gdn_norm, w_branch_a, w_branch_b, w_branch_c,
           w_out, ffn_up, ffn_conv, ffn_down, norm_f):
    b, s, d = x.shape
    n_ctx = ctx.shape[1]
    depth = w_mod.shape[0]
    tm = TOKEN_TILE
    assert n_ctx % tm == 0 and s % tm == 0 and s >= 3 * SWA_TILE
    nctt = n_ctx // tm
    ntt = (n_ctx + s) // tm

    rows = -(-(b + 1) // SUBLANES) * SUBLANES
    cc = jnp.zeros((rows, d), F32).at[0:b].set(c).at[b].set(c_ctx)
    mod = _modulation(cc, w_mod, b_mod).reshape(depth, rows, 1, 6 * d)

    hd_q = MLA_NOPE + MLA_ROPE
    hd_kv = MLA_NOPE + MLA_V
    uq_segs = [s for h in range(MLA_HEADS) for s in ((h * hd_q, (h + 1) * hd_q), LANES - hd_q)]
    uk_segs = [s for h in range(MLA_HEADS) for s in ((h * hd_kv, h * hd_kv + MLA_NOPE), LANES - MLA_NOPE)]
    uv_segs = [(h * hd_kv + MLA_NOPE, (h + 1) * hd_kv) for h in range(MLA_HEADS)]
    win_segs = _win_segments(d)

    def layer_weights(l):
        wpb = jnp.concatenate([w_branch_b[l, h * SWA_HEAD_DIM:(h + 1) * SWA_HEAD_DIM] for h in _swa_head_order()],
                              axis=0)
        gpar = jnp.zeros((SUBLANES, LANES), F32)
        gpar = gpar.at[0, 0:2 * GDN_HEADS].set(gdn_a_log[l].reshape(-1))
        gpar = gpar.at[1, 0:2 * GDN_HEADS].set(gdn_dt_bias[l].reshape(-1))
        pad_rows = lambda a: jnp.pad(a, ((0, SUBLANES - a.shape[0]), (0, 0)))
        return dict(
            win=_cols(w_in[l], win_segs).astype(BF16),
            wuq=_cols(w_uq[l], uq_segs).astype(BF16),
            wukvk=_cols(w_ukv[l], uk_segs).astype(BF16),
            wukvv=_cols(w_ukv[l], uv_segs).astype(BF16),
            wpa=w_branch_a[l].astype(BF16), wpb=wpb.astype(BF16), wpc=w_branch_c[l].astype(BF16),
            wout=w_out[l].astype(BF16), wup=ffn_up[l].astype(BF16), wdn=ffn_down[l].astype(BF16),
            sink=jnp.broadcast_to(swa_sink[l][:, None], (SWA_HEADS, LANES)),
            onorm=jnp.tile(gdn_norm[l], GDN_HEADS).reshape(1, -1),
            gconv=pad_rows(gdn_conv[l]), fconv=pad_rows(ffn_conv[l]), gpar=gpar)

    rope_tabs = _rope_tables(n_ctx, s)

    xs = jnp.concatenate([ctx, x], axis=1)
    row2 = lambda a: a.reshape(1, -1)
    for l in range(depth):
        last = l == depth - 1
        t_off = nctt if last else 0
        p = layer_weights(l)
        qm, km, vm, sq, sk, sv, qkv, z, ab, gate = _proj(
            xs, mod[l], row2(norm1[l]), p["win"], row2(mla_q_norm[l]), row2(mla_kv_norm[l]),
            p["wuq"], p["wukvk"], p["wukvv"], rope_tabs, nctt, b)
        ya = _mla(qm, km, vm, n_ctx, last)
        yb = _swa(p["sink"], sq, sk, sv, n_ctx, t_off * (tm // SWA_TILE))
        a_mat, rv, rk, qk, qe, ke, dl = _gdn_prep(qkv, ab, p["gconv"], p["gpar"], (0, nctt), (nctt - 1, ntt - 1))
        u, w = _gdn_apply(_gdn_solve(a_mat), rv, rk)
        oc_f, oc_b = _gdn_scan(u, w, qk, qe, ke, dl, n_ctx)
        x1 = _merge(xs, mod[l], ya, yb, oc_f, oc_b, z, gate, p["onorm"], p["wpa"], p["wpb"], p["wpc"], p["wout"],
                    nctt, b, t_off)
        if last:
            xs = _ffn(x1, mod[l], row2(norm2[l]), p["wup"], p["fconv"], p["wdn"], row2(norm_f),
                      0, b, (0,), (s // tm - 1,), True)
        else:
            xs = _ffn(x1, mod[l], row2(norm2[l]), p["wup"], p["fconv"], p["wdn"], row2(norm_f),
                      nctt, b, (0, nctt), (nctt - 1, ntt - 1), False)
    return xs
```

```python
import functools

import numpy as np
import jax
import jax.numpy as jnp
from jax import lax
from jax.experimental import pallas as pl
from jax.experimental.pallas import tpu as pltpu

F32 = jnp.float32
BF16 = jnp.bfloat16

GRID_W = 64
ROPE_THETA = 10000.0
NEG_INF = -1e30
EPS = 1e-6
LOG2E = 1.4426950408889634

MLA_HEADS = 8
MLA_Q_RANK = 384
MLA_KV_RANK = 256
MLA_NOPE = 64
MLA_ROPE = 32
MLA_V = 64

SWA_HEADS = 8
SWA_KV_HEADS = 2
SWA_HEAD_DIM = 64
WINDOW = 128

GDN_HEADS = 8
GDN_DK = 64
GDN_DV = 64
GDN_CHUNK = 64

FFN_DIM = 2816

LANES = 128
SUBLANES = 8
VMEM_LIMIT_BYTES = 56 * 1024 * 1024

TOKEN_TILE = 256
SWA_TILE = 128
MLA_Q_TILE = 1024
PACK = 4
PACK_W = PACK * GDN_DK
FFN_SPLIT = 1


def _sigmoid(x):
    return 0.5 * jnp.tanh(0.5 * x) + 0.5


def _silu(x):
    return x * _sigmoid(x)


def _softplus(x):
    return jnp.maximum(x, 0.0) + jnp.log(1.0 + jnp.exp(-jnp.abs(x)))


def _rms(x, g):
    return x * lax.rsqrt(jnp.mean(x * x, axis=-1, keepdims=True) + EPS) * g


def _mm(a, b):
    return jnp.dot(a.astype(BF16), b.astype(BF16), preferred_element_type=F32)


def _mm_nt(a, b):
    return lax.dot_general(a.astype(BF16), b.astype(BF16), (((1,), (1,)), ((), ())),
                           preferred_element_type=F32)


def _mm_tn(a, b):
    return lax.dot_general(a.astype(BF16), b.astype(BF16), (((0,), (0,)), ((), ())),
                           preferred_element_type=F32)


def _split_bf16(x, parts):
    out = []
    for _ in range(parts - 1):
        hi = x.astype(BF16)
        out.append(hi)
        x = x - hi.astype(F32)
    out.append(x.astype(BF16))
    return out


def _group_sum(x, ones_blk):
    n = x.shape[0]
    r = jnp.dot(jnp.concatenate(_split_bf16(x, 2), axis=0), ones_blk, preferred_element_type=F32)
    return r[0:n] + r[n:2 * n]


def _iota(shape, dim):
    return lax.broadcasted_iota(jnp.int32, shape, dim)


def _group_ones(n, group):
    sh = group.bit_length() - 1
    return (jnp.right_shift(_iota((n, n), 0), sh) == jnp.right_shift(_iota((n, n), 1), sh)).astype(F32)


def _rope(x, c, sa, sb, half):
    n = x.shape[-1]
    return x * c + pltpu.roll(x, n - half, 1) * sa + pltpu.roll(x, half, 1) * sb


def _cparams(sem):
    return pltpu.CompilerParams(dimension_semantics=sem, vmem_limit_bytes=VMEM_LIMIT_BYTES)


def _mod_kernel(c_ref, w_ref, b_ref, o_ref):
    s = _silu(c_ref[...])
    o_ref[0] = _mm(s, w_ref[0]) + b_ref[0]


def _modulation(cc, w_mod, b_mod):
    depth, d, n = w_mod.shape
    r = cc.shape[0]
    tn = 2048
    return pl.pallas_call(
        _mod_kernel,
        grid=(depth, n // tn),
        in_specs=[pl.BlockSpec((r, d), lambda l, j: (0, 0)),
                  pl.BlockSpec((1, d, tn), lambda l, j: (l, 0, j)),
                  pl.BlockSpec((1, 1, tn), lambda l, j: (l, 0, j))],
        out_specs=pl.BlockSpec((1, r, tn), lambda l, j: (l, 0, j)),
        out_shape=jax.ShapeDtypeStruct((depth, r, n), F32),
        compiler_params=_cparams(("parallel", "parallel")),
        name="modulation",
    )(cc, w_mod, b_mod.reshape(depth, 1, n))


_P_CQ = (0, 384)
_P_CKV = (384, 640)
_P_KR = (640, 768)
_P_SQ = (768, 1280)
_P_SK = (1280, 1408)
_P_SV = (1408, 1536)
_P_QKV = (1536, 3072)
_P_Z = (3072, 3584)
_P_AB = (3584, 3712)
_P_GATE = (3712, 6784)
_P_WIDTH = 6784


def _proj_kernel(x_ref, mod_ref, g1_ref, win_ref, qn_ref, kvn_ref, wuq_ref, wukvk_ref, wukvv_ref,
                 mc_ref, ma_ref, mb_ref, sc_ref, sa_ref, sb_ref,
                 qm_ref, km_ref, vm_ref, sq_ref, sk_ref, sv_ref, qkv_ref, z_ref, ab_ref, gate_ref):
    x = x_ref[0]
    d = x.shape[-1]
    m = mod_ref[0]
    xb = (_rms(x, g1_ref[...]) * (1.0 + m[:, d:2 * d]) + m[:, 0:d]).astype(BF16)

    def proj(piece):
        return jnp.dot(xb, win_ref[:, piece[0]:piece[1]], preferred_element_type=F32)

    mc, ma, mb = mc_ref[...], ma_ref[...], mb_ref[...]
    sc, sa, sb = sc_ref[...], sa_ref[...], sb_ref[...]

    cq = _rms(proj(_P_CQ), qn_ref[...])
    ckv = _rms(proj(_P_CKV), kvn_ref[...])

    gate_ref[0] = proj(_P_GATE).astype(gate_ref.dtype)
    qkv_ref[0] = proj(_P_QKV)
    z_ref[0] = proj(_P_Z).astype(z_ref.dtype)
    ab_ref[0] = proj(_P_AB)
    sv_ref[0] = proj(_P_SV).astype(sv_ref.dtype)

    q = _mm(cq, wuq_ref[...]) * ((MLA_NOPE + MLA_ROPE) ** -0.5 * LOG2E)
    kn = _mm(ckv, wukvk_ref[...])
    vm_ref[0] = _mm(ckv, wukvv_ref[...]).astype(vm_ref.dtype)
    kr = _rope(proj(_P_KR), mc, ma, mb, MLA_ROPE // 4)
    for h in range(MLA_HEADS):
        sl = slice(h * LANES, (h + 1) * LANES)
        qm_ref[0, :, sl] = _rope(q[:, sl], mc, ma, mb, MLA_ROPE // 4).astype(qm_ref.dtype)
        km_ref[0, :, sl] = (kn[:, sl] + kr).astype(km_ref.dtype)

    sq = proj(_P_SQ) * (SWA_HEAD_DIM ** -0.5 * LOG2E)
    for g in range(SWA_HEADS * SWA_HEAD_DIM // LANES):
        sl = slice(g * LANES, (g + 1) * LANES)
        sq_ref[0, :, sl] = _rope(sq[:, sl], sc, sa, sb, SWA_HEAD_DIM // 4).astype(sq_ref.dtype)
    sk_ref[0] = _rope(proj(_P_SK), sc, sa, sb, SWA_HEAD_DIM // 4).astype(sk_ref.dtype)


def _proj(xs, mod_l, g1, win, qn, kvn, wuq, wukvk, wukvv, rope_tabs, nctt, ctx_row):
    b, t, d = xs.shape
    tm = TOKEN_TILE
    nt = t // tm

    def tok(w):
        return pl.BlockSpec((1, tm, w), lambda i, j: (i, j, 0))

    def full(a):
        return pl.BlockSpec(a.shape, lambda i, j: (0,) * a.ndim)

    tab = pl.BlockSpec((tm, LANES), lambda i, j: (j, 0))
    mod_spec = pl.BlockSpec((1, 1, mod_l.shape[-1]),
                            lambda i, j: (jnp.where(j < nctt, ctx_row, i), 0, 0))
    widths = (1024, 1024, 512, 512, 128, 128, 1536, 512, 128, 3072)
    dtypes = (BF16, BF16, BF16, BF16, BF16, BF16, F32, BF16, F32, BF16)
    return pl.pallas_call(
        _proj_kernel,
        grid=(b, nt),
        in_specs=[tok(d), mod_spec, full(g1), full(win), full(qn), full(kvn), full(wuq),
                  full(wukvk), full(wukvv)] + [tab] * 6,
        out_specs=[tok(w) for w in widths],
        out_shape=[jax.ShapeDtypeStruct((b, t, w), dt) for w, dt in zip(widths, dtypes)],
        compiler_params=_cparams(("parallel", "parallel")),
        name="proj",
    )(xs, mod_l, g1, win, qn, kvn, wuq, wukvk, wukvv, *rope_tabs)


def _mla_kernel(q_ref, k_ref, v_ref, o_ref, *, n_ctx, q_tile, skip_ctx):
    n = pl.program_id(2)

    def run(row0, nq, nk):
        low = _iota((nq, LANES), 1) < MLA_V
        v = v_ref[0, 0:nk, :]
        v_low = _iota((nk, LANES), 1) < MLA_V
        one = jnp.ones_like(v)
        ss = [_mm_nt(q_ref[0, pl.ds(row0, nq), hh * LANES:(hh + 1) * LANES],
                     k_ref[0, 0:nk, hh * LANES:(hh + 1) * LANES]) for hh in range(2)]
        ps = [jnp.exp2(s - jnp.max(s, axis=-1, keepdims=True)) for s in ss]
        outs = []
        for hh in range(2):
            acc = _mm(ps[hh], jnp.where(v_low if hh == 0 else jnp.logical_not(v_low), v, one))
            den = acc[:, MLA_V:MLA_V + 1] if hh == 0 else acc[:, 0:1]
            outs.append(acc / den)
        o_ref[0, pl.ds(row0, nq), :] = jnp.where(low, outs[0], outs[1]).astype(o_ref.dtype)

    @pl.when(n == 0)
    def _():
        if skip_ctx:
            o_ref[0, 0:n_ctx, :] = jnp.zeros((n_ctx, LANES), o_ref.dtype)
        else:
            run(0, n_ctx, n_ctx)

    @pl.when(n > 0)
    def _():
        run(pl.multiple_of(n_ctx + (n - 1) * q_tile, LANES), q_tile, k_ref.shape[1])


def _mla(qm, km, vm, n_ctx, skip_ctx):
    b, t, _ = qm.shape
    q_tile = MLA_Q_TILE
    assert (t - n_ctx) % q_tile == 0
    steps = (t - n_ctx) // q_tile + 1
    return pl.pallas_call(
        functools.partial(_mla_kernel, n_ctx=n_ctx, q_tile=q_tile, skip_ctx=skip_ctx),
        grid=(b, MLA_HEADS // 2, steps),
        in_specs=[pl.BlockSpec((1, t, 2 * LANES), lambda i, j, n: (i, 0, j)),
                  pl.BlockSpec((1, t, 2 * LANES), lambda i, j, n: (i, 0, j)),
                  pl.BlockSpec((1, t, LANES), lambda i, j, n: (i, 0, j))],
        out_specs=pl.BlockSpec((1, t, LANES), lambda i, j, n: (i, 0, j)),
        out_shape=jax.ShapeDtypeStruct((b, t, MLA_HEADS * MLA_V), BF16),
        compiler_params=_cparams(("parallel", "parallel", "arbitrary")),
        name="mla",
    )(qm, km, vm)


def _swa_kernel(sink_ref, q_ref, k_ref, v_ref, o_ref, *, n_ctx, skip_ctx):
    i = pl.program_id(1)
    blk = q_ref.shape[1]
    t = k_ref.shape[1]
    n_lat = t - n_ctx
    nctt = n_ctx // blk
    low = _iota((blk, LANES), 1) < SWA_HEAD_DIM
    n_slab = SWA_HEADS * SWA_HEAD_DIM // LANES
    half = n_slab * blk

    parts, sinks = [], []
    for hh in range(2):
        for g in range(n_slab):
            qs = q_ref[0, :, g * LANES:(g + 1) * LANES]
            parts.append(jnp.where(low if hh == 0 else jnp.logical_not(low), qs, jnp.zeros_like(qs)))
            head = hh * n_slab + g
            sinks.append(jnp.broadcast_to(sink_ref[head:head + 1, 0:1] * LOG2E, (blk, 1)))
    q_all = jnp.concatenate(parts, axis=0)
    sink = jnp.concatenate(sinks, axis=0)

    def with_ones(v, hh):
        v_low = _iota(v.shape, 1) < SWA_HEAD_DIM
        return jnp.where(v_low if hh == 0 else jnp.logical_not(v_low), v, jnp.ones_like(v))

    def attend(loc):
        kc = k_ref[0, 0:n_ctx, :]
        vc = v_ref[0, 0:n_ctx, :]
        s_c = _mm_nt(q_all, kc)
        mx = jnp.maximum(jnp.max(s_c, axis=-1, keepdims=True), sink)
        if loc is not None:
            kl, vl, valid = loc
            s_l = jnp.where(valid, _mm_nt(q_all, kl), NEG_INF)
            mx = jnp.maximum(mx, jnp.max(s_l, axis=-1, keepdims=True))
        p_c = jnp.exp2(s_c - mx).astype(BF16)
        p_sink = jnp.exp2(sink - mx)
        if loc is not None:
            p_l = jnp.exp2(s_l - mx).astype(BF16)
        outs = []
        for hh in range(2):
            rows = slice(hh * half, (hh + 1) * half)
            acc = _mm(p_c[rows], with_ones(vc, hh))
            if loc is not None:
                acc = acc + _mm(p_l[rows], with_ones(vl, hh))
            den = (acc[:, SWA_HEAD_DIM:SWA_HEAD_DIM + 1] if hh == 0 else acc[:, 0:1]) + p_sink[rows]
            outs.append(acc / den)
        for g in range(n_slab):
            rows = slice(g * blk, (g + 1) * blk)
            o_ref[0, :, g * LANES:(g + 1) * LANES] = jnp.where(low, outs[0][rows], outs[1][rows]).astype(o_ref.dtype)

    @pl.when(i < nctt)
    def _():
        if skip_ctx:
            o_ref[...] = jnp.zeros_like(o_ref)
        else:
            attend(None)

    @pl.when(i >= nctt)
    def _():
        n = i - nctt
        span = 3 * blk
        start = jnp.clip((n - 1) * blk, 0, n_lat - span)
        row0 = pl.multiple_of(n_ctx + start, blk)
        kl = k_ref[0, pl.ds(row0, span), :]
        vl = v_ref[0, pl.ds(row0, span), :]
        kpos = start + _iota((blk, span), 1)
        qpos = n * blk + _iota((blk, span), 0)
        valid = jnp.abs(kpos - qpos) <= WINDOW
        attend((kl, vl, jnp.concatenate([valid] * (2 * n_slab), axis=0)))


def _swa(sink, sq, sk, sv, n_ctx, skip_ctx):
    b, t, w = sq.shape
    blk = SWA_TILE
    nt = t // blk
    return pl.pallas_call(
        functools.partial(_swa_kernel, n_ctx=n_ctx, skip_ctx=skip_ctx),
        grid=(b, nt),
        in_specs=[pl.BlockSpec(sink.shape, lambda i, n: (0, 0)),
                  pl.BlockSpec((1, blk, w), lambda i, n: (i, n, 0)),
                  pl.BlockSpec((1, t, LANES), lambda i, n: (i, 0, 0)),
                  pl.BlockSpec((1, t, LANES), lambda i, n: (i, 0, 0))],
        out_specs=pl.BlockSpec((1, blk, w), lambda i, n: (i, n, 0)),
        out_shape=jax.ShapeDtypeStruct((b, t, w), BF16),
        compiler_params=_cparams(("parallel", "parallel")),
        name="swa",
    )(sink, sq, sk, sv)


def _block_diag(x, bd_mask):
    return jnp.where(bd_mask, jnp.concatenate([x] * PACK, axis=0), 0.0)


def _gdn_prep_kernel(qkv_ref, prev_ref, next_ref, ab_ref, cw_ref, gp_ref,
                     a_ref, rv_ref, rk_ref, qk_ref, qe_ref, ke_ref, dl_ref, *, seg_starts, seg_ends):
    ti = pl.program_id(1)
    tm = qkv_ref.shape[1]
    hw = GDN_HEADS * GDN_DK
    x = qkv_ref[0]
    row = _iota((tm, 1), 0)
    is_start = functools.reduce(jnp.logical_or, [ti == s for s in seg_starts])
    is_end = functools.reduce(jnp.logical_or, [ti == s for s in seg_ends])
    prev_row = jnp.where(is_start, 0.0, prev_ref[0, SUBLANES - 1:SUBLANES, :])
    next_row = jnp.where(is_end, 0.0, next_ref[0, 0:1, :])
    x_prev = jnp.where(row == 0, prev_row, pltpu.roll(x, 1, 0))
    x_next = jnp.where(row == tm - 1, next_row, pltpu.roll(x, tm - 1, 0))
    y = _silu(x_prev * cw_ref[0:1, :] + x * cw_ref[1:2, :] + x_next * cw_ref[2:3, :])

    ones_blk = _group_ones(LANES, GDN_DK).astype(BF16)

    def l2(slab):
        return slab * lax.rsqrt(_group_sum(slab * slab, ones_blk) + EPS)

    q = jnp.concatenate([l2(y[:, s * LANES:(s + 1) * LANES]) for s in range(hw // LANES)], axis=1)
    q = q * (GDN_DK ** -0.5)
    k = jnp.concatenate([l2(y[:, hw + s * LANES:hw + (s + 1) * LANES]) for s in range(hw // LANES)], axis=1)
    v = y[:, 2 * hw:3 * hw]

    ab = ab_ref[0]
    g_all = -jnp.exp(gp_ref[0:1, :]) * _softplus(ab + gp_ref[1:2, :])
    beta_all = _sigmoid(ab)
    ri = _iota((tm, tm), 0)
    ci = _iota((tm, tm), 1)
    same = jnp.right_shift(ri, 6) == jnp.right_shift(ci, 6)
    g_parts = jnp.concatenate(_split_bf16(g_all, 3), axis=1)

    def cumsum(tri):
        r = jnp.dot(jnp.logical_and(same, tri).astype(BF16), g_parts, preferred_element_type=F32)
        return r[:, 0:LANES] + r[:, LANES:2 * LANES] + r[:, 2 * LANES:3 * LANES]

    gam_all = jnp.where(_iota((tm, LANES), 1) < GDN_HEADS, cumsum(ci <= ri), cumsum(ci >= ri))
    col_head = jnp.right_shift(_iota((LANES, 2 * hw), 1), GDN_DK.bit_length() - 1)
    src = _iota((LANES, 2 * hw), 0)

    def expand(x, parts, sel):
        r = jnp.dot(jnp.concatenate(_split_bf16(x, parts), axis=0), sel.astype(BF16), preferred_element_type=F32)
        return functools.reduce(lambda a, b: a + b, [r[p * tm:(p + 1) * tm] for p in range(parts)])

    gam_exp = expand(gam_all, 3, src == col_head)
    b_exp = expand(beta_all, 2, src == col_head + 2 * GDN_HEADS)
    gam_f, gam_b = gam_exp[:, 0:hw], gam_exp[:, hw:2 * hw]

    c = GDN_CHUNK
    r64 = _iota((c, PACK_W), 0)
    c64 = jnp.bitwise_and(_iota((c, PACK_W), 1), c - 1)
    eye = (r64 == c64).astype(F32)
    bd_mask = jnp.right_shift(_iota((PACK_W, PACK_W), 0), 6) == jnp.right_shift(_iota((PACK_W, PACK_W), 1), 6)

    for ch in range(tm // c):
        rows = slice(ch * c, (ch + 1) * c)
        for gi in range(hw // PACK_W):
            lanes = slice(gi * PACK_W, (gi + 1) * PACK_W)
            kc, qc, vc = k[rows, lanes], q[rows, lanes], v[rows, lanes]
            kbd = _block_diag(kc, bd_mask)
            kq = _mm_nt(jnp.concatenate([kc, qc], axis=0), kbd)
            kk, qkm = kq[0:c], kq[c:2 * c]
            for d in range(2):
                gam = (gam_f if d == 0 else gam_b)[rows, lanes]
                beta = b_exp[rows, d * hw + gi * PACK_W:d * hw + (gi + 1) * PACK_W]
                gam_row = jnp.sum(gam * eye, axis=0, keepdims=True)
                decay = jnp.exp(jnp.minimum(gam - gam_row, 0.0))
                strict = (r64 > c64) if d == 0 else (r64 < c64)
                incl = (r64 >= c64) if d == 0 else (r64 <= c64)
                a_mat = jnp.where(strict, beta * kk * decay, 0.0)
                for pp in range(PACK_W // LANES):
                    a_ref[d, ch * (hw // LANES) + gi * (PACK_W // LANES) + pp] = \
                        a_mat[:, pp * LANES:(pp + 1) * LANES]
                e_gam = jnp.exp(gam)
                g_last = gam[c - 1:c, :] if d == 0 else gam[0:1, :]
                out = slice(d * hw + gi * PACK_W, d * hw + (gi + 1) * PACK_W)
                rv_ref[0, rows, out] = (vc * beta).astype(rv_ref.dtype)
                rk_ref[0, rows, out] = (kc * beta * e_gam).astype(rk_ref.dtype)
                qk_ref[0, rows, out] = jnp.where(incl, qkm * decay, 0.0).astype(qk_ref.dtype)
                qe_ref[0, rows, out] = (qc * e_gam).astype(qe_ref.dtype)
                ke_ref[0, rows, out] = (kc * jnp.exp(g_last - gam)).astype(ke_ref.dtype)
                dl_ref[0, ch, :, out] = jnp.exp(g_last)


def _gdn_prep(qkv, ab, cw, gp, seg_starts, seg_ends):
    b, t, w = qkv.shape
    tm = TOKEN_TILE
    nt = t // tm
    hb = tm // SUBLANES
    nh = t // SUBLANES
    wide = 2 * GDN_HEADS * GDN_DK
    c = GDN_CHUNK
    cpt = tm // c
    spt = cpt * (GDN_HEADS * GDN_DK // LANES)
    tok = lambda width: pl.BlockSpec((1, tm, width), lambda i, j: (i, j, 0))
    out_shape = [jax.ShapeDtypeStruct((2, b * nt * spt, c, LANES), F32)] + \
                [jax.ShapeDtypeStruct((b, t, wide), BF16)] * 5 + \
                [jax.ShapeDtypeStruct((b, t // c, 1, wide), F32)]
    return pl.pallas_call(
        functools.partial(_gdn_prep_kernel, seg_starts=seg_starts, seg_ends=seg_ends),
        grid=(b, nt),
        in_specs=[tok(w),
                  pl.BlockSpec((1, SUBLANES, w), lambda i, j: (i, jnp.maximum(j * hb - 1, 0), 0)),
                  pl.BlockSpec((1, SUBLANES, w), lambda i, j: (i, jnp.minimum((j + 1) * hb, nh - 1), 0)),
                  tok(LANES),
                  pl.BlockSpec(cw.shape, lambda i, j: (0, 0)),
                  pl.BlockSpec(gp.shape, lambda i, j: (0, 0))],
        out_specs=[pl.BlockSpec((2, spt, c, LANES), lambda i, j: (0, i * nt + j, 0, 0))] +
                  [tok(wide)] * 5 + [pl.BlockSpec((1, cpt, 1, wide), lambda i, j: (i, j, 0, 0))],
        out_shape=out_shape,
        compiler_params=_cparams(("parallel", "parallel")),
        name="gdn_prep",
    )(qkv, qkv, qkv, ab, cw, gp)


def _gdn_solve_kernel(a_ref, o_ref, at_ref, x_ref, acc_ref, ot_ref):
    c = GDN_CHUNK
    sub = SUBLANES
    nb = c // sub
    a_rows = pltpu.einshape("sil->isl", a_ref[0])
    for i in range(c):
        at_ref[i * LANES:(i + 1) * LANES, :] = a_rows[i].T
    x_ref[...] = jnp.zeros_like(x_ref)
    row_iota = _iota((LANES, LANES), 0)

    def solve(upper):
        def row(step, carry):
            i = (c - 1 - step) if upper else step
            base = pl.multiple_of(i * LANES, LANES)
            acc_ref[...] = (jnp.bitwise_and(row_iota, c - 1) == i).astype(F32)
            for kb in range(nb):
                active = (kb * sub + sub - 1 > i) if upper else (kb * sub < i)

                @pl.when(active)
                def _():
                    for h2 in range(LANES // c):
                        coefs = [at_ref[pl.ds(base + h2 * c + kb * sub + kk, 1), :] for kk in range(sub)]
                        for jb in (range(kb, nb) if upper else range(kb + 1)):
                            dst = slice(h2 * c + jb * sub, h2 * c + (jb + 1) * sub)
                            acc = acc_ref[dst, :]
                            for kk in range(sub):
                                lo = (kb * sub + kk) * LANES + h2 * c + jb * sub
                                acc = acc - coefs[kk] * x_ref[lo:lo + sub, :]
                            acc_ref[dst, :] = acc
            x_ref[pl.ds(base, LANES), :] = acc_ref[...]
            return carry

        lax.fori_loop(0, c, row, 0)

    @pl.when(pl.program_id(0) == 0)
    def _():
        solve(False)

    @pl.when(pl.program_id(0) == 1)
    def _():
        solve(True)

    for i in range(c):
        ot_ref[i] = x_ref[i * LANES:(i + 1) * LANES, :].T
    o_ref[0] = pltpu.einshape("isl->sil", ot_ref[...])


def _gdn_solve(a):
    _, n_slab, c, _ = a.shape
    assert n_slab % LANES == 0
    blk = pl.BlockSpec((1, LANES, c, LANES), lambda d, g: (d, g, 0, 0))
    return pl.pallas_call(
        _gdn_solve_kernel,
        grid=(2, n_slab // LANES),
        in_specs=[blk],
        out_specs=blk,
        out_shape=jax.ShapeDtypeStruct(a.shape, F32),
        scratch_shapes=[pltpu.VMEM((c * LANES, LANES), F32), pltpu.VMEM((c * LANES, LANES), F32),
                        pltpu.VMEM((LANES, LANES), F32), pltpu.VMEM((c, LANES, LANES), F32)],
        compiler_params=_cparams(("parallel", "parallel")),
        name="gdn_solve",
    )(a)


def _gdn_apply_kernel(t_ref, rv_ref, rk_ref, u_ref, w_ref):
    tm = rv_ref.shape[1]
    c = GDN_CHUNK
    hw = GDN_HEADS * GDN_DK
    ppg = PACK_W // LANES
    bd_mask = jnp.right_shift(_iota((PACK_W, PACK_W), 0), 6) == jnp.right_shift(_iota((PACK_W, PACK_W), 1), 6)
    for ch in range(tm // c):
        rows = slice(ch * c, (ch + 1) * c)
        for gi in range(hw // PACK_W):
            for d in range(2):
                out = slice(d * hw + gi * PACK_W, d * hw + (gi + 1) * PACK_W)
                tinv = jnp.concatenate([t_ref[d, ch * (hw // LANES) + gi * ppg + pp] for pp in range(ppg)],
                                       axis=1)
                t16 = tinv.astype(BF16)
                u_ref[0, rows, out] = jnp.dot(t16, _block_diag(rv_ref[0, rows, out], bd_mask),
                                              preferred_element_type=F32)
                w_ref[0, rows, out] = jnp.dot(t16, _block_diag(rk_ref[0, rows, out], bd_mask),
                                              preferred_element_type=F32).astype(w_ref.dtype)


def _gdn_apply(tinv, rv, rk):
    b, t, wide = rv.shape
    tm = TOKEN_TILE
    c = GDN_CHUNK
    nt = t // tm
    spt = (tm // c) * (GDN_HEADS * GDN_DK // LANES)
    tok = pl.BlockSpec((1, tm, wide), lambda i, j: (i, j, 0))
    return pl.pallas_call(
        _gdn_apply_kernel,
        grid=(b, nt),
        in_specs=[pl.BlockSpec((2, spt, c, LANES), lambda i, j: (0, i * nt + j, 0, 0)), tok, tok],
        out_specs=[tok, tok],
        out_shape=[jax.ShapeDtypeStruct((b, t, wide), F32), jax.ShapeDtypeStruct((b, t, wide), BF16)],
        compiler_params=_cparams(("parallel", "parallel")),
        name="gdn_apply",
    )(tinv, rv, rk)


def _gdn_scan_kernel(uf, wf, qkf, qef, kef, dlf, ub, wb, qkb, qeb, keb, dlb, of_ref, ob_ref, s_ref):
    c = GDN_CHUNK
    cpt = uf.shape[1] // c
    ng = uf.shape[2] // PACK_W
    bd_mask = jnp.right_shift(_iota((PACK_W, PACK_W), 0), 6) == jnp.right_shift(_iota((PACK_W, PACK_W), 1), 6)

    @pl.when(pl.program_id(1) == 0)
    def _():
        s_ref[...] = jnp.zeros_like(s_ref)

    chains = ((uf, wf, qkf, qef, kef, dlf, of_ref), (ub, wb, qkb, qeb, keb, dlb, ob_ref))
    for j in range(cpt):
        where = []
        for d in range(2):
            ch = j if d == 0 else cpt - 1 - j
            for gi in range(ng):
                where.append((d, gi, ch, slice(ch * c, (ch + 1) * c), slice(gi * PACK_W, (gi + 1) * PACK_W)))
        states = [s_ref[d * ng + gi] for d, gi, _, _, _ in where]
        wqs = [_mm(jnp.concatenate([chains[d][1][0, rows, lanes], chains[d][3][0, rows, lanes]], axis=0), st)
               for (d, gi, ch, rows, lanes), st in zip(where, states)]
        v_news = [chains[d][0][0, rows, lanes] - wq[0:c] for (d, gi, ch, rows, lanes), wq in zip(where, wqs)]
        kvs = [_mm_tn(chains[d][4][0, rows, lanes], v_new) for (d, gi, ch, rows, lanes), v_new in zip(where, v_news)]
        for (d, gi, ch, rows, lanes), st, kv in zip(where, states, kvs):
            s_ref[d * ng + gi] = st * chains[d][5][0, ch, :, lanes] + jnp.where(bd_mask, kv, 0.0)
        for (d, gi, ch, rows, lanes), wq, v_new in zip(where, wqs, v_news):
            chains[d][6][0, rows, lanes] = wq[c:2 * c] + _mm(chains[d][2][0, rows, lanes],
                                                             _block_diag(v_new, bd_mask))


def _gdn_scan(u, w, qk, qe, ke, dl, n_ctx):
    b, t, wide = u.shape
    tm = TOKEN_TILE
    nt = t // tm
    nctt = n_ctx // tm
    hw = wide // 2
    cpt = tm // GDN_CHUNK

    def mirror(g):
        return jnp.where(g < nctt, nctt - 1 - g, nt - 1 - (g - nctt))

    fwd = pl.BlockSpec((1, tm, hw), lambda i, g: (i, g, 0))
    bwd = pl.BlockSpec((1, tm, hw), lambda i, g: (i, mirror(g), 1))
    dfwd = pl.BlockSpec((1, cpt, 1, hw), lambda i, g: (i, g, 0, 0))
    dbwd = pl.BlockSpec((1, cpt, 1, hw), lambda i, g: (i, mirror(g), 0, 1))
    out_sds = jax.ShapeDtypeStruct((b, t, hw), F32)
    return pl.pallas_call(
        _gdn_scan_kernel,
        grid=(b, nt),
        in_specs=[fwd] * 5 + [dfwd] + [bwd] * 5 + [dbwd],
        out_specs=[pl.BlockSpec((1, tm, hw), lambda i, g: (i, g, 0)),
                   pl.BlockSpec((1, tm, hw), lambda i, g: (i, mirror(g), 0))],
        out_shape=[out_sds, out_sds],
        scratch_shapes=[pltpu.VMEM((2 * hw // PACK_W, PACK_W, PACK_W), F32)],
        compiler_params=_cparams(("parallel", "arbitrary")),
        name="gdn_scan",
    )(u, w, qk, qe, ke, dl, u, w, qk, qe, ke, dl)


def _merge_kernel(x_ref, mod_ref, ya_ref, yb_ref, of_ref, ob_ref, z_ref, gate_ref, on_ref,
                  wpa_ref, wpb_ref, wpc_ref, wout_ref, o_ref):
    x = x_ref[0]
    d = x.shape[-1]
    m = mod_ref[0]
    oc = of_ref[0] + ob_ref[0]
    ones_blk = _group_ones(LANES, GDN_DV).astype(BF16)
    ms = jnp.concatenate(
        [_group_sum(oc[:, s * LANES:(s + 1) * LANES] ** 2, ones_blk) for s in range(oc.shape[-1] // LANES)],
        axis=1) * (1.0 / GDN_DV)
    yc = oc * lax.rsqrt(ms + EPS) * on_ref[...] * _silu(z_ref[0].astype(F32))
    gate = gate_ref[0].astype(F32)
    mix = (_sigmoid(gate[:, 0:d]) * _mm(ya_ref[0], wpa_ref[...])
           + _sigmoid(gate[:, d:2 * d]) * _mm(yb_ref[0], wpb_ref[...])
           + _sigmoid(gate[:, 2 * d:3 * d]) * _mm(yc, wpc_ref[...]))
    o_ref[0] = x + m[:, 2 * d:3 * d] * _mm(mix, wout_ref[...])


def _merge(xs, mod_l, ya, yb, oc_f, oc_b, z, gate, onorm, wpa, wpb, wpc, wout, nctt, ctx_row, t_off):
    b, t, d = xs.shape
    tm = TOKEN_TILE
    nt = t // tm - t_off
    tok = lambda width: pl.BlockSpec((1, tm, width), lambda i, j: (i, j + t_off, 0))
    full = lambda a: pl.BlockSpec(a.shape, lambda i, j: (0,) * a.ndim)
    mod_spec = pl.BlockSpec((1, 1, mod_l.shape[-1]),
                            lambda i, j: (jnp.where(j + t_off < nctt, ctx_row, i), 0, 0))
    return pl.pallas_call(
        _merge_kernel,
        grid=(b, nt),
        in_specs=[tok(d), mod_spec, tok(512), tok(512), tok(512), tok(512), tok(512), tok(3 * d),
                  full(onorm), full(wpa), full(wpb), full(wpc), full(wout)],
        out_specs=pl.BlockSpec((1, tm, d), lambda i, j: (i, j, 0)),
        out_shape=jax.ShapeDtypeStruct((b, nt * tm, d), F32),
        compiler_params=_cparams(("parallel", "parallel")),
        name="merge",
    )(xs, mod_l, ya, yb, oc_f, oc_b, z, gate, onorm, wpa, wpb, wpc, wout)


def _ffn_kernel(x_ref, prev_ref, next_ref, mod_ref, g2_ref, wup_ref, cw_ref, wdn_ref, nf_ref, o_ref,
                *, seg_starts, seg_ends, final):
    ti = pl.program_id(1)
    x = x_ref[0]
    tm, d = x.shape
    m = mod_ref[0]
    xall = jnp.concatenate([prev_ref[0], x, next_ref[0]], axis=0)
    xa = _rms(xall, g2_ref[...]) * (1.0 + m[:, 4 * d:5 * d]) + m[:, 3 * d:4 * d]
    n_all = tm + 2 * SUBLANES
    row = _iota((n_all, 1), 0)
    is_start = functools.reduce(jnp.logical_or, [ti == s for s in seg_starts])
    is_end = functools.reduce(jnp.logical_or, [ti == s for s in seg_ends])
    drop = jnp.logical_or(jnp.logical_and(row == SUBLANES - 1, is_start),
                          jnp.logical_and(row == SUBLANES + tm, is_end))
    xa = jnp.where(drop, 0.0, xa).astype(BF16)

    def conv(h, cols):
        hp = pltpu.roll(h, 1, 0)[SUBLANES:SUBLANES + tm]
        hn = pltpu.roll(h, n_all - 1, 0)[SUBLANES:SUBLANES + tm]
        return (hp * cw_ref[0:1, cols] + h[SUBLANES:SUBLANES + tm] * cw_ref[1:2, cols]
                + hn * cw_ref[2:3, cols])

    cwid = FFN_DIM // FFN_SPLIT
    acc = jnp.zeros((tm, d), F32)
    for j in range(FFN_SPLIT):
        ca = slice(j * cwid, (j + 1) * cwid)
        cb = slice(FFN_DIM + j * cwid, FFN_DIM + (j + 1) * cwid)
        ha = conv(jnp.dot(xa, wup_ref[:, ca], preferred_element_type=F32), ca)
        hb = conv(jnp.dot(xa, wup_ref[:, cb], preferred_element_type=F32), cb)
        acc = acc + _mm(_silu(ha) * hb, wdn_ref[ca, :])
    y = x + m[:, 5 * d:6 * d] * acc
    if final:
        y = _rms(y, nf_ref[...])
    o_ref[0] = y


def _ffn(x1, mod_l, g2, wup, cw, wdn, nf, nctt_mod, ctx_row, seg_starts, seg_ends, final):
    b, t, d = x1.shape
    tm = TOKEN_TILE
    nt = t // tm
    hb = tm // SUBLANES
    nh = t // SUBLANES
    full = lambda a: pl.BlockSpec(a.shape, lambda i, j: (0,) * a.ndim)
    mod_spec = pl.BlockSpec((1, 1, mod_l.shape[-1]),
                            lambda i, j: (jnp.where(j < nctt_mod, ctx_row, i), 0, 0))
    return pl.pallas_call(
        functools.partial(_ffn_kernel, seg_starts=seg_starts, seg_ends=seg_ends, final=final),
        grid=(b, nt),
        in_specs=[pl.BlockSpec((1, tm, d), lambda i, j: (i, j, 0)),
                  pl.BlockSpec((1, SUBLANES, d), lambda i, j: (i, jnp.maximum(j * hb - 1, 0), 0)),
                  pl.BlockSpec((1, SUBLANES, d), lambda i, j: (i, jnp.minimum((j + 1) * hb, nh - 1), 0)),
                  mod_spec, full(g2), full(wup), full(cw), full(wdn), full(nf)],
        out_specs=pl.BlockSpec((1, tm, d), lambda i, j: (i, j, 0)),
        out_shape=jax.ShapeDtypeStruct((b, t, d), F32),
        compiler_params=_cparams(("parallel", "parallel")),
        name="ffn",
    )(x1, x1, x1, mod_l, g2, wup, cw, wdn, nf)


def _cols(w, segments):
    parts = [jnp.zeros(w.shape[:-1] + (seg,), w.dtype) if isinstance(seg, int) else w[..., seg[0]:seg[1]]
             for seg in segments]
    return jnp.concatenate(parts, axis=-1)


def _win_segments(d):
    widths = [MLA_Q_RANK, MLA_KV_RANK, MLA_ROPE, SWA_HEADS * SWA_HEAD_DIM, SWA_KV_HEADS * SWA_HEAD_DIM,
              SWA_KV_HEADS * SWA_HEAD_DIM, GDN_HEADS * (2 * GDN_DK + GDN_DV), GDN_HEADS * GDN_DV,
              2 * GDN_HEADS, 2 * GDN_HEADS, 3 * d]
    off = np.cumsum([0] + widths)
    cq, ckv, kr, sq, sk, sv, qkv, z, a, b, gate = [(int(off[i]), int(off[i + 1])) for i in range(11)]
    sq_perm = [(sq[0] + h * SWA_HEAD_DIM, sq[0] + (h + 1) * SWA_HEAD_DIM) for h in _swa_head_order()]
    segs = [cq, ckv, MLA_NOPE, kr, LANES - MLA_NOPE - MLA_ROPE] + sq_perm + \
           [sk, sv, qkv, z, a, b, LANES - 4 * GDN_HEADS, gate]
    return segs


def _swa_head_order():
    group = SWA_HEADS // SWA_KV_HEADS
    return [kv * group + g for g in range(group) for kv in range(SWA_KV_HEADS)]


def _rope_tables(n_ctx, n_lat):
    t = np.arange(n_lat)
    pos = np.stack([t // GRID_W, t % GRID_W]).astype(np.float32)

    def build(head_w, rope_lo, rope_w):
        da = rope_w // 2
        half = da // 2
        inv = ROPE_THETA ** (-np.arange(half, dtype=np.float32) / half)
        c = np.ones((n_lat, LANES), np.float32)
        sa = np.zeros((n_lat, LANES), np.float32)
        sb = np.zeros((n_lat, LANES), np.float32)
        for lane in range(LANES):
            dd = lane % head_w - rope_lo
            if dd < 0 or dd >= rope_w:
                continue
            axis, r = dd // da, dd % da
            ang = pos[axis] * inv[r % half]
            c[:, lane] = np.cos(ang)
            if r < half:
                sa[:, lane] = -np.sin(ang)
            else:
                sb[:, lane] = np.sin(ang)
        ctx = [np.ones((n_ctx, LANES), np.float32), np.zeros((n_ctx, LANES), np.float32),
               np.zeros((n_ctx, LANES), np.float32)]
        return [jnp.asarray(np.concatenate([cx, lt])) for cx, lt in zip(ctx, (c, sa, sb))]

    return build(LANES, MLA_NOPE, MLA_ROPE) + build(SWA_HEAD_DIM, 0, SWA_HEAD_DIM)


def kernel(x, c, ctx, c_ctx, w_mod, b_mod, norm1, norm2, w_in, mla_q_norm, mla_kv_norm, w_uq, w_ukv,
           swa_sink, gdn_conv, gdn_a_log, gdn_dt_bias, gdn_norm, w_branch_a, w_branch_b, w_branch_c,
           w_out, ffn_up, ffn_conv, ffn_down, norm_f):
    b, s, d = x.shape
    n_ctx = ctx.shape[1]
    depth = w_mod.shape[0]
    tm = TOKEN_TILE
    assert n_ctx % tm == 0 and s % tm == 0 and s >= 3 * SWA_TILE
    nctt = n_ctx // tm
    ntt = (n_ctx + s) // tm

    rows = -(-(b + 1) // SUBLANES) * SUBLANES
    cc = jnp.zeros((rows, d), F32).at[0:b].set(c).at[b].set(c_ctx)
    mod = _modulation(cc, w_mod, b_mod).reshape(depth, rows, 1, 6 * d)

    hd_q = MLA_NOPE + MLA_ROPE
    hd_kv = MLA_NOPE + MLA_V
    uq_segs = [s for h in range(MLA_HEADS) for s in ((h * hd_q, (h + 1) * hd_q), LANES - hd_q)]
    uk_segs = [s for h in range(MLA_HEADS) for s in ((h * hd_kv, h * hd_kv + MLA_NOPE), LANES - MLA_NOPE)]
    uv_segs = [(h * hd_kv + MLA_NOPE, (h + 1) * hd_kv) for h in range(MLA_HEADS)]
    win_segs = _win_segments(d)

    def layer_weights(l):
        wpb = jnp.concatenate([w_branch_b[l, h * SWA_HEAD_DIM:(h + 1) * SWA_HEAD_DIM] for h in _swa_head_order()],
                              axis=0)
        gpar = jnp.zeros((SUBLANES, LANES), F32)
        gpar = gpar.at[0, 0:2 * GDN_HEADS].set(gdn_a_log[l].reshape(-1))
        gpar = gpar.at[1, 0:2 * GDN_HEADS].set(gdn_dt_bias[l].reshape(-1))
        pad_rows = lambda a: jnp.pad(a, ((0, SUBLANES - a.shape[0]), (0, 0)))
        return dict(
            win=_cols(w_in[l], win_segs).astype(BF16),
            wuq=_cols(w_uq[l], uq_segs).astype(BF16),
            wukvk=_cols(w_ukv[l], uk_segs).astype(BF16),
            wukvv=_cols(w_ukv[l], uv_segs).astype(BF16),
            wpa=w_branch_a[l].astype(BF16), wpb=wpb.astype(BF16), wpc=w_branch_c[l].astype(BF16),
            wout=w_out[l].astype(BF16), wup=ffn_up[l].astype(BF16), wdn=ffn_down[l].astype(BF16),
            sink=jnp.broadcast_to(swa_sink[l][:, None], (SWA_HEADS, LANES)),
            onorm=jnp.tile(gdn_norm[l], GDN_HEADS).reshape(1, -1),
            gconv=pad_rows(gdn_conv[l]), fconv=pad_rows(ffn_conv[l]), gpar=gpar)

    rope_tabs = _rope_tables(n_ctx, s)

    xs = jnp.concatenate([ctx, x], axis=1)
    row2 = lambda a: a.reshape(1, -1)
    for l in range(depth):
        last = l == depth - 1
        t_off = nctt if last else 0
        p = layer_weights(l)
        qm, km, vm, sq, sk, sv, qkv, z, ab, gate = _proj(
            xs, mod[l], row2(norm1[l]), p["win"], row2(mla_q_norm[l]), row2(mla_kv_norm[l]),
            p["wuq"], p["wukvk"], p["wukvv"], rope_tabs, nctt, b)
        ya = _mla(qm, km, vm, n_ctx, last)
        yb = _swa(p["sink"], sq, sk, sv, n_ctx, last)
        a_mat, rv, rk, qk, qe, ke, dl = _gdn_prep(qkv, ab, p["gconv"], p["gpar"], (0, nctt), (nctt - 1, ntt - 1))
        u, w = _gdn_apply(_gdn_solve(a_mat), rv, rk)
        oc_f, oc_b = _gdn_scan(u, w, qk, qe, ke, dl, n_ctx)
        x1 = _merge(xs, mod[l], ya, yb, oc_f, oc_b, z, gate, p["onorm"], p["wpa"], p["wpb"], p["wpc"], p["wout"],
                    nctt, b, t_off)
        if last:
            xs = _ffn(x1, mod[l], row2(norm2[l]), p["wup"], p["fconv"], p["wdn"], row2(norm_f),
                      0, b, (0,), (s // tm - 1,), True)
        else:
            xs = _ffn(x1, mod[l], row2(norm2[l]), p["wup"], p["fconv"], p["wdn"], row2(norm_f),
                      nctt, b, (0, nctt), (nctt - 1, ntt - 1), False)
    return xs
```

```python
import functools

import numpy as np
import jax
import jax.numpy as jnp
from jax import lax
from jax.experimental import pallas as pl
from jax.experimental.pallas import tpu as pltpu

F32 = jnp.float32
BF16 = jnp.bfloat16

GRID_W = 64
ROPE_THETA = 10000.0
NEG_INF = -1e30
EPS = 1e-6
LOG2E = 1.4426950408889634

MLA_HEADS = 8
MLA_Q_RANK = 384
MLA_KV_RANK = 256
MLA_NOPE = 64
MLA_ROPE = 32
MLA_V = 64

SWA_HEADS = 8
SWA_KV_HEADS = 2
SWA_HEAD_DIM = 64
WINDOW = 128

GDN_HEADS = 8
GDN_DK = 64
GDN_DV = 64
GDN_CHUNK = 64

FFN_DIM = 2816

LANES = 128
SUBLANES = 8
VMEM_LIMIT_BYTES = 56 * 1024 * 1024

TOKEN_TILE = 256
SWA_TILE = 128
MLA_Q_TILE = 1024
PACK = 4
PACK_W = PACK * GDN_DK
FFN_SPLIT = 1


def _sigmoid(x):
    return 0.5 * jnp.tanh(0.5 * x) + 0.5


def _silu(x):
    return x * _sigmoid(x)


def _softplus(x):
    return jnp.maximum(x, 0.0) + jnp.log(1.0 + jnp.exp(-jnp.abs(x)))


def _rms(x, g):
    return x * lax.rsqrt(jnp.mean(x * x, axis=-1, keepdims=True) + EPS) * g


def _mm(a, b):
    return jnp.dot(a.astype(BF16), b.astype(BF16), preferred_element_type=F32)


def _mm_nt(a, b):
    return lax.dot_general(a.astype(BF16), b.astype(BF16), (((1,), (1,)), ((), ())),
                           preferred_element_type=F32)


def _mm_tn(a, b):
    return lax.dot_general(a.astype(BF16), b.astype(BF16), (((0,), (0,)), ((), ())),
                           preferred_element_type=F32)


def _split_bf16(x, parts):
    out = []
    for _ in range(parts - 1):
        hi = x.astype(BF16)
        out.append(hi)
        x = x - hi.astype(F32)
    out.append(x.astype(BF16))
    return out


def _group_sum(x, ones_blk):
    n = x.shape[0]
    r = jnp.dot(jnp.concatenate(_split_bf16(x, 2), axis=0), ones_blk, preferred_element_type=F32)
    return r[0:n] + r[n:2 * n]


def _iota(shape, dim):
    return lax.broadcasted_iota(jnp.int32, shape, dim)


def _group_ones(n, group):
    sh = group.bit_length() - 1
    return (jnp.right_shift(_iota((n, n), 0), sh) == jnp.right_shift(_iota((n, n), 1), sh)).astype(F32)


def _rope(x, c, sa, sb, half):
    n = x.shape[-1]
    return x * c + pltpu.roll(x, n - half, 1) * sa + pltpu.roll(x, half, 1) * sb


def _cparams(sem):
    return pltpu.CompilerParams(dimension_semantics=sem, vmem_limit_bytes=VMEM_LIMIT_BYTES)


def _mod_kernel(c_ref, w_ref, b_ref, o_ref):
    s = _silu(c_ref[...])
    o_ref[0] = _mm(s, w_ref[0]) + b_ref[0]


def _modulation(cc, w_mod, b_mod):
    depth, d, n = w_mod.shape
    r = cc.shape[0]
    tn = 2048
    return pl.pallas_call(
        _mod_kernel,
        grid=(depth, n // tn),
        in_specs=[pl.BlockSpec((r, d), lambda l, j: (0, 0)),
                  pl.BlockSpec((1, d, tn), lambda l, j: (l, 0, j)),
                  pl.BlockSpec((1, 1, tn), lambda l, j: (l, 0, j))],
        out_specs=pl.BlockSpec((1, r, tn), lambda l, j: (l, 0, j)),
        out_shape=jax.ShapeDtypeStruct((depth, r, n), F32),
        compiler_params=_cparams(("parallel", "parallel")),
        name="modulation",
    )(cc, w_mod, b_mod.reshape(depth, 1, n))


_P_CQ = (0, 384)
_P_CKV = (384, 640)
_P_KR = (640, 768)
_P_SQ = (768, 1280)
_P_SK = (1280, 1408)
_P_SV = (1408, 1536)
_P_QKV = (1536, 3072)
_P_Z = (3072, 3584)
_P_AB = (3584, 3712)
_P_GATE = (3712, 6784)
_P_WIDTH = 6784


def _proj_kernel(x_ref, mod_ref, g1_ref, win_ref, qn_ref, kvn_ref, wuq_ref, wukvk_ref, wukvv_ref,
                 mc_ref, ma_ref, mb_ref, sc_ref, sa_ref, sb_ref,
                 qm_ref, km_ref, vm_ref, sq_ref, sk_ref, sv_ref, qkv_ref, z_ref, ab_ref, gate_ref):
    x = x_ref[0]
    d = x.shape[-1]
    m = mod_ref[0]
    xb = (_rms(x, g1_ref[...]) * (1.0 + m[:, d:2 * d]) + m[:, 0:d]).astype(BF16)

    def proj(piece):
        return jnp.dot(xb, win_ref[:, piece[0]:piece[1]], preferred_element_type=F32)

    mc, ma, mb = mc_ref[...], ma_ref[...], mb_ref[...]
    sc, sa, sb = sc_ref[...], sa_ref[...], sb_ref[...]

    cq = _rms(proj(_P_CQ), qn_ref[...])
    ckv = _rms(proj(_P_CKV), kvn_ref[...])

    gate_ref[0] = proj(_P_GATE).astype(gate_ref.dtype)
    qkv_ref[0] = proj(_P_QKV)
    z_ref[0] = proj(_P_Z).astype(z_ref.dtype)
    ab_ref[0] = proj(_P_AB)
    sv_ref[0] = proj(_P_SV).astype(sv_ref.dtype)

    q = _mm(cq, wuq_ref[...]) * ((MLA_NOPE + MLA_ROPE) ** -0.5 * LOG2E)
    kn = _mm(ckv, wukvk_ref[...])
    vm_ref[0] = _mm(ckv, wukvv_ref[...]).astype(vm_ref.dtype)
    kr = _rope(proj(_P_KR), mc, ma, mb, MLA_ROPE // 4)
    for h in range(MLA_HEADS):
        sl = slice(h * LANES, (h + 1) * LANES)
        qm_ref[0, :, sl] = _rope(q[:, sl], mc, ma, mb, MLA_ROPE // 4).astype(qm_ref.dtype)
        km_ref[0, :, sl] = (kn[:, sl] + kr).astype(km_ref.dtype)

    sq = proj(_P_SQ) * (SWA_HEAD_DIM ** -0.5 * LOG2E)
    for g in range(SWA_HEADS * SWA_HEAD_DIM // LANES):
        sl = slice(g * LANES, (g + 1) * LANES)
        sq_ref[0, :, sl] = _rope(sq[:, sl], sc, sa, sb, SWA_HEAD_DIM // 4).astype(sq_ref.dtype)
    sk_ref[0] = _rope(proj(_P_SK), sc, sa, sb, SWA_HEAD_DIM // 4).astype(sk_ref.dtype)


def _proj(xs, mod_l, g1, win, qn, kvn, wuq, wukvk, wukvv, rope_tabs, nctt, ctx_row):
    b, t, d = xs.shape
    tm = TOKEN_TILE
    nt = t // tm

    def tok(w):
        return pl.BlockSpec((1, tm, w), lambda i, j: (i, j, 0))

    def full(a):
        return pl.BlockSpec(a.shape, lambda i, j: (0,) * a.ndim)

    tab = pl.BlockSpec((tm, LANES), lambda i, j: (j, 0))
    mod_spec = pl.BlockSpec((1, 1, mod_l.shape[-1]),
                            lambda i, j: (jnp.where(j < nctt, ctx_row, i), 0, 0))
    widths = (1024, 1024, 512, 512, 128, 128, 1536, 512, 128, 3072)
    dtypes = (BF16, BF16, BF16, BF16, BF16, BF16, F32, BF16, F32, BF16)
    return pl.pallas_call(
        _proj_kernel,
        grid=(b, nt),
        in_specs=[tok(d), mod_spec, full(g1), full(win), full(qn), full(kvn), full(wuq),
                  full(wukvk), full(wukvv)] + [tab] * 6,
        out_specs=[tok(w) for w in widths],
        out_shape=[jax.ShapeDtypeStruct((b, t, w), dt) for w, dt in zip(widths, dtypes)],
        compiler_params=_cparams(("parallel", "parallel")),
        name="proj",
    )(xs, mod_l, g1, win, qn, kvn, wuq, wukvk, wukvv, *rope_tabs)


def _mla_kernel(q_ref, k_ref, v_ref, o_ref, *, n_ctx, q_tile, skip_ctx):
    n = pl.program_id(2)

    def run(row0, nq, nk):
        low = _iota((nq, LANES), 1) < MLA_V
        v = v_ref[0, 0:nk, :]
        v_low = _iota((nk, LANES), 1) < MLA_V
        one = jnp.ones_like(v)
        ss = [_mm_nt(q_ref[0, pl.ds(row0, nq), hh * LANES:(hh + 1) * LANES],
                     k_ref[0, 0:nk, hh * LANES:(hh + 1) * LANES]) for hh in range(2)]
        ps = [jnp.exp2(s - jnp.max(s, axis=-1, keepdims=True)) for s in ss]
        outs = []
        for hh in range(2):
            acc = _mm(ps[hh], jnp.where(v_low if hh == 0 else jnp.logical_not(v_low), v, one))
            den = acc[:, MLA_V:MLA_V + 1] if hh == 0 else acc[:, 0:1]
            outs.append(acc / den)
        o_ref[0, pl.ds(row0, nq), :] = jnp.where(low, outs[0], outs[1]).astype(o_ref.dtype)

    @pl.when(n == 0)
    def _():
        if skip_ctx:
            o_ref[0, 0:n_ctx, :] = jnp.zeros((n_ctx, LANES), o_ref.dtype)
        else:
            run(0, n_ctx, n_ctx)

    @pl.when(n > 0)
    def _():
        run(pl.multiple_of(n_ctx + (n - 1) * q_tile, LANES), q_tile, k_ref.shape[1])


def _mla(qm, km, vm, n_ctx, skip_ctx):
    b, t, _ = qm.shape
    q_tile = MLA_Q_TILE
    assert (t - n_ctx) % q_tile == 0
    steps = (t - n_ctx) // q_tile + 1
    return pl.pallas_call(
        functools.partial(_mla_kernel, n_ctx=n_ctx, q_tile=q_tile, skip_ctx=skip_ctx),
        grid=(b, MLA_HEADS // 2, steps),
        in_specs=[pl.BlockSpec((1, t, 2 * LANES), lambda i, j, n: (i, 0, j)),
                  pl.BlockSpec((1, t, 2 * LANES), lambda i, j, n: (i, 0, j)),
                  pl.BlockSpec((1, t, LANES), lambda i, j, n: (i, 0, j))],
        out_specs=pl.BlockSpec((1, t, LANES), lambda i, j, n: (i, 0, j)),
        out_shape=jax.ShapeDtypeStruct((b, t, MLA_HEADS * MLA_V), BF16),
        compiler_params=_cparams(("parallel", "parallel", "arbitrary")),
        name="mla",
    )(qm, km, vm)


def _swa_kernel(sink_ref, q_ref, k_ref, v_ref, o_ref, *, n_ctx, skip_ctx):
    i = pl.program_id(1)
    blk = q_ref.shape[1]
    t = k_ref.shape[1]
    n_lat = t - n_ctx
    nctt = n_ctx // blk
    low = _iota((blk, LANES), 1) < SWA_HEAD_DIM
    n_slab = SWA_HEADS * SWA_HEAD_DIM // LANES
    half = n_slab * blk

    parts, sinks = [], []
    for hh in range(2):
        for g in range(n_slab):
            qs = q_ref[0, :, g * LANES:(g + 1) * LANES]
            parts.append(jnp.where(low if hh == 0 else jnp.logical_not(low), qs, jnp.zeros_like(qs)))
            head = hh * n_slab + g
            sinks.append(jnp.broadcast_to(sink_ref[head:head + 1, 0:1] * LOG2E, (blk, 1)))
    q_all = jnp.concatenate(parts, axis=0)
    sink = jnp.concatenate(sinks, axis=0)

    def with_ones(v, hh):
        v_low = _iota(v.shape, 1) < SWA_HEAD_DIM
        return jnp.where(v_low if hh == 0 else jnp.logical_not(v_low), v, jnp.ones_like(v))

    def attend(loc):
        kc = k_ref[0, 0:n_ctx, :]
        vc = v_ref[0, 0:n_ctx, :]
        s_c = _mm_nt(q_all, kc)
        mx = jnp.maximum(jnp.max(s_c, axis=-1, keepdims=True), sink)
        if loc is not None:
            kl, vl, valid = loc
            s_l = jnp.where(valid[None], _mm_nt(q_all, kl).reshape(2 * n_slab, blk, valid.shape[-1]), NEG_INF)
            s_l = s_l.reshape(2 * n_slab * blk, valid.shape[-1])
            mx = jnp.maximum(mx, jnp.max(s_l, axis=-1, keepdims=True))
        p_c = jnp.exp2(s_c - mx).astype(BF16)
        p_sink = jnp.exp2(sink - mx)
        if loc is not None:
            p_l = jnp.exp2(s_l - mx).astype(BF16)
        outs = []
        for hh in range(2):
            rows = slice(hh * half, (hh + 1) * half)
            acc = _mm(p_c[rows], with_ones(vc, hh))
            if loc is not None:
                acc = acc + _mm(p_l[rows], with_ones(vl, hh))
            den = (acc[:, SWA_HEAD_DIM:SWA_HEAD_DIM + 1] if hh == 0 else acc[:, 0:1]) + p_sink[rows]
            outs.append(acc / den)
        for g in range(n_slab):
            rows = slice(g * blk, (g + 1) * blk)
            o_ref[0, :, g * LANES:(g + 1) * LANES] = jnp.where(low, outs[0][rows], outs[1][rows]).astype(o_ref.dtype)

    @pl.when(i < nctt)
    def _():
        if skip_ctx:
            o_ref[...] = jnp.zeros_like(o_ref)
        else:
            attend(None)

    @pl.when(i >= nctt)
    def _():
        n = i - nctt
        span = blk + 2 * WINDOW
        start = jnp.clip(n * blk - WINDOW, 0, n_lat - span)
        row0 = pl.multiple_of(n_ctx + start, LANES)
        kl = k_ref[0, pl.ds(row0, span), :]
        vl = v_ref[0, pl.ds(row0, span), :]
        kpos = start + _iota((blk, span), 1)
        qpos = n * blk + _iota((blk, span), 0)
        valid = jnp.abs(kpos - qpos) <= WINDOW
        attend((kl, vl, valid))


def _swa(sink, sq, sk, sv, n_ctx, skip_ctx):
    b, t, w = sq.shape
    blk = SWA_TILE
    nt = t // blk
    return pl.pallas_call(
        functools.partial(_swa_kernel, n_ctx=n_ctx, skip_ctx=skip_ctx),
        grid=(b, nt),
        in_specs=[pl.BlockSpec(sink.shape, lambda i, n: (0, 0)),
                  pl.BlockSpec((1, blk, w), lambda i, n: (i, n, 0)),
                  pl.BlockSpec((1, t, LANES), lambda i, n: (i, 0, 0)),
                  pl.BlockSpec((1, t, LANES), lambda i, n: (i, 0, 0))],
        out_specs=pl.BlockSpec((1, blk, w), lambda i, n: (i, n, 0)),
        out_shape=jax.ShapeDtypeStruct((b, t, w), BF16),
        compiler_params=_cparams(("parallel", "parallel")),
        name="swa",
    )(sink, sq, sk, sv)


def _block_diag(x, bd_mask):
    return jnp.where(bd_mask, jnp.concatenate([x] * PACK, axis=0), 0.0)


def _gdn_prep_kernel(qkv_ref, prev_ref, next_ref, ab_ref, cw_ref, gp_ref,
                     a_ref, rv_ref, rk_ref, qk_ref, qe_ref, ke_ref, dl_ref, *, seg_starts, seg_ends):
    ti = pl.program_id(1)
    tm = qkv_ref.shape[1]
    hw = GDN_HEADS * GDN_DK
    x = qkv_ref[0]
    row = _iota((tm, 1), 0)
    is_start = functools.reduce(jnp.logical_or, [ti == s for s in seg_starts])
    is_end = functools.reduce(jnp.logical_or, [ti == s for s in seg_ends])
    prev_row = jnp.where(is_start, 0.0, prev_ref[0, SUBLANES - 1:SUBLANES, :])
    next_row = jnp.where(is_end, 0.0, next_ref[0, 0:1, :])
    x_prev = jnp.where(row == 0, prev_row, pltpu.roll(x, 1, 0))
    x_next = jnp.where(row == tm - 1, next_row, pltpu.roll(x, tm - 1, 0))
    y = _silu(x_prev * cw_ref[0:1, :] + x * cw_ref[1:2, :] + x_next * cw_ref[2:3, :])

    ones_blk = _group_ones(LANES, GDN_DK).astype(BF16)

    def l2(slab):
        return slab * lax.rsqrt(_group_sum(slab * slab, ones_blk) + EPS)

    q = jnp.concatenate([l2(y[:, s * LANES:(s + 1) * LANES]) for s in range(hw // LANES)], axis=1)
    q = q * (GDN_DK ** -0.5)
    k = jnp.concatenate([l2(y[:, hw + s * LANES:hw + (s + 1) * LANES]) for s in range(hw // LANES)], axis=1)
    v = y[:, 2 * hw:3 * hw]

    ab = ab_ref[0]
    g_all = -jnp.exp(gp_ref[0:1, :]) * _softplus(ab + gp_ref[1:2, :])
    beta_all = _sigmoid(ab)
    ri = _iota((tm, tm), 0)
    ci = _iota((tm, tm), 1)
    same = jnp.right_shift(ri, 6) == jnp.right_shift(ci, 6)
    g_parts = jnp.concatenate(_split_bf16(g_all, 3), axis=1)

    def cumsum(tri):
        r = jnp.dot(jnp.logical_and(same, tri).astype(BF16), g_parts, preferred_element_type=F32)
        return r[:, 0:LANES] + r[:, LANES:2 * LANES] + r[:, 2 * LANES:3 * LANES]

    gam_all = jnp.where(_iota((tm, LANES), 1) < GDN_HEADS, cumsum(ci <= ri), cumsum(ci >= ri))
    col_head = jnp.right_shift(_iota((LANES, 2 * hw), 1), GDN_DK.bit_length() - 1)
    src = _iota((LANES, 2 * hw), 0)

    def expand(x, parts, sel):
        r = jnp.dot(jnp.concatenate(_split_bf16(x, parts), axis=0), sel.astype(BF16), preferred_element_type=F32)
        return functools.reduce(lambda a, b: a + b, [r[p * tm:(p + 1) * tm] for p in range(parts)])

    gam_exp = expand(gam_all, 3, src == col_head)
    b_exp = expand(beta_all, 2, src == col_head + 2 * GDN_HEADS)
    gam_f, gam_b = gam_exp[:, 0:hw], gam_exp[:, hw:2 * hw]

    c = GDN_CHUNK
    r64 = _iota((c, PACK_W), 0)
    c64 = jnp.bitwise_and(_iota((c, PACK_W), 1), c - 1)
    eye = (r64 == c64).astype(F32)
    bd_mask = jnp.right_shift(_iota((PACK_W, PACK_W), 0), 6) == jnp.right_shift(_iota((PACK_W, PACK_W), 1), 6)

    for ch in range(tm // c):
        rows = slice(ch * c, (ch + 1) * c)
        for gi in range(hw // PACK_W):
            lanes = slice(gi * PACK_W, (gi + 1) * PACK_W)
            kc, qc, vc = k[rows, lanes], q[rows, lanes], v[rows, lanes]
            kbd = _block_diag(kc, bd_mask)
            kq = _mm_nt(jnp.concatenate([kc, qc], axis=0), kbd)
            kk, qkm = kq[0:c], kq[c:2 * c]
            for d in range(2):
                gam = (gam_f if d == 0 else gam_b)[rows, lanes]
                beta = b_exp[rows, d * hw + gi * PACK_W:d * hw + (gi + 1) * PACK_W]
                gam_row = jnp.sum(gam * eye, axis=0, keepdims=True)
                decay = jnp.exp(jnp.minimum(gam - gam_row, 0.0))
                strict = (r64 > c64) if d == 0 else (r64 < c64)
                incl = (r64 >= c64) if d == 0 else (r64 <= c64)
                a_mat = jnp.where(strict, beta * kk * decay, 0.0)
                for pp in range(PACK_W // LANES):
                    a_ref[d, ch * (hw // LANES) + gi * (PACK_W // LANES) + pp] = \
                        a_mat[:, pp * LANES:(pp + 1) * LANES]
                e_gam = jnp.exp(gam)
                g_last = gam[c - 1:c, :] if d == 0 else gam[0:1, :]
                out = slice(d * hw + gi * PACK_W, d * hw + (gi + 1) * PACK_W)
                rv_ref[0, rows, out] = (vc * beta).astype(rv_ref.dtype)
                rk_ref[0, rows, out] = (kc * beta * e_gam).astype(rk_ref.dtype)
                qk_ref[0, rows, out] = jnp.where(incl, qkm * decay, 0.0).astype(qk_ref.dtype)
                qe_ref[0, rows, out] = (qc * e_gam).astype(qe_ref.dtype)
                ke_ref[0, rows, out] = (kc * jnp.exp(g_last - gam)).astype(ke_ref.dtype)
                dl_ref[0, ch, :, out] = jnp.exp(g_last)


def _gdn_prep(qkv, ab, cw, gp, seg_starts, seg_ends):
    b, t, w = qkv.shape
    tm = TOKEN_TILE
    nt = t // tm
    hb = tm // SUBLANES
    nh = t // SUBLANES
    wide = 2 * GDN_HEADS * GDN_DK
    c = GDN_CHUNK
    cpt = tm // c
    spt = cpt * (GDN_HEADS * GDN_DK // LANES)
    tok = lambda width: pl.BlockSpec((1, tm, width), lambda i, j: (i, j, 0))
    out_shape = [jax.ShapeDtypeStruct((2, b * nt * spt, c, LANES), F32)] + \
                [jax.ShapeDtypeStruct((b, t, wide), BF16)] * 5 + \
                [jax.ShapeDtypeStruct((b, t // c, 1, wide), F32)]
    return pl.pallas_call(
        functools.partial(_gdn_prep_kernel, seg_starts=seg_starts, seg_ends=seg_ends),
        grid=(b, nt),
        in_specs=[tok(w),
                  pl.BlockSpec((1, SUBLANES, w), lambda i, j: (i, jnp.maximum(j * hb - 1, 0), 0)),
                  pl.BlockSpec((1, SUBLANES, w), lambda i, j: (i, jnp.minimum((j + 1) * hb, nh - 1), 0)),
                  tok(LANES),
                  pl.BlockSpec(cw.shape, lambda i, j: (0, 0)),
                  pl.BlockSpec(gp.shape, lambda i, j: (0, 0))],
        out_specs=[pl.BlockSpec((2, spt, c, LANES), lambda i, j: (0, i * nt + j, 0, 0))] +
                  [tok(wide)] * 5 + [pl.BlockSpec((1, cpt, 1, wide), lambda i, j: (i, j, 0, 0))],
        out_shape=out_shape,
        compiler_params=_cparams(("parallel", "parallel")),
        name="gdn_prep",
    )(qkv, qkv, qkv, ab, cw, gp)


def _gdn_solve_kernel(a_ref, o_ref, at_ref, x_ref, ot_ref):
    c = GDN_CHUNK
    sub = SUBLANES
    nb = c // sub
    a_rows = pltpu.einshape("sil->isl", a_ref[0])
    for i in range(c):
        at_ref[i * LANES:(i + 1) * LANES, :] = a_rows[i].T
    x_ref[...] = jnp.zeros_like(x_ref)
    sub_iota = _iota((sub, LANES), 0)

    def solve(upper):
        for step in range(nb):
            ib = (nb - 1 - step) if upper else step
            kbs = range(ib, nb) if upper else range(ib + 1)

            def row(r, carry, ib=ib, kbs=kbs):
                i = ib * sub + ((sub - 1 - r) if upper else r)
                base = pl.multiple_of(i * LANES, LANES)
                accs = {(h2, jb): (sub_iota + jb * sub == i).astype(F32)
                        for h2 in range(LANES // c) for jb in range(nb)}
                for kb in kbs:
                    for h2 in range(LANES // c):
                        coefs = [at_ref[pl.ds(base + h2 * c + kb * sub + kk, 1), :] for kk in range(sub)]
                        for jb in (range(kb, nb) if upper else range(kb + 1)):
                            acc = accs[(h2, jb)]
                            for kk in range(sub):
                                lo = (kb * sub + kk) * LANES + h2 * c + jb * sub
                                acc = acc - coefs[kk] * x_ref[lo:lo + sub, :]
                            accs[(h2, jb)] = acc
                for h2 in range(LANES // c):
                    for jb in range(nb):
                        x_ref[pl.ds(base + h2 * c + jb * sub, sub), :] = accs[(h2, jb)]
                return carry

            lax.fori_loop(0, sub, row, 0)

    @pl.when(pl.program_id(0) == 0)
    def _():
        solve(False)

    @pl.when(pl.program_id(0) == 1)
    def _():
        solve(True)

    for i in range(c):
        ot_ref[i] = x_ref[i * LANES:(i + 1) * LANES, :].T
    o_ref[0] = pltpu.einshape("isl->sil", ot_ref[...])


def _gdn_solve(a):
    _, n_slab, c, _ = a.shape
    assert n_slab % LANES == 0
    blk = pl.BlockSpec((1, LANES, c, LANES), lambda d, g: (d, g, 0, 0))
    return pl.pallas_call(
        _gdn_solve_kernel,
        grid=(2, n_slab // LANES),
        in_specs=[blk],
        out_specs=blk,
        out_shape=jax.ShapeDtypeStruct(a.shape, F32),
        scratch_shapes=[pltpu.VMEM((c * LANES, LANES), F32), pltpu.VMEM((c * LANES, LANES), F32),
                        pltpu.VMEM((c, LANES, LANES), F32)],
        compiler_params=_cparams(("parallel", "parallel")),
        name="gdn_solve",
    )(a)


def _gdn_apply_kernel(t_ref, rv_ref, rk_ref, u_ref, w_ref):
    tm = rv_ref.shape[1]
    c = GDN_CHUNK
    hw = GDN_HEADS * GDN_DK
    ppg = PACK_W // LANES
    bd_mask = jnp.right_shift(_iota((PACK_W, PACK_W), 0), 6) == jnp.right_shift(_iota((PACK_W, PACK_W), 1), 6)
    for ch in range(tm // c):
        rows = slice(ch * c, (ch + 1) * c)
        for gi in range(hw // PACK_W):
            for d in range(2):
                out = slice(d * hw + gi * PACK_W, d * hw + (gi + 1) * PACK_W)
                tinv = jnp.concatenate([t_ref[d, ch * (hw // LANES) + gi * ppg + pp] for pp in range(ppg)],
                                       axis=1)
                t16 = tinv.astype(BF16)
                u_ref[0, rows, out] = jnp.dot(t16, _block_diag(rv_ref[0, rows, out], bd_mask),
                                              preferred_element_type=F32)
                w_ref[0, rows, out] = jnp.dot(t16, _block_diag(rk_ref[0, rows, out], bd_mask),
                                              preferred_element_type=F32).astype(w_ref.dtype)


def _gdn_apply(tinv, rv, rk):
    b, t, wide = rv.shape
    tm = TOKEN_TILE
    c = GDN_CHUNK
    nt = t // tm
    spt = (tm // c) * (GDN_HEADS * GDN_DK // LANES)
    tok = pl.BlockSpec((1, tm, wide), lambda i, j: (i, j, 0))
    return pl.pallas_call(
        _gdn_apply_kernel,
        grid=(b, nt),
        in_specs=[pl.BlockSpec((2, spt, c, LANES), lambda i, j: (0, i * nt + j, 0, 0)), tok, tok],
        out_specs=[tok, tok],
        out_shape=[jax.ShapeDtypeStruct((b, t, wide), F32), jax.ShapeDtypeStruct((b, t, wide), BF16)],
        compiler_params=_cparams(("parallel", "parallel")),
        name="gdn_apply",
    )(tinv, rv, rk)


def _gdn_scan_kernel(uf, wf, qkf, qef, kef, dlf, ub, wb, qkb, qeb, keb, dlb, of_ref, ob_ref, s_ref):
    c = GDN_CHUNK
    cpt = uf.shape[1] // c
    ng = uf.shape[2] // PACK_W
    bd_mask = jnp.right_shift(_iota((PACK_W, PACK_W), 0), 6) == jnp.right_shift(_iota((PACK_W, PACK_W), 1), 6)

    @pl.when(pl.program_id(1) == 0)
    def _():
        s_ref[...] = jnp.zeros_like(s_ref)

    chains = ((uf, wf, qkf, qef, kef, dlf, of_ref), (ub, wb, qkb, qeb, keb, dlb, ob_ref))
    for j in range(cpt):
        where = []
        for d in range(2):
            ch = j if d == 0 else cpt - 1 - j
            for gi in range(ng):
                where.append((d, gi, ch, slice(ch * c, (ch + 1) * c), slice(gi * PACK_W, (gi + 1) * PACK_W)))
        states = [s_ref[d * ng + gi] for d, gi, _, _, _ in where]
        wqs = [_mm(jnp.concatenate([chains[d][1][0, rows, lanes], chains[d][3][0, rows, lanes]], axis=0), st)
               for (d, gi, ch, rows, lanes), st in zip(where, states)]
        v_news = [chains[d][0][0, rows, lanes] - wq[0:c] for (d, gi, ch, rows, lanes), wq in zip(where, wqs)]
        kvs = [_mm_tn(chains[d][4][0, rows, lanes], v_new) for (d, gi, ch, rows, lanes), v_new in zip(where, v_news)]
        for (d, gi, ch, rows, lanes), st, kv in zip(where, states, kvs):
            s_ref[d * ng + gi] = st * chains[d][5][0, ch, :, lanes] + jnp.where(bd_mask, kv, 0.0)
        for (d, gi, ch, rows, lanes), wq, v_new in zip(where, wqs, v_news):
            chains[d][6][0, rows, lanes] = wq[c:2 * c] + _mm(chains[d][2][0, rows, lanes],
                                                             _block_diag(v_new, bd_mask))


def _gdn_scan(u, w, qk, qe, ke, dl, n_ctx):
    b, t, wide = u.shape
    tm = TOKEN_TILE
    nt = t // tm
    nctt = n_ctx // tm
    hw = wide // 2
    cpt = tm // GDN_CHUNK

    def mirror(g):
        return jnp.where(g < nctt, nctt - 1 - g, nt - 1 - (g - nctt))

    fwd = pl.BlockSpec((1, tm, hw), lambda i, g: (i, g, 0))
    bwd = pl.BlockSpec((1, tm, hw), lambda i, g: (i, mirror(g), 1))
    dfwd = pl.BlockSpec((1, cpt, 1, hw), lambda i, g: (i, g, 0, 0))
    dbwd = pl.BlockSpec((1, cpt, 1, hw), lambda i, g: (i, mirror(g), 0, 1))
    out_sds = jax.ShapeDtypeStruct((b, t, hw), F32)
    return pl.pallas_call(
        _gdn_scan_kernel,
        grid=(b, nt),
        in_specs=[fwd] * 5 + [dfwd] + [bwd] * 5 + [dbwd],
        out_specs=[pl.BlockSpec((1, tm, hw), lambda i, g: (i, g, 0)),
                   pl.BlockSpec((1, tm, hw), lambda i, g: (i, mirror(g), 0))],
        out_shape=[out_sds, out_sds],
        scratch_shapes=[pltpu.VMEM((2 * hw // PACK_W, PACK_W, PACK_W), F32)],
        compiler_params=_cparams(("parallel", "arbitrary")),
        name="gdn_scan",
    )(u, w, qk, qe, ke, dl, u, w, qk, qe, ke, dl)


def _merge_kernel(x_ref, mod_ref, ya_ref, yb_ref, of_ref, ob_ref, z_ref, gate_ref, on_ref,
                  wpa_ref, wpb_ref, wpc_ref, wout_ref, o_ref):
    x = x_ref[0]
    d = x.shape[-1]
    m = mod_ref[0]
    oc = of_ref[0] + ob_ref[0]
    ones_blk = _group_ones(LANES, GDN_DV).astype(BF16)
    ms = jnp.concatenate(
        [_group_sum(oc[:, s * LANES:(s + 1) * LANES] ** 2, ones_blk) for s in range(oc.shape[-1] // LANES)],
        axis=1) * (1.0 / GDN_DV)
    yc = oc * lax.rsqrt(ms + EPS) * on_ref[...] * _silu(z_ref[0].astype(F32))
    gate = gate_ref[0].astype(F32)
    mix = (_sigmoid(gate[:, 0:d]) * _mm(ya_ref[0], wpa_ref[...])
           + _sigmoid(gate[:, d:2 * d]) * _mm(yb_ref[0], wpb_ref[...])
           + _sigmoid(gate[:, 2 * d:3 * d]) * _mm(yc, wpc_ref[...]))
    o_ref[0] = x + m[:, 2 * d:3 * d] * _mm(mix, wout_ref[...])


def _merge(xs, mod_l, ya, yb, oc_f, oc_b, z, gate, onorm, wpa, wpb, wpc, wout, nctt, ctx_row, t_off):
    b, t, d = xs.shape
    tm = TOKEN_TILE
    nt = t // tm - t_off
    tok = lambda width: pl.BlockSpec((1, tm, width), lambda i, j: (i, j + t_off, 0))
    full = lambda a: pl.BlockSpec(a.shape, lambda i, j: (0,) * a.ndim)
    mod_spec = pl.BlockSpec((1, 1, mod_l.shape[-1]),
                            lambda i, j: (jnp.where(j + t_off < nctt, ctx_row, i), 0, 0))
    return pl.pallas_call(
        _merge_kernel,
        grid=(b, nt),
        in_specs=[tok(d), mod_spec, tok(512), tok(512), tok(512), tok(512), tok(512), tok(3 * d),
                  full(onorm), full(wpa), full(wpb), full(wpc), full(wout)],
        out_specs=pl.BlockSpec((1, tm, d), lambda i, j: (i, j, 0)),
        out_shape=jax.ShapeDtypeStruct((b, nt * tm, d), F32),
        compiler_params=_cparams(("parallel", "parallel")),
        name="merge",
    )(xs, mod_l, ya, yb, oc_f, oc_b, z, gate, onorm, wpa, wpb, wpc, wout)


def _ffn_kernel(x_ref, prev_ref, next_ref, mod_ref, g2_ref, wup_ref, cw_ref, wdn_ref, nf_ref, o_ref,
                *, seg_starts, seg_ends, final):
    ti = pl.program_id(1)
    x = x_ref[0]
    tm, d = x.shape
    m = mod_ref[0]
    xall = jnp.concatenate([prev_ref[0], x, next_ref[0]], axis=0)
    xa = _rms(xall, g2_ref[...]) * (1.0 + m[:, 4 * d:5 * d]) + m[:, 3 * d:4 * d]
    n_all = tm + 2 * SUBLANES
    row = _iota((n_all, 1), 0)
    is_start = functools.reduce(jnp.logical_or, [ti == s for s in seg_starts])
    is_end = functools.reduce(jnp.logical_or, [ti == s for s in seg_ends])
    drop = jnp.logical_or(jnp.logical_and(row == SUBLANES - 1, is_start),
                          jnp.logical_and(row == SUBLANES + tm, is_end))
    xa = jnp.where(drop, 0.0, xa).astype(BF16)

    def conv(h, cols):
        hp = pltpu.roll(h, 1, 0)[SUBLANES:SUBLANES + tm]
        hn = pltpu.roll(h, n_all - 1, 0)[SUBLANES:SUBLANES + tm]
        return (hp * cw_ref[0:1, cols] + h[SUBLANES:SUBLANES + tm] * cw_ref[1:2, cols]
                + hn * cw_ref[2:3, cols])

    cwid = FFN_DIM // FFN_SPLIT
    acc = jnp.zeros((tm, d), F32)
    for j in range(FFN_SPLIT):
        ca = slice(j * cwid, (j + 1) * cwid)
        cb = slice(FFN_DIM + j * cwid, FFN_DIM + (j + 1) * cwid)
        ha = conv(jnp.dot(xa, wup_ref[:, ca], preferred_element_type=F32), ca)
        hb = conv(jnp.dot(xa, wup_ref[:, cb], preferred_element_type=F32), cb)
        acc = acc + _mm(_silu(ha) * hb, wdn_ref[ca, :])
    y = x + m[:, 5 * d:6 * d] * acc
    if final:
        y = _rms(y, nf_ref[...])
    o_ref[0] = y


def _ffn(x1, mod_l, g2, wup, cw, wdn, nf, nctt_mod, ctx_row, seg_starts, seg_ends, final):
    b, t, d = x1.shape
    tm = TOKEN_TILE
    nt = t // tm
    hb = tm // SUBLANES
    nh = t // SUBLANES
    full = lambda a: pl.BlockSpec(a.shape, lambda i, j: (0,) * a.ndim)
    mod_spec = pl.BlockSpec((1, 1, mod_l.shape[-1]),
                            lambda i, j: (jnp.where(j < nctt_mod, ctx_row, i), 0, 0))
    return pl.pallas_call(
        functools.partial(_ffn_kernel, seg_starts=seg_starts, seg_ends=seg_ends, final=final),
        grid=(b, nt),
        in_specs=[pl.BlockSpec((1, tm, d), lambda i, j: (i, j, 0)),
                  pl.BlockSpec((1, SUBLANES, d), lambda i, j: (i, jnp.maximum(j * hb - 1, 0), 0)),
                  pl.BlockSpec((1, SUBLANES, d), lambda i, j: (i, jnp.minimum((j + 1) * hb, nh - 1), 0)),
                  mod_spec, full(g2), full(wup), full(cw), full(wdn), full(nf)],
        out_specs=pl.BlockSpec((1, tm, d), lambda i, j: (i, j, 0)),
        out_shape=jax.ShapeDtypeStruct((b, t, d), F32),
        compiler_params=_cparams(("parallel", "parallel")),
        name="ffn",
    )(x1, x1, x1, mod_l, g2, wup, cw, wdn, nf)


def _cols(w, segments):
    parts = [jnp.zeros(w.shape[:-1] + (seg,), w.dtype) if isinstance(seg, int) else w[..., seg[0]:seg[1]]
             for seg in segments]
    return jnp.concatenate(parts, axis=-1)


def _win_segments(d):
    widths = [MLA_Q_RANK, MLA_KV_RANK, MLA_ROPE, SWA_HEADS * SWA_HEAD_DIM, SWA_KV_HEADS * SWA_HEAD_DIM,
              SWA_KV_HEADS * SWA_HEAD_DIM, GDN_HEADS * (2 * GDN_DK + GDN_DV), GDN_HEADS * GDN_DV,
              2 * GDN_HEADS, 2 * GDN_HEADS, 3 * d]
    off = np.cumsum([0] + widths)
    cq, ckv, kr, sq, sk, sv, qkv, z, a, b, gate = [(int(off[i]), int(off[i + 1])) for i in range(11)]
    sq_perm = [(sq[0] + h * SWA_HEAD_DIM, sq[0] + (h + 1) * SWA_HEAD_DIM) for h in _swa_head_order()]
    segs = [cq, ckv, MLA_NOPE, kr, LANES - MLA_NOPE - MLA_ROPE] + sq_perm + \
           [sk, sv, qkv, z, a, b, LANES - 4 * GDN_HEADS, gate]
    return segs


def _swa_head_order():
    group = SWA_HEADS // SWA_KV_HEADS
    return [kv * group + g for g in range(group) for kv in range(SWA_KV_HEADS)]


def _rope_tables(n_ctx, n_lat):
    t = np.arange(n_lat)
    pos = np.stack([t // GRID_W, t % GRID_W]).astype(np.float32)

    def build(head_w, rope_lo, rope_w):
        da = rope_w // 2
        half = da // 2
        inv = ROPE_THETA ** (-np.arange(half, dtype=np.float32) / half)
        c = np.ones((n_lat, LANES), np.float32)
        sa = np.zeros((n_lat, LANES), np.float32)
        sb = np.zeros((n_lat, LANES), np.float32)
        for lane in range(LANES):
            dd = lane % head_w - rope_lo
            if dd < 0 or dd >= rope_w:
                continue
            axis, r = dd // da, dd % da
            ang = pos[axis] * inv[r % half]
            c[:, lane] = np.cos(ang)
            if r < half:
                sa[:, lane] = -np.sin(ang)
            else:
                sb[:, lane] = np.sin(ang)
        ctx = [np.ones((n_ctx, LANES), np.float32), np.zeros((n_ctx, LANES), np.float32),
               np.zeros((n_ctx, LANES), np.float32)]
        return [jnp.asarray(np.concatenate([cx, lt])) for cx, lt in zip(ctx, (c, sa, sb))]

    return build(LANES, MLA_NOPE, MLA_ROPE) + build(SWA_HEAD_DIM, 0, SWA_HEAD_DIM)


def kernel(x, c, ctx, c_ctx, w_mod, b_mod, norm1, norm2, w_in, mla_q_norm, mla_kv_norm, w_uq, w_ukv,
           swa_sink, gdn_conv, gdn_a_log, gdn_dt_bias, gdn_norm, w_branch_a, w_branch_b, w_branch_c,
           w_out, ffn_up, ffn_conv, ffn_down, norm_f):
    b, s, d = x.shape
    n_ctx = ctx.shape[1]
    depth = w_mod.shape[0]
    tm = TOKEN_TILE
    assert n_ctx % tm == 0 and s % tm == 0 and s >= SWA_TILE + 2 * WINDOW and WINDOW % LANES == 0
    nctt = n_ctx // tm
    ntt = (n_ctx + s) // tm

    rows = -(-(b + 1) // SUBLANES) * SUBLANES
    cc = jnp.zeros((rows, d), F32).at[0:b].set(c).at[b].set(c_ctx)
    mod = _modulation(cc, w_mod, b_mod).reshape(depth, rows, 1, 6 * d)

    hd_q = MLA_NOPE + MLA_ROPE
    hd_kv = MLA_NOPE + MLA_V
    uq_segs = [s for h in range(MLA_HEADS) for s in ((h * hd_q, (h + 1) * hd_q), LANES - hd_q)]
    uk_segs = [s for h in range(MLA_HEADS) for s in ((h * hd_kv, h * hd_kv + MLA_NOPE), LANES - MLA_NOPE)]
    uv_segs = [(h * hd_kv + MLA_NOPE, (h + 1) * hd_kv) for h in range(MLA_HEADS)]
    win_segs = _win_segments(d)

    def layer_weights(l):
        wpb = jnp.concatenate([w_branch_b[l, h * SWA_HEAD_DIM:(h + 1) * SWA_HEAD_DIM] for h in _swa_head_order()],
                              axis=0)
        gpar = jnp.zeros((SUBLANES, LANES), F32)
        gpar = gpar.at[0, 0:2 * GDN_HEADS].set(gdn_a_log[l].reshape(-1))
        gpar = gpar.at[1, 0:2 * GDN_HEADS].set(gdn_dt_bias[l].reshape(-1))
        pad_rows = lambda a: jnp.pad(a, ((0, SUBLANES - a.shape[0]), (0, 0)))
        return dict(
            win=_cols(w_in[l], win_segs).astype(BF16),
            wuq=_cols(w_uq[l], uq_segs).astype(BF16),
            wukvk=_cols(w_ukv[l], uk_segs).astype(BF16),
            wukvv=_cols(w_ukv[l], uv_segs).astype(BF16),
            wpa=w_branch_a[l].astype(BF16), wpb=wpb.astype(BF16), wpc=w_branch_c[l].astype(BF16),
            wout=w_out[l].astype(BF16), wup=ffn_up[l].astype(BF16), wdn=ffn_down[l].astype(BF16),
            sink=jnp.broadcast_to(swa_sink[l][:, None], (SWA_HEADS, LANES)),
            onorm=jnp.tile(gdn_norm[l], GDN_HEADS).reshape(1, -1),
            gconv=pad_rows(gdn_conv[l]), fconv=pad_rows(ffn_conv[l]), gpar=gpar)

    rope_tabs = _rope_tables(n_ctx, s)

    xs = jnp.concatenate([ctx, x], axis=1)
    row2 = lambda a: a.reshape(1, -1)
    for l in range(depth):
        last = l == depth - 1
        t_off = nctt if last else 0
        p = layer_weights(l)
        qm, km, vm, sq, sk, sv, qkv, z, ab, gate = _proj(
            xs, mod[l], row2(norm1[l]), p["win"], row2(mla_q_norm[l]), row2(mla_kv_norm[l]),
            p["wuq"], p["wukvk"], p["wukvv"], rope_tabs, nctt, b)
        ya = _mla(qm, km, vm, n_ctx, last)
        yb = _swa(p["sink"], sq, sk, sv, n_ctx, last)
        a_mat, rv, rk, qk, qe, ke, dl = _gdn_prep(qkv, ab, p["gconv"], p["gpar"], (0, nctt), (nctt - 1, ntt - 1))
        u, w = _gdn_apply(_gdn_solve(a_mat), rv, rk)
        oc_f, oc_b = _gdn_scan(u, w, qk, qe, ke, dl, n_ctx)
        x1 = _merge(xs, mod[l], ya, yb, oc_f, oc_b, z, gate, p["onorm"], p["wpa"], p["wpb"], p["wpc"], p["wout"],
                    nctt, b, t_off)
        if last:
            xs = _ffn(x1, mod[l], row2(norm2[l]), p["wup"], p["fconv"], p["wdn"], row2(norm_f),
                      0, b, (0,), (s // tm - 1,), True)
        else:
            xs = _ffn(x1, mod[l], row2(norm2[l]), p["wup"], p["fconv"], p["wdn"], row2(norm_f),
                      nctt, b, (0, nctt), (nctt - 1, ntt - 1), False)
    return xs
```

```python
import functools

import numpy as np
import jax
import jax.numpy as jnp
from jax import lax
from jax.experimental import pallas as pl
from jax.experimental.pallas import tpu as pltpu

F32 = jnp.float32
BF16 = jnp.bfloat16

GRID_W = 64
ROPE_THETA = 10000.0
NEG_INF = -1e30
EPS = 1e-6
LOG2E = 1.4426950408889634

MLA_HEADS = 8
MLA_Q_RANK = 384
MLA_KV_RANK = 256
MLA_NOPE = 64
MLA_ROPE = 32
MLA_V = 64

SWA_HEADS = 8
SWA_KV_HEADS = 2
SWA_HEAD_DIM = 64
WINDOW = 128

GDN_HEADS = 8
GDN_DK = 64
GDN_DV = 64
GDN_CHUNK = 64

FFN_DIM = 2816

LANES = 128
SUBLANES = 8
VMEM_LIMIT_BYTES = 56 * 1024 * 1024

TOKEN_TILE = 256
SWA_TILE = 128
MLA_Q_TILE = 1024
PACK = 4
PACK_W = PACK * GDN_DK
FFN_SPLIT = 1


def _sigmoid(x):
    return 0.5 * jnp.tanh(0.5 * x) + 0.5


def _silu(x):
    return x * _sigmoid(x)


def _softplus(x):
    return jnp.maximum(x, 0.0) + jnp.log(1.0 + jnp.exp(-jnp.abs(x)))


def _rms(x, g):
    return x * lax.rsqrt(jnp.mean(x * x, axis=-1, keepdims=True) + EPS) * g


def _mm(a, b):
    return jnp.dot(a.astype(BF16), b.astype(BF16), preferred_element_type=F32)


def _mm_nt(a, b):
    return lax.dot_general(a.astype(BF16), b.astype(BF16), (((1,), (1,)), ((), ())),
                           preferred_element_type=F32)


def _mm_tn(a, b):
    return lax.dot_general(a.astype(BF16), b.astype(BF16), (((0,), (0,)), ((), ())),
                           preferred_element_type=F32)


def _split_bf16(x, parts):
    out = []
    for _ in range(parts - 1):
        hi = x.astype(BF16)
        out.append(hi)
        x = x - hi.astype(F32)
    out.append(x.astype(BF16))
    return out


def _group_sum(x, ones_blk):
    n = x.shape[0]
    r = jnp.dot(jnp.concatenate(_split_bf16(x, 2), axis=0), ones_blk, preferred_element_type=F32)
    return r[0:n] + r[n:2 * n]


def _iota(shape, dim):
    return lax.broadcasted_iota(jnp.int32, shape, dim)


def _group_ones(n, group):
    sh = group.bit_length() - 1
    return (jnp.right_shift(_iota((n, n), 0), sh) == jnp.right_shift(_iota((n, n), 1), sh)).astype(F32)


def _rope(x, c, sa, sb, half):
    n = x.shape[-1]
    return x * c + pltpu.roll(x, n - half, 1) * sa + pltpu.roll(x, half, 1) * sb


def _cparams(sem):
    return pltpu.CompilerParams(dimension_semantics=sem, vmem_limit_bytes=VMEM_LIMIT_BYTES)


def _mod_kernel(c_ref, w_ref, b_ref, o_ref):
    s = _silu(c_ref[...])
    o_ref[0] = _mm(s, w_ref[0]) + b_ref[0]


def _modulation(cc, w_mod, b_mod):
    depth, d, n = w_mod.shape
    r = cc.shape[0]
    tn = 2048
    return pl.pallas_call(
        _mod_kernel,
        grid=(depth, n // tn),
        in_specs=[pl.BlockSpec((r, d), lambda l, j: (0, 0)),
                  pl.BlockSpec((1, d, tn), lambda l, j: (l, 0, j)),
                  pl.BlockSpec((1, 1, tn), lambda l, j: (l, 0, j))],
        out_specs=pl.BlockSpec((1, r, tn), lambda l, j: (l, 0, j)),
        out_shape=jax.ShapeDtypeStruct((depth, r, n), F32),
        compiler_params=_cparams(("parallel", "parallel")),
        name="modulation",
    )(cc, w_mod, b_mod.reshape(depth, 1, n))


_P_CQ = (0, 384)
_P_CKV = (384, 640)
_P_KR = (640, 768)
_P_SQ = (768, 1280)
_P_SK = (1280, 1408)
_P_SV = (1408, 1536)
_P_QKV = (1536, 3072)
_P_Z = (3072, 3584)
_P_AB = (3584, 3712)
_P_GATE = (3712, 6784)
_P_WIDTH = 6784


def _proj_kernel(x_ref, mod_ref, g1_ref, win_ref, qn_ref, kvn_ref, wuq_ref, wukvk_ref, wukvv_ref,
                 mc_ref, ma_ref, mb_ref, sc_ref, sa_ref, sb_ref,
                 qm_ref, km_ref, vm_ref, sq_ref, sk_ref, sv_ref, qkv_ref, z_ref, ab_ref, gate_ref):
    x = x_ref[0]
    d = x.shape[-1]
    m = mod_ref[0]
    xb = (_rms(x, g1_ref[...]) * (1.0 + m[:, d:2 * d]) + m[:, 0:d]).astype(BF16)

    def proj(piece):
        return jnp.dot(xb, win_ref[:, piece[0]:piece[1]], preferred_element_type=F32)

    mc, ma, mb = mc_ref[...], ma_ref[...], mb_ref[...]
    sc, sa, sb = sc_ref[...], sa_ref[...], sb_ref[...]

    cq = _rms(proj(_P_CQ), qn_ref[...])
    ckv = _rms(proj(_P_CKV), kvn_ref[...])

    gate_ref[0] = proj(_P_GATE).astype(gate_ref.dtype)
    qkv_ref[0] = proj(_P_QKV)
    z_ref[0] = proj(_P_Z).astype(z_ref.dtype)
    ab_ref[0] = proj(_P_AB)
    sv_ref[0] = proj(_P_SV).astype(sv_ref.dtype)

    q = _mm(cq, wuq_ref[...]) * ((MLA_NOPE + MLA_ROPE) ** -0.5 * LOG2E)
    kn = _mm(ckv, wukvk_ref[...])
    vm_ref[0] = _mm(ckv, wukvv_ref[...]).astype(vm_ref.dtype)
    kr = _rope(proj(_P_KR), mc, ma, mb, MLA_ROPE // 4)
    for h in range(MLA_HEADS):
        sl = slice(h * LANES, (h + 1) * LANES)
        qm_ref[0, :, sl] = _rope(q[:, sl], mc, ma, mb, MLA_ROPE // 4).astype(qm_ref.dtype)
        km_ref[0, :, sl] = (kn[:, sl] + kr).astype(km_ref.dtype)

    sq = proj(_P_SQ) * (SWA_HEAD_DIM ** -0.5 * LOG2E)
    for g in range(SWA_HEADS * SWA_HEAD_DIM // LANES):
        sl = slice(g * LANES, (g + 1) * LANES)
        sq_ref[0, :, sl] = _rope(sq[:, sl], sc, sa, sb, SWA_HEAD_DIM // 4).astype(sq_ref.dtype)
    sk_ref[0] = _rope(proj(_P_SK), sc, sa, sb, SWA_HEAD_DIM // 4).astype(sk_ref.dtype)


def _proj(xs, mod_l, g1, win, qn, kvn, wuq, wukvk, wukvv, rope_tabs, nctt, ctx_row):
    b, t, d = xs.shape
    tm = TOKEN_TILE
    nt = t // tm

    def tok(w):
        return pl.BlockSpec((1, tm, w), lambda i, j: (i, j, 0))

    def full(a):
        return pl.BlockSpec(a.shape, lambda i, j: (0,) * a.ndim)

    tab = pl.BlockSpec((tm, LANES), lambda i, j: (j, 0))
    mod_spec = pl.BlockSpec((1, 1, mod_l.shape[-1]),
                            lambda i, j: (jnp.where(j < nctt, ctx_row, i), 0, 0))
    widths = (1024, 1024, 512, 512, 128, 128, 1536, 512, 128, 3072)
    dtypes = (BF16, BF16, BF16, BF16, BF16, BF16, F32, BF16, F32, BF16)
    return pl.pallas_call(
        _proj_kernel,
        grid=(b, nt),
        in_specs=[tok(d), mod_spec, full(g1), full(win), full(qn), full(kvn), full(wuq),
                  full(wukvk), full(wukvv)] + [tab] * 6,
        out_specs=[tok(w) for w in widths],
        out_shape=[jax.ShapeDtypeStruct((b, t, w), dt) for w, dt in zip(widths, dtypes)],
        compiler_params=_cparams(("parallel", "parallel")),
        name="proj",
    )(xs, mod_l, g1, win, qn, kvn, wuq, wukvk, wukvv, *rope_tabs)


def _mla_kernel(q_ref, k_ref, v_ref, o_ref, *, n_ctx, q_tile, skip_ctx):
    n = pl.program_id(2)

    def run(row0, nq, nk):
        low = _iota((nq, LANES), 1) < MLA_V
        v = v_ref[0, 0:nk, :]
        v_low = _iota((nk, LANES), 1) < MLA_V
        one = jnp.ones_like(v)
        ss = [_mm_nt(q_ref[0, pl.ds(row0, nq), hh * LANES:(hh + 1) * LANES],
                     k_ref[0, 0:nk, hh * LANES:(hh + 1) * LANES]) for hh in range(2)]
        ps = [jnp.exp2(s - jnp.max(s, axis=-1, keepdims=True)) for s in ss]
        outs = []
        for hh in range(2):
            acc = _mm(ps[hh], jnp.where(v_low if hh == 0 else jnp.logical_not(v_low), v, one))
            den = acc[:, MLA_V:MLA_V + 1] if hh == 0 else acc[:, 0:1]
            outs.append(acc / den)
        o_ref[0, pl.ds(row0, nq), :] = jnp.where(low, outs[0], outs[1]).astype(o_ref.dtype)

    @pl.when(n == 0)
    def _():
        if skip_ctx:
            o_ref[0, 0:n_ctx, :] = jnp.zeros((n_ctx, LANES), o_ref.dtype)
        else:
            run(0, n_ctx, n_ctx)

    @pl.when(n > 0)
    def _():
        run(pl.multiple_of(n_ctx + (n - 1) * q_tile, LANES), q_tile, k_ref.shape[1])


def _mla(qm, km, vm, n_ctx, skip_ctx):
    b, t, _ = qm.shape
    q_tile = MLA_Q_TILE
    assert (t - n_ctx) % q_tile == 0
    steps = (t - n_ctx) // q_tile + 1
    return pl.pallas_call(
        functools.partial(_mla_kernel, n_ctx=n_ctx, q_tile=q_tile, skip_ctx=skip_ctx),
        grid=(b, MLA_HEADS // 2, steps),
        in_specs=[pl.BlockSpec((1, t, 2 * LANES), lambda i, j, n: (i, 0, j)),
                  pl.BlockSpec((1, t, 2 * LANES), lambda i, j, n: (i, 0, j)),
                  pl.BlockSpec((1, t, LANES), lambda i, j, n: (i, 0, j))],
        out_specs=pl.BlockSpec((1, t, LANES), lambda i, j, n: (i, 0, j)),
        out_shape=jax.ShapeDtypeStruct((b, t, MLA_HEADS * MLA_V), BF16),
        compiler_params=_cparams(("parallel", "parallel", "arbitrary")),
        name="mla",
    )(qm, km, vm)


def _swa_kernel(sink_ref, q_ref, k_ref, v_ref, o_ref, *, n_ctx, skip_ctx):
    i = pl.program_id(1)
    blk = q_ref.shape[1]
    t = k_ref.shape[1]
    n_lat = t - n_ctx
    nctt = n_ctx // blk
    low = _iota((blk, LANES), 1) < SWA_HEAD_DIM
    n_slab = SWA_HEADS * SWA_HEAD_DIM // LANES
    half = n_slab * blk

    parts, sinks = [], []
    for hh in range(2):
        for g in range(n_slab):
            qs = q_ref[0, :, g * LANES:(g + 1) * LANES]
            parts.append(jnp.where(low if hh == 0 else jnp.logical_not(low), qs, jnp.zeros_like(qs)))
            head = hh * n_slab + g
            sinks.append(jnp.broadcast_to(sink_ref[head:head + 1, 0:1] * LOG2E, (blk, 1)))
    q_all = jnp.concatenate(parts, axis=0)
    sink = jnp.concatenate(sinks, axis=0)

    def with_ones(v, hh):
        v_low = _iota(v.shape, 1) < SWA_HEAD_DIM
        return jnp.where(v_low if hh == 0 else jnp.logical_not(v_low), v, jnp.ones_like(v))

    def attend(loc):
        kc = k_ref[0, 0:n_ctx, :]
        vc = v_ref[0, 0:n_ctx, :]
        s_c = _mm_nt(q_all, kc)
        mx = jnp.maximum(jnp.max(s_c, axis=-1, keepdims=True), sink)
        if loc is not None:
            kl, vl, valid = loc
            s_l = jnp.where(valid[None], _mm_nt(q_all, kl).reshape(2 * n_slab, blk, valid.shape[-1]), NEG_INF)
            s_l = s_l.reshape(2 * n_slab * blk, valid.shape[-1])
            mx = jnp.maximum(mx, jnp.max(s_l, axis=-1, keepdims=True))
        p_c = jnp.exp2(s_c - mx).astype(BF16)
        p_sink = jnp.exp2(sink - mx)
        if loc is not None:
            p_l = jnp.exp2(s_l - mx).astype(BF16)
        outs = []
        for hh in range(2):
            rows = slice(hh * half, (hh + 1) * half)
            acc = _mm(p_c[rows], with_ones(vc, hh))
            if loc is not None:
                acc = acc + _mm(p_l[rows], with_ones(vl, hh))
            den = (acc[:, SWA_HEAD_DIM:SWA_HEAD_DIM + 1] if hh == 0 else acc[:, 0:1]) + p_sink[rows]
            outs.append(acc / den)
        for g in range(n_slab):
            rows = slice(g * blk, (g + 1) * blk)
            o_ref[0, :, g * LANES:(g + 1) * LANES] = jnp.where(low, outs[0][rows], outs[1][rows]).astype(o_ref.dtype)

    @pl.when(i < nctt)
    def _():
        if skip_ctx:
            o_ref[...] = jnp.zeros_like(o_ref)
        else:
            attend(None)

    @pl.when(i >= nctt)
    def _():
        n = i - nctt
        span = blk + 2 * WINDOW
        start = jnp.clip(n * blk - WINDOW, 0, n_lat - span)
        row0 = pl.multiple_of(n_ctx + start, LANES)
        kl = k_ref[0, pl.ds(row0, span), :]
        vl = v_ref[0, pl.ds(row0, span), :]
        kpos = start + _iota((blk, span), 1)
        qpos = n * blk + _iota((blk, span), 0)
        valid = jnp.abs(kpos - qpos) <= WINDOW
        attend((kl, vl, valid))


def _swa(sink, sq, sk, sv, n_ctx, skip_ctx):
    b, t, w = sq.shape
    blk = SWA_TILE
    nt = t // blk
    return pl.pallas_call(
        functools.partial(_swa_kernel, n_ctx=n_ctx, skip_ctx=skip_ctx),
        grid=(b, nt),
        in_specs=[pl.BlockSpec(sink.shape, lambda i, n: (0, 0)),
                  pl.BlockSpec((1, blk, w), lambda i, n: (i, n, 0)),
                  pl.BlockSpec((1, t, LANES), lambda i, n: (i, 0, 0)),
                  pl.BlockSpec((1, t, LANES), lambda i, n: (i, 0, 0))],
        out_specs=pl.BlockSpec((1, blk, w), lambda i, n: (i, n, 0)),
        out_shape=jax.ShapeDtypeStruct((b, t, w), BF16),
        compiler_params=_cparams(("parallel", "parallel")),
        name="swa",
    )(sink, sq, sk, sv)


def _block_diag(x, bd_mask):
    return jnp.where(bd_mask, jnp.concatenate([x] * PACK, axis=0), 0.0)


def _gdn_prep_kernel(qkv_ref, prev_ref, next_ref, ab_ref, cw_ref, gp_ref,
                     a_ref, rv_ref, rk_ref, qk_ref, qe_ref, ke_ref, dl_ref, *, seg_starts, seg_ends):
    ti = pl.program_id(1)
    tm = qkv_ref.shape[1]
    hw = GDN_HEADS * GDN_DK
    x = qkv_ref[0]
    row = _iota((tm, 1), 0)
    is_start = functools.reduce(jnp.logical_or, [ti == s for s in seg_starts])
    is_end = functools.reduce(jnp.logical_or, [ti == s for s in seg_ends])
    prev_row = jnp.where(is_start, 0.0, prev_ref[0, SUBLANES - 1:SUBLANES, :])
    next_row = jnp.where(is_end, 0.0, next_ref[0, 0:1, :])
    x_prev = jnp.where(row == 0, prev_row, pltpu.roll(x, 1, 0))
    x_next = jnp.where(row == tm - 1, next_row, pltpu.roll(x, tm - 1, 0))
    y = _silu(x_prev * cw_ref[0:1, :] + x * cw_ref[1:2, :] + x_next * cw_ref[2:3, :])

    ones_blk = _group_ones(LANES, GDN_DK).astype(BF16)

    def l2(slab):
        return slab * lax.rsqrt(_group_sum(slab * slab, ones_blk) + EPS)

    q = jnp.concatenate([l2(y[:, s * LANES:(s + 1) * LANES]) for s in range(hw // LANES)], axis=1)
    q = q * (GDN_DK ** -0.5)
    k = jnp.concatenate([l2(y[:, hw + s * LANES:hw + (s + 1) * LANES]) for s in range(hw // LANES)], axis=1)
    v = y[:, 2 * hw:3 * hw]

    ab = ab_ref[0]
    g_all = -jnp.exp(gp_ref[0:1, :]) * _softplus(ab + gp_ref[1:2, :])
    beta_all = _sigmoid(ab)
    ri = _iota((tm, tm), 0)
    ci = _iota((tm, tm), 1)
    same = jnp.right_shift(ri, 6) == jnp.right_shift(ci, 6)
    g_parts = jnp.concatenate(_split_bf16(g_all, 3), axis=1)

    def cumsum(tri):
        r = jnp.dot(jnp.logical_and(same, tri).astype(BF16), g_parts, preferred_element_type=F32)
        return r[:, 0:LANES] + r[:, LANES:2 * LANES] + r[:, 2 * LANES:3 * LANES]

    gam_all = jnp.where(_iota((tm, LANES), 1) < GDN_HEADS, cumsum(ci <= ri), cumsum(ci >= ri))
    col_head = jnp.right_shift(_iota((LANES, 2 * hw), 1), GDN_DK.bit_length() - 1)
    src = _iota((LANES, 2 * hw), 0)

    def expand(x, parts, sel):
        r = jnp.dot(jnp.concatenate(_split_bf16(x, parts), axis=0), sel.astype(BF16), preferred_element_type=F32)
        return functools.reduce(lambda a, b: a + b, [r[p * tm:(p + 1) * tm] for p in range(parts)])

    gam_exp = expand(gam_all, 3, src == col_head)
    b_exp = expand(beta_all, 2, src == col_head + 2 * GDN_HEADS)
    gam_f, gam_b = gam_exp[:, 0:hw], gam_exp[:, hw:2 * hw]

    c = GDN_CHUNK
    r64 = _iota((c, PACK_W), 0)
    c64 = jnp.bitwise_and(_iota((c, PACK_W), 1), c - 1)
    eye = (r64 == c64).astype(F32)
    bd_mask = jnp.right_shift(_iota((PACK_W, PACK_W), 0), 6) == jnp.right_shift(_iota((PACK_W, PACK_W), 1), 6)

    for ch in range(tm // c):
        rows = slice(ch * c, (ch + 1) * c)
        for gi in range(hw // PACK_W):
            lanes = slice(gi * PACK_W, (gi + 1) * PACK_W)
            kc, qc, vc = k[rows, lanes], q[rows, lanes], v[rows, lanes]
            kbd = _block_diag(kc, bd_mask)
            kq = _mm_nt(jnp.concatenate([kc, qc], axis=0), kbd)
            kk, qkm = kq[0:c], kq[c:2 * c]
            for d in range(2):
                gam = (gam_f if d == 0 else gam_b)[rows, lanes]
                beta = b_exp[rows, d * hw + gi * PACK_W:d * hw + (gi + 1) * PACK_W]
                gam_row = jnp.sum(gam * eye, axis=0, keepdims=True)
                decay = jnp.exp(jnp.minimum(gam - gam_row, 0.0))
                strict = (r64 > c64) if d == 0 else (r64 < c64)
                incl = (r64 >= c64) if d == 0 else (r64 <= c64)
                a_mat = jnp.where(strict, beta * kk * decay, 0.0)
                for pp in range(PACK_W // LANES):
                    a_ref[d, ch * (hw // LANES) + gi * (PACK_W // LANES) + pp] = \
                        a_mat[:, pp * LANES:(pp + 1) * LANES]
                e_gam = jnp.exp(gam)
                g_last = gam[c - 1:c, :] if d == 0 else gam[0:1, :]
                out = slice(d * hw + gi * PACK_W, d * hw + (gi + 1) * PACK_W)
                rv_ref[0, rows, out] = (vc * beta).astype(rv_ref.dtype)
                rk_ref[0, rows, out] = (kc * beta * e_gam).astype(rk_ref.dtype)
                qk_ref[0, rows, out] = jnp.where(incl, qkm * decay, 0.0).astype(qk_ref.dtype)
                qe_ref[0, rows, out] = (qc * e_gam).astype(qe_ref.dtype)
                ke_ref[0, rows, out] = (kc * jnp.exp(g_last - gam)).astype(ke_ref.dtype)
                dl_ref[0, ch, :, out] = jnp.exp(g_last)


def _gdn_prep(qkv, ab, cw, gp, seg_starts, seg_ends):
    b, t, w = qkv.shape
    tm = TOKEN_TILE
    nt = t // tm
    hb = tm // SUBLANES
    nh = t // SUBLANES
    wide = 2 * GDN_HEADS * GDN_DK
    c = GDN_CHUNK
    cpt = tm // c
    spt = cpt * (GDN_HEADS * GDN_DK // LANES)
    tok = lambda width: pl.BlockSpec((1, tm, width), lambda i, j: (i, j, 0))
    out_shape = [jax.ShapeDtypeStruct((2, b * nt * spt, c, LANES), F32)] + \
                [jax.ShapeDtypeStruct((b, t, wide), BF16)] * 5 + \
                [jax.ShapeDtypeStruct((b, t // c, 1, wide), F32)]
    return pl.pallas_call(
        functools.partial(_gdn_prep_kernel, seg_starts=seg_starts, seg_ends=seg_ends),
        grid=(b, nt),
        in_specs=[tok(w),
                  pl.BlockSpec((1, SUBLANES, w), lambda i, j: (i, jnp.maximum(j * hb - 1, 0), 0)),
                  pl.BlockSpec((1, SUBLANES, w), lambda i, j: (i, jnp.minimum((j + 1) * hb, nh - 1), 0)),
                  tok(LANES),
                  pl.BlockSpec(cw.shape, lambda i, j: (0, 0)),
                  pl.BlockSpec(gp.shape, lambda i, j: (0, 0))],
        out_specs=[pl.BlockSpec((2, spt, c, LANES), lambda i, j: (0, i * nt + j, 0, 0))] +
                  [tok(wide)] * 5 + [pl.BlockSpec((1, cpt, 1, wide), lambda i, j: (i, j, 0, 0))],
        out_shape=out_shape,
        compiler_params=_cparams(("parallel", "parallel")),
        name="gdn_prep",
    )(qkv, qkv, qkv, ab, cw, gp)


def _gdn_solve_kernel(a_ref, o_ref, at_ref, x_ref, ot_ref):
    c = GDN_CHUNK
    sub = SUBLANES
    nb = c // sub
    a_rows = pltpu.einshape("sil->isl", a_ref[0])
    for i in range(c):
        at_ref[i * LANES:(i + 1) * LANES, :] = a_rows[i].T
    x_ref[...] = jnp.zeros_like(x_ref)
    sub_iota = _iota((sub, LANES), 0)

    def solve(upper):
        for step in range(nb):
            ib = (nb - 1 - step) if upper else step
            kbs = range(ib, nb) if upper else range(ib + 1)

            def row(r, carry, ib=ib, kbs=kbs):
                i = ib * sub + ((sub - 1 - r) if upper else r)
                base = pl.multiple_of(i * LANES, LANES)
                accs = {(h2, jb): (sub_iota + jb * sub == i).astype(F32)
                        for h2 in range(LANES // c) for jb in range(nb)}
                for kb in kbs:
                    for h2 in range(LANES // c):
                        coefs = [at_ref[pl.ds(base + h2 * c + kb * sub + kk, 1), :] for kk in range(sub)]
                        for jb in (range(kb, nb) if upper else range(kb + 1)):
                            acc = accs[(h2, jb)]
                            for kk in range(sub):
                                lo = (kb * sub + kk) * LANES + h2 * c + jb * sub
                                acc = acc - coefs[kk] * x_ref[lo:lo + sub, :]
                            accs[(h2, jb)] = acc
                for h2 in range(LANES // c):
                    for jb in range(nb):
                        x_ref[pl.ds(base + h2 * c + jb * sub, sub), :] = accs[(h2, jb)]
                return carry

            lax.fori_loop(0, sub, row, 0)

    @pl.when(pl.program_id(0) == 0)
    def _():
        solve(False)

    @pl.when(pl.program_id(0) == 1)
    def _():
        solve(True)

    for i in range(c):
        ot_ref[i] = x_ref[i * LANES:(i + 1) * LANES, :].T
    o_ref[0] = pltpu.einshape("isl->sil", ot_ref[...])


def _gdn_solve(a):
    _, n_slab, c, _ = a.shape
    assert n_slab % LANES == 0
    blk = pl.BlockSpec((1, LANES, c, LANES), lambda d, g: (d, g, 0, 0))
    return pl.pallas_call(
        _gdn_solve_kernel,
        grid=(2, n_slab // LANES),
        in_specs=[blk],
        out_specs=blk,
        out_shape=jax.ShapeDtypeStruct(a.shape, F32),
        scratch_shapes=[pltpu.VMEM((c * LANES, LANES), F32), pltpu.VMEM((c * LANES, LANES), F32),
                        pltpu.VMEM((c, LANES, LANES), F32)],
        compiler_params=_cparams(("parallel", "parallel")),
        name="gdn_solve",
    )(a)


def _gdn_scan_kernel(tf, rvf, rkf, qkf, qef, kef, dlf, tb, rvb, rkb, qkb, qeb, keb, dlb, of_ref, ob_ref, s_ref):
    c = GDN_CHUNK
    cpt = rvf.shape[1] // c
    ng = rvf.shape[2] // PACK_W
    ppg = PACK_W // LANES
    bd_mask = jnp.right_shift(_iota((PACK_W, PACK_W), 0), 6) == jnp.right_shift(_iota((PACK_W, PACK_W), 1), 6)

    @pl.when(pl.program_id(1) == 0)
    def _():
        s_ref[...] = jnp.zeros_like(s_ref)

    chains = ((tf, rvf, rkf, qkf, qef, kef, dlf, of_ref), (tb, rvb, rkb, qkb, qeb, keb, dlb, ob_ref))
    for j in range(cpt):
        where = []
        for d in range(2):
            ch = j if d == 0 else cpt - 1 - j
            for gi in range(ng):
                where.append((d, gi, ch, slice(ch * c, (ch + 1) * c), slice(gi * PACK_W, (gi + 1) * PACK_W)))
        tinvs = [jnp.concatenate([chains[d][0][0, ch * ng * ppg + gi * ppg + pp] for pp in range(ppg)],
                                 axis=1).astype(BF16) for d, gi, ch, rows, lanes in where]
        us = [jnp.dot(t16, _block_diag(chains[d][1][0, rows, lanes], bd_mask), preferred_element_type=F32)
              for (d, gi, ch, rows, lanes), t16 in zip(where, tinvs)]
        ws = [jnp.dot(t16, _block_diag(chains[d][2][0, rows, lanes], bd_mask), preferred_element_type=F32)
              for (d, gi, ch, rows, lanes), t16 in zip(where, tinvs)]
        states = [s_ref[d * ng + gi] for d, gi, _, _, _ in where]
        wqs = [_mm(jnp.concatenate([w.astype(BF16), chains[d][4][0, rows, lanes]], axis=0), st)
               for (d, gi, ch, rows, lanes), w, st in zip(where, ws, states)]
        v_news = [u - wq[0:c] for u, wq in zip(us, wqs)]
        kvs = [_mm_tn(chains[d][5][0, rows, lanes], v_new) for (d, gi, ch, rows, lanes), v_new in zip(where, v_news)]
        for (d, gi, ch, rows, lanes), st, kv in zip(where, states, kvs):
            s_ref[d * ng + gi] = st * chains[d][6][0, ch, :, lanes] + jnp.where(bd_mask, kv, 0.0)
        for (d, gi, ch, rows, lanes), wq, v_new in zip(where, wqs, v_news):
            chains[d][7][0, rows, lanes] = wq[c:2 * c] + _mm(chains[d][3][0, rows, lanes],
                                                             _block_diag(v_new, bd_mask))


def _gdn_scan(tinv, rv, rk, qk, qe, ke, dl, n_ctx):
    b, t, wide = rv.shape
    tm = TOKEN_TILE
    nt = t // tm
    nctt = n_ctx // tm
    hw = wide // 2
    c = GDN_CHUNK
    cpt = tm // c
    spt = cpt * (hw // LANES)

    def mirror(g):
        return jnp.where(g < nctt, nctt - 1 - g, nt - 1 - (g - nctt))

    fwd = pl.BlockSpec((1, tm, hw), lambda i, g: (i, g, 0))
    bwd = pl.BlockSpec((1, tm, hw), lambda i, g: (i, mirror(g), 1))
    tfwd = pl.BlockSpec((1, spt, c, LANES), lambda i, g: (0, i * nt + g, 0, 0))
    tbwd = pl.BlockSpec((1, spt, c, LANES), lambda i, g: (1, i * nt + mirror(g), 0, 0))
    dfwd = pl.BlockSpec((1, cpt, 1, hw), lambda i, g: (i, g, 0, 0))
    dbwd = pl.BlockSpec((1, cpt, 1, hw), lambda i, g: (i, mirror(g), 0, 1))
    out_sds = jax.ShapeDtypeStruct((b, t, hw), F32)
    return pl.pallas_call(
        _gdn_scan_kernel,
        grid=(b, nt),
        in_specs=[tfwd] + [fwd] * 5 + [dfwd] + [tbwd] + [bwd] * 5 + [dbwd],
        out_specs=[pl.BlockSpec((1, tm, hw), lambda i, g: (i, g, 0)),
                   pl.BlockSpec((1, tm, hw), lambda i, g: (i, mirror(g), 0))],
        out_shape=[out_sds, out_sds],
        scratch_shapes=[pltpu.VMEM((2 * hw // PACK_W, PACK_W, PACK_W), F32)],
        compiler_params=_cparams(("parallel", "arbitrary")),
        name="gdn_scan",
    )(tinv, rv, rk, qk, qe, ke, dl, tinv, rv, rk, qk, qe, ke, dl)


def _merge_kernel(x_ref, mod_ref, ya_ref, yb_ref, of_ref, ob_ref, z_ref, gate_ref, on_ref,
                  wpa_ref, wpb_ref, wpc_ref, wout_ref, o_ref):
    x = x_ref[0]
    d = x.shape[-1]
    m = mod_ref[0]
    oc = of_ref[0] + ob_ref[0]
    ones_blk = _group_ones(LANES, GDN_DV).astype(BF16)
    ms = jnp.concatenate(
        [_group_sum(oc[:, s * LANES:(s + 1) * LANES] ** 2, ones_blk) for s in range(oc.shape[-1] // LANES)],
        axis=1) * (1.0 / GDN_DV)
    yc = oc * lax.rsqrt(ms + EPS) * on_ref[...] * _silu(z_ref[0].astype(F32))
    gate = gate_ref[0].astype(F32)
    mix = (_sigmoid(gate[:, 0:d]) * _mm(ya_ref[0], wpa_ref[...])
           + _sigmoid(gate[:, d:2 * d]) * _mm(yb_ref[0], wpb_ref[...])
           + _sigmoid(gate[:, 2 * d:3 * d]) * _mm(yc, wpc_ref[...]))
    o_ref[0] = x + m[:, 2 * d:3 * d] * _mm(mix, wout_ref[...])


def _merge(xs, mod_l, ya, yb, oc_f, oc_b, z, gate, onorm, wpa, wpb, wpc, wout, nctt, ctx_row, t_off):
    b, t, d = xs.shape
    tm = TOKEN_TILE
    nt = t // tm - t_off
    tok = lambda width: pl.BlockSpec((1, tm, width), lambda i, j: (i, j + t_off, 0))
    full = lambda a: pl.BlockSpec(a.shape, lambda i, j: (0,) * a.ndim)
    mod_spec = pl.BlockSpec((1, 1, mod_l.shape[-1]),
                            lambda i, j: (jnp.where(j + t_off < nctt, ctx_row, i), 0, 0))
    return pl.pallas_call(
        _merge_kernel,
        grid=(b, nt),
        in_specs=[tok(d), mod_spec, tok(512), tok(512), tok(512), tok(512), tok(512), tok(3 * d),
                  full(onorm), full(wpa), full(wpb), full(wpc), full(wout)],
        out_specs=pl.BlockSpec((1, tm, d), lambda i, j: (i, j, 0)),
        out_shape=jax.ShapeDtypeStruct((b, nt * tm, d), F32),
        compiler_params=_cparams(("parallel", "parallel")),
        name="merge",
    )(xs, mod_l, ya, yb, oc_f, oc_b, z, gate, onorm, wpa, wpb, wpc, wout)


def _ffn_kernel(x_ref, prev_ref, next_ref, mod_ref, g2_ref, wup_ref, cw_ref, wdn_ref, nf_ref, o_ref,
                *, seg_starts, seg_ends, final):
    ti = pl.program_id(1)
    x = x_ref[0]
    tm, d = x.shape
    m = mod_ref[0]
    xall = jnp.concatenate([prev_ref[0], x, next_ref[0]], axis=0)
    xa = _rms(xall, g2_ref[...]) * (1.0 + m[:, 4 * d:5 * d]) + m[:, 3 * d:4 * d]
    n_all = tm + 2 * SUBLANES
    row = _iota((n_all, 1), 0)
    is_start = functools.reduce(jnp.logical_or, [ti == s for s in seg_starts])
    is_end = functools.reduce(jnp.logical_or, [ti == s for s in seg_ends])
    drop = jnp.logical_or(jnp.logical_and(row == SUBLANES - 1, is_start),
                          jnp.logical_and(row == SUBLANES + tm, is_end))
    xa = jnp.where(drop, 0.0, xa).astype(BF16)

    def conv(h, cols):
        hp = pltpu.roll(h, 1, 0)[SUBLANES:SUBLANES + tm]
        hn = pltpu.roll(h, n_all - 1, 0)[SUBLANES:SUBLANES + tm]
        return (hp * cw_ref[0:1, cols] + h[SUBLANES:SUBLANES + tm] * cw_ref[1:2, cols]
                + hn * cw_ref[2:3, cols])

    cwid = FFN_DIM // FFN_SPLIT
    acc = jnp.zeros((tm, d), F32)
    for j in range(FFN_SPLIT):
        ca = slice(j * cwid, (j + 1) * cwid)
        cb = slice(FFN_DIM + j * cwid, FFN_DIM + (j + 1) * cwid)
        ha = conv(jnp.dot(xa, wup_ref[:, ca], preferred_element_type=F32), ca)
        hb = conv(jnp.dot(xa, wup_ref[:, cb], preferred_element_type=F32), cb)
        acc = acc + _mm(_silu(ha) * hb, wdn_ref[ca, :])
    y = x + m[:, 5 * d:6 * d] * acc
    if final:
        y = _rms(y, nf_ref[...])
    o_ref[0] = y


def _ffn(x1, mod_l, g2, wup, cw, wdn, nf, nctt_mod, ctx_row, seg_starts, seg_ends, final):
    b, t, d = x1.shape
    tm = TOKEN_TILE
    nt = t // tm
    hb = tm // SUBLANES
    nh = t // SUBLANES
    full = lambda a: pl.BlockSpec(a.shape, lambda i, j: (0,) * a.ndim)
    mod_spec = pl.BlockSpec((1, 1, mod_l.shape[-1]),
                            lambda i, j: (jnp.where(j < nctt_mod, ctx_row, i), 0, 0))
    return pl.pallas_call(
        functools.partial(_ffn_kernel, seg_starts=seg_starts, seg_ends=seg_ends, final=final),
        grid=(b, nt),
        in_specs=[pl.BlockSpec((1, tm, d), lambda i, j: (i, j, 0)),
                  pl.BlockSpec((1, SUBLANES, d), lambda i, j: (i, jnp.maximum(j * hb - 1, 0), 0)),
                  pl.BlockSpec((1, SUBLANES, d), lambda i, j: (i, jnp.minimum((j + 1) * hb, nh - 1), 0)),
                  mod_spec, full(g2), full(wup), full(cw), full(wdn), full(nf)],
        out_specs=pl.BlockSpec((1, tm, d), lambda i, j: (i, j, 0)),
        out_shape=jax.ShapeDtypeStruct((b, t, d), F32),
        compiler_params=_cparams(("parallel", "parallel")),
        name="ffn",
    )(x1, x1, x1, mod_l, g2, wup, cw, wdn, nf)


def _cols(w, segments):
    parts = [jnp.zeros(w.shape[:-1] + (seg,), w.dtype) if isinstance(seg, int) else w[..., seg[0]:seg[1]]
             for seg in segments]
    return jnp.concatenate(parts, axis=-1)


def _win_segments(d):
    widths = [MLA_Q_RANK, MLA_KV_RANK, MLA_ROPE, SWA_HEADS * SWA_HEAD_DIM, SWA_KV_HEADS * SWA_HEAD_DIM,
              SWA_KV_HEADS * SWA_HEAD_DIM, GDN_HEADS * (2 * GDN_DK + GDN_DV), GDN_HEADS * GDN_DV,
              2 * GDN_HEADS, 2 * GDN_HEADS, 3 * d]
    off = np.cumsum([0] + widths)
    cq, ckv, kr, sq, sk, sv, qkv, z, a, b, gate = [(int(off[i]), int(off[i + 1])) for i in range(11)]
    sq_perm = [(sq[0] + h * SWA_HEAD_DIM, sq[0] + (h + 1) * SWA_HEAD_DIM) for h in _swa_head_order()]
    segs = [cq, ckv, MLA_NOPE, kr, LANES - MLA_NOPE - MLA_ROPE] + sq_perm + \
           [sk, sv, qkv, z, a, b, LANES - 4 * GDN_HEADS, gate]
    return segs


def _swa_head_order():
    group = SWA_HEADS // SWA_KV_HEADS
    return [kv * group + g for g in range(group) for kv in range(SWA_KV_HEADS)]


def _rope_tables(n_ctx, n_lat):
    t = np.arange(n_lat)
    pos = np.stack([t // GRID_W, t % GRID_W]).astype(np.float32)

    def build(head_w, rope_lo, rope_w):
        da = rope_w // 2
        half = da // 2
        inv = ROPE_THETA ** (-np.arange(half, dtype=np.float32) / half)
        c = np.ones((n_lat, LANES), np.float32)
        sa = np.zeros((n_lat, LANES), np.float32)
        sb = np.zeros((n_lat, LANES), np.float32)
        for lane in range(LANES):
            dd = lane % head_w - rope_lo
            if dd < 0 or dd >= rope_w:
                continue
            axis, r = dd // da, dd % da
            ang = pos[axis] * inv[r % half]
            c[:, lane] = np.cos(ang)
            if r < half:
                sa[:, lane] = -np.sin(ang)
            else:
                sb[:, lane] = np.sin(ang)
        ctx = [np.ones((n_ctx, LANES), np.float32), np.zeros((n_ctx, LANES), np.float32),
               np.zeros((n_ctx, LANES), np.float32)]
        return [jnp.asarray(np.concatenate([cx, lt])) for cx, lt in zip(ctx, (c, sa, sb))]

    return build(LANES, MLA_NOPE, MLA_ROPE) + build(SWA_HEAD_DIM, 0, SWA_HEAD_DIM)


def kernel(x, c, ctx, c_ctx, w_mod, b_mod, norm1, norm2, w_in, mla_q_norm, mla_kv_norm, w_uq, w_ukv,
           swa_sink, gdn_conv, gdn_a_log, gdn_dt_bias, gdn_norm, w_branch_a, w_branch_b, w_branch_c,
           w_out, ffn_up, ffn_conv, ffn_down, norm_f):
    b, s, d = x.shape
    n_ctx = ctx.shape[1]
    depth = w_mod.shape[0]
    tm = TOKEN_TILE
    assert n_ctx % tm == 0 and s % tm == 0 and s >= SWA_TILE + 2 * WINDOW and WINDOW % LANES == 0
    nctt = n_ctx // tm
    ntt = (n_ctx + s) // tm

    rows = -(-(b + 1) // SUBLANES) * SUBLANES
    cc = jnp.zeros((rows, d), F32).at[0:b].set(c).at[b].set(c_ctx)
    mod = _modulation(cc, w_mod, b_mod).reshape(depth, rows, 1, 6 * d)

    hd_q = MLA_NOPE + MLA_ROPE
    hd_kv = MLA_NOPE + MLA_V
    uq_segs = [s for h in range(MLA_HEADS) for s in ((h * hd_q, (h + 1) * hd_q), LANES - hd_q)]
    uk_segs = [s for h in range(MLA_HEADS) for s in ((h * hd_kv, h * hd_kv + MLA_NOPE), LANES - MLA_NOPE)]
    uv_segs = [(h * hd_kv + MLA_NOPE, (h + 1) * hd_kv) for h in range(MLA_HEADS)]
    win_segs = _win_segments(d)

    def layer_weights(l):
        wpb = jnp.concatenate([w_branch_b[l, h * SWA_HEAD_DIM:(h + 1) * SWA_HEAD_DIM] for h in _swa_head_order()],
                              axis=0)
        gpar = jnp.zeros((SUBLANES, LANES), F32)
        gpar = gpar.at[0, 0:2 * GDN_HEADS].set(gdn_a_log[l].reshape(-1))
        gpar = gpar.at[1, 0:2 * GDN_HEADS].set(gdn_dt_bias[l].reshape(-1))
        pad_rows = lambda a: jnp.pad(a, ((0, SUBLANES - a.shape[0]), (0, 0)))
        return dict(
            win=_cols(w_in[l], win_segs).astype(BF16),
            wuq=_cols(w_uq[l], uq_segs).astype(BF16),
            wukvk=_cols(w_ukv[l], uk_segs).astype(BF16),
            wukvv=_cols(w_ukv[l], uv_segs).astype(BF16),
            wpa=w_branch_a[l].astype(BF16), wpb=wpb.astype(BF16), wpc=w_branch_c[l].astype(BF16),
            wout=w_out[l].astype(BF16), wup=ffn_up[l].astype(BF16), wdn=ffn_down[l].astype(BF16),
            sink=jnp.broadcast_to(swa_sink[l][:, None], (SWA_HEADS, LANES)),
            onorm=jnp.tile(gdn_norm[l], GDN_HEADS).reshape(1, -1),
            gconv=pad_rows(gdn_conv[l]), fconv=pad_rows(ffn_conv[l]), gpar=gpar)

    rope_tabs = _rope_tables(n_ctx, s)

    xs = jnp.concatenate([ctx, x], axis=1)
    row2 = lambda a: a.reshape(1, -1)
    for l in range(depth):
        last = l == depth - 1
        t_off = nctt if last else 0
        p = layer_weights(l)
        qm, km, vm, sq, sk, sv, qkv, z, ab, gate = _proj(
            xs, mod[l], row2(norm1[l]), p["win"], row2(mla_q_norm[l]), row2(mla_kv_norm[l]),
            p["wuq"], p["wukvk"], p["wukvv"], rope_tabs, nctt, b)
        ya = _mla(qm, km, vm, n_ctx, last)
        yb = _swa(p["sink"], sq, sk, sv, n_ctx, last)
        a_mat, rv, rk, qk, qe, ke, dl = _gdn_prep(qkv, ab, p["gconv"], p["gpar"], (0, nctt), (nctt - 1, ntt - 1))
        oc_f, oc_b = _gdn_scan(_gdn_solve(a_mat), rv, rk, qk, qe, ke, dl, n_ctx)
        x1 = _merge(xs, mod[l], ya, yb, oc_f, oc_b, z, gate, p["onorm"], p["wpa"], p["wpb"], p["wpc"], p["wout"],
                    nctt, b, t_off)
        if last:
            xs = _ffn(x1, mod[l], row2(norm2[l]), p["wup"], p["fconv"], p["wdn"], row2(norm_f),
                      0, b, (0,), (s // tm - 1,), True)
        else:
            xs = _ffn(x1, mod[l], row2(norm2[l]), p["wup"], p["fconv"], p["wdn"], row2(norm_f),
                      nctt, b, (0, nctt), (nctt - 1, ntt - 1), False)
    return xs
```

```python
import functools

import numpy as np
import jax
import jax.numpy as jnp
from jax import lax
from jax.experimental import pallas as pl
from jax.experimental.pallas import tpu as pltpu

F32 = jnp.float32
BF16 = jnp.bfloat16

GRID_W = 64
ROPE_THETA = 10000.0
NEG_INF = -1e30
EPS = 1e-6
LOG2E = 1.4426950408889634

MLA_HEADS = 8
MLA_Q_RANK = 384
MLA_KV_RANK = 256
MLA_NOPE = 64
MLA_ROPE = 32
MLA_V = 64

SWA_HEADS = 8
SWA_KV_HEADS = 2
SWA_HEAD_DIM = 64
WINDOW = 128

GDN_HEADS = 8
GDN_DK = 64
GDN_DV = 64
GDN_CHUNK = 64

FFN_DIM = 2816

LANES = 128
SUBLANES = 8
VMEM_LIMIT_BYTES = 56 * 1024 * 1024

TOKEN_TILE = 256
SWA_TILE = 128
MLA_Q_TILE = 1024
PACK = 4
PACK_W = PACK * GDN_DK
FFN_SPLIT = 1


def _sigmoid(x):
    return 0.5 * jnp.tanh(0.5 * x) + 0.5


def _silu(x):
    return x * _sigmoid(x)


def _softplus(x):
    return jnp.maximum(x, 0.0) + jnp.log(1.0 + jnp.exp(-jnp.abs(x)))


def _rms(x, g):
    return x * lax.rsqrt(jnp.mean(x * x, axis=-1, keepdims=True) + EPS) * g


def _mm(a, b):
    return jnp.dot(a.astype(BF16), b.astype(BF16), preferred_element_type=F32)


def _mm_nt(a, b):
    return lax.dot_general(a.astype(BF16), b.astype(BF16), (((1,), (1,)), ((), ())),
                           preferred_element_type=F32)


def _mm_tn(a, b):
    return lax.dot_general(a.astype(BF16), b.astype(BF16), (((0,), (0,)), ((), ())),
                           preferred_element_type=F32)


def _split_bf16(x, parts):
    out = []
    for _ in range(parts - 1):
        hi = x.astype(BF16)
        out.append(hi)
        x = x - hi.astype(F32)
    out.append(x.astype(BF16))
    return out


def _group_sum(x, ones_blk):
    n = x.shape[0]
    r = jnp.dot(jnp.concatenate(_split_bf16(x, 2), axis=0), ones_blk, preferred_element_type=F32)
    return r[0:n] + r[n:2 * n]


def _iota(shape, dim):
    return lax.broadcasted_iota(jnp.int32, shape, dim)


def _group_ones(n, group):
    sh = group.bit_length() - 1
    return (jnp.right_shift(_iota((n, n), 0), sh) == jnp.right_shift(_iota((n, n), 1), sh)).astype(F32)


def _rope(x, c, sa, sb, half):
    n = x.shape[-1]
    return x * c + pltpu.roll(x, n - half, 1) * sa + pltpu.roll(x, half, 1) * sb


def _cparams(sem):
    return pltpu.CompilerParams(dimension_semantics=sem, vmem_limit_bytes=VMEM_LIMIT_BYTES)


def _mod_kernel(c_ref, w_ref, b_ref, o_ref):
    s = _silu(c_ref[...])
    o_ref[0] = _mm(s, w_ref[0]) + b_ref[0]


def _modulation(cc, w_mod, b_mod):
    depth, d, n = w_mod.shape
    r = cc.shape[0]
    tn = 2048
    return pl.pallas_call(
        _mod_kernel,
        grid=(depth, n // tn),
        in_specs=[pl.BlockSpec((r, d), lambda l, j: (0, 0)),
                  pl.BlockSpec((1, d, tn), lambda l, j: (l, 0, j)),
                  pl.BlockSpec((1, 1, tn), lambda l, j: (l, 0, j))],
        out_specs=pl.BlockSpec((1, r, tn), lambda l, j: (l, 0, j)),
        out_shape=jax.ShapeDtypeStruct((depth, r, n), F32),
        compiler_params=_cparams(("parallel", "parallel")),
        name="modulation",
    )(cc, w_mod, b_mod.reshape(depth, 1, n))


_P_CQ = (0, 384)
_P_CKV = (384, 640)
_P_KR = (640, 768)
_P_SQ = (768, 1280)
_P_SK = (1280, 1408)
_P_SV = (1408, 1536)
_P_QKV = (1536, 3072)
_P_Z = (3072, 3584)
_P_AB = (3584, 3712)
_P_GATE = (3712, 6784)
_P_WIDTH = 6784


def _proj_kernel(x_ref, mod_ref, g1_ref, win_ref, qn_ref, kvn_ref, wuq_ref, wukvk_ref, wukvv_ref,
                 mc_ref, ma_ref, mb_ref, sc_ref, sa_ref, sb_ref,
                 qm_ref, km_ref, vm_ref, sq_ref, sk_ref, sv_ref, qkv_ref, z_ref, ab_ref, gate_ref):
    x = x_ref[0]
    d = x.shape[-1]
    m = mod_ref[0]
    xb = (_rms(x, g1_ref[...]) * (1.0 + m[:, d:2 * d]) + m[:, 0:d]).astype(BF16)

    def proj(piece):
        return jnp.dot(xb, win_ref[:, piece[0]:piece[1]], preferred_element_type=F32)

    mc, ma, mb = mc_ref[...], ma_ref[...], mb_ref[...]
    sc, sa, sb = sc_ref[...], sa_ref[...], sb_ref[...]

    cq = _rms(proj(_P_CQ), qn_ref[...])
    ckv = _rms(proj(_P_CKV), kvn_ref[...])

    gate_ref[0] = proj(_P_GATE).astype(gate_ref.dtype)
    qkv_ref[0] = proj(_P_QKV)
    z_ref[0] = proj(_P_Z).astype(z_ref.dtype)
    ab_ref[0] = proj(_P_AB)
    sv_ref[0] = proj(_P_SV).astype(sv_ref.dtype)

    q = _mm(cq, wuq_ref[...]) * ((MLA_NOPE + MLA_ROPE) ** -0.5 * LOG2E)
    kn = _mm(ckv, wukvk_ref[...])
    vm_ref[0] = _mm(ckv, wukvv_ref[...]).astype(vm_ref.dtype)
    kr = _rope(proj(_P_KR), mc, ma, mb, MLA_ROPE // 4)
    for h in range(MLA_HEADS):
        sl = slice(h * LANES, (h + 1) * LANES)
        qm_ref[0, :, sl] = _rope(q[:, sl], mc, ma, mb, MLA_ROPE // 4).astype(qm_ref.dtype)
        km_ref[0, :, sl] = (kn[:, sl] + kr).astype(km_ref.dtype)

    sq = proj(_P_SQ) * (SWA_HEAD_DIM ** -0.5 * LOG2E)
    for g in range(SWA_HEADS * SWA_HEAD_DIM // LANES):
        sl = slice(g * LANES, (g + 1) * LANES)
        sq_ref[0, :, sl] = _rope(sq[:, sl], sc, sa, sb, SWA_HEAD_DIM // 4).astype(sq_ref.dtype)
    sk_ref[0] = _rope(proj(_P_SK), sc, sa, sb, SWA_HEAD_DIM // 4).astype(sk_ref.dtype)


def _proj(xs, mod_l, g1, win, qn, kvn, wuq, wukvk, wukvv, rope_tabs, nctt, ctx_row):
    b, t, d = xs.shape
    tm = TOKEN_TILE
    nt = t // tm

    def tok(w):
        return pl.BlockSpec((1, tm, w), lambda i, j: (i, j, 0))

    def full(a):
        return pl.BlockSpec(a.shape, lambda i, j: (0,) * a.ndim)

    tab = pl.BlockSpec((tm, LANES), lambda i, j: (j, 0))
    mod_spec = pl.BlockSpec((1, 1, mod_l.shape[-1]),
                            lambda i, j: (jnp.where(j < nctt, ctx_row, i), 0, 0))
    widths = (1024, 1024, 512, 512, 128, 128, 1536, 512, 128, 3072)
    dtypes = (BF16, BF16, BF16, BF16, BF16, BF16, F32, BF16, F32, BF16)
    return pl.pallas_call(
        _proj_kernel,
        grid=(b, nt),
        in_specs=[tok(d), mod_spec, full(g1), full(win), full(qn), full(kvn), full(wuq),
                  full(wukvk), full(wukvv)] + [tab] * 6,
        out_specs=[tok(w) for w in widths],
        out_shape=[jax.ShapeDtypeStruct((b, t, w), dt) for w, dt in zip(widths, dtypes)],
        compiler_params=_cparams(("parallel", "parallel")),
        name="proj",
    )(xs, mod_l, g1, win, qn, kvn, wuq, wukvk, wukvv, *rope_tabs)


def _mla_kernel(q_ref, k_ref, v_ref, o_ref, *, n_ctx, q_tile, skip_ctx):
    n = pl.program_id(2)

    def run(row0, nq, nk):
        low = _iota((nq, LANES), 1) < MLA_V
        v = v_ref[0, 0:nk, :]
        v_low = _iota((nk, LANES), 1) < MLA_V
        one = jnp.ones_like(v)
        ss = [_mm_nt(q_ref[0, pl.ds(row0, nq), hh * LANES:(hh + 1) * LANES],
                     k_ref[0, 0:nk, hh * LANES:(hh + 1) * LANES]) for hh in range(2)]
        ps = [jnp.exp2(s - jnp.max(s, axis=-1, keepdims=True)) for s in ss]
        outs = []
        for hh in range(2):
            acc = _mm(ps[hh], jnp.where(v_low if hh == 0 else jnp.logical_not(v_low), v, one))
            den = acc[:, MLA_V:MLA_V + 1] if hh == 0 else acc[:, 0:1]
            outs.append(acc / den)
        o_ref[0, pl.ds(row0, nq), :] = jnp.where(low, outs[0], outs[1]).astype(o_ref.dtype)

    @pl.when(n == 0)
    def _():
        if skip_ctx:
            o_ref[0, 0:n_ctx, :] = jnp.zeros((n_ctx, LANES), o_ref.dtype)
        else:
            run(0, n_ctx, n_ctx)

    @pl.when(n > 0)
    def _():
        run(pl.multiple_of(n_ctx + (n - 1) * q_tile, LANES), q_tile, k_ref.shape[1])


def _mla(qm, km, vm, n_ctx, skip_ctx):
    b, t, _ = qm.shape
    q_tile = MLA_Q_TILE
    assert (t - n_ctx) % q_tile == 0
    steps = (t - n_ctx) // q_tile + 1
    return pl.pallas_call(
        functools.partial(_mla_kernel, n_ctx=n_ctx, q_tile=q_tile, skip_ctx=skip_ctx),
        grid=(b, MLA_HEADS // 2, steps),
        in_specs=[pl.BlockSpec((1, t, 2 * LANES), lambda i, j, n: (i, 0, j)),
                  pl.BlockSpec((1, t, 2 * LANES), lambda i, j, n: (i, 0, j)),
                  pl.BlockSpec((1, t, LANES), lambda i, j, n: (i, 0, j))],
        out_specs=pl.BlockSpec((1, t, LANES), lambda i, j, n: (i, 0, j)),
        out_shape=jax.ShapeDtypeStruct((b, t, MLA_HEADS * MLA_V), BF16),
        compiler_params=_cparams(("parallel", "parallel", "arbitrary")),
        name="mla",
    )(qm, km, vm)


def _swa_kernel(sink_ref, q_ref, k_ref, v_ref, o_ref, *, n_ctx, skip_ctx):
    i = pl.program_id(1)
    blk = q_ref.shape[1]
    t = k_ref.shape[1]
    n_lat = t - n_ctx
    nctt = n_ctx // blk
    low = _iota((blk, LANES), 1) < SWA_HEAD_DIM
    n_slab = SWA_HEADS * SWA_HEAD_DIM // LANES
    half = n_slab * blk

    parts, sinks = [], []
    for hh in range(2):
        for g in range(n_slab):
            qs = q_ref[0, :, g * LANES:(g + 1) * LANES]
            parts.append(jnp.where(low if hh == 0 else jnp.logical_not(low), qs, jnp.zeros_like(qs)))
            head = hh * n_slab + g
            sinks.append(jnp.broadcast_to(sink_ref[head:head + 1, 0:1] * LOG2E, (blk, 1)))
    q_all = jnp.concatenate(parts, axis=0)
    sink = jnp.concatenate(sinks, axis=0)

    def with_ones(v, hh):
        v_low = _iota(v.shape, 1) < SWA_HEAD_DIM
        return jnp.where(v_low if hh == 0 else jnp.logical_not(v_low), v, jnp.ones_like(v))

    def attend(loc):
        kc = k_ref[0, 0:n_ctx, :]
        vc = v_ref[0, 0:n_ctx, :]
        s_c = _mm_nt(q_all, kc)
        mx = jnp.maximum(jnp.max(s_c, axis=-1, keepdims=True), sink)
        if loc is not None:
            kl, vl, valid = loc
            s_l = jnp.where(valid[None], _mm_nt(q_all, kl).reshape(2 * n_slab, blk, valid.shape[-1]), NEG_INF)
            s_l = s_l.reshape(2 * n_slab * blk, valid.shape[-1])
            mx = jnp.maximum(mx, jnp.max(s_l, axis=-1, keepdims=True))
        p_c = jnp.exp2(s_c - mx).astype(BF16)
        p_sink = jnp.exp2(sink - mx)
        if loc is not None:
            p_l = jnp.exp2(s_l - mx).astype(BF16)
        outs = []
        for hh in range(2):
            rows = slice(hh * half, (hh + 1) * half)
            acc = _mm(p_c[rows], with_ones(vc, hh))
            if loc is not None:
                acc = acc + _mm(p_l[rows], with_ones(vl, hh))
            den = (acc[:, SWA_HEAD_DIM:SWA_HEAD_DIM + 1] if hh == 0 else acc[:, 0:1]) + p_sink[rows]
            outs.append(acc / den)
        for g in range(n_slab):
            rows = slice(g * blk, (g + 1) * blk)
            o_ref[0, :, g * LANES:(g + 1) * LANES] = jnp.where(low, outs[0][rows], outs[1][rows]).astype(o_ref.dtype)

    @pl.when(i < nctt)
    def _():
        if skip_ctx:
            o_ref[...] = jnp.zeros_like(o_ref)
        else:
            attend(None)

    @pl.when(i >= nctt)
    def _():
        n = i - nctt
        span = blk + 2 * WINDOW
        start = jnp.clip(n * blk - WINDOW, 0, n_lat - span)
        row0 = pl.multiple_of(n_ctx + start, LANES)
        kl = k_ref[0, pl.ds(row0, span), :]
        vl = v_ref[0, pl.ds(row0, span), :]
        kpos = start + _iota((blk, span), 1)
        qpos = n * blk + _iota((blk, span), 0)
        valid = jnp.abs(kpos - qpos) <= WINDOW
        attend((kl, vl, valid))


def _swa(sink, sq, sk, sv, n_ctx, skip_ctx):
    b, t, w = sq.shape
    blk = SWA_TILE
    nt = t // blk
    return pl.pallas_call(
        functools.partial(_swa_kernel, n_ctx=n_ctx, skip_ctx=skip_ctx),
        grid=(b, nt),
        in_specs=[pl.BlockSpec(sink.shape, lambda i, n: (0, 0)),
                  pl.BlockSpec((1, blk, w), lambda i, n: (i, n, 0)),
                  pl.BlockSpec((1, t, LANES), lambda i, n: (i, 0, 0)),
                  pl.BlockSpec((1, t, LANES), lambda i, n: (i, 0, 0))],
        out_specs=pl.BlockSpec((1, blk, w), lambda i, n: (i, n, 0)),
        out_shape=jax.ShapeDtypeStruct((b, t, w), BF16),
        compiler_params=_cparams(("parallel", "parallel")),
        name="swa",
    )(sink, sq, sk, sv)


def _block_diag(x, bd_mask):
    return jnp.where(bd_mask, jnp.concatenate([x] * PACK, axis=0), 0.0)


def _gdn_prep_kernel(qkv_ref, prev_ref, next_ref, ab_ref, cw_ref, gp_ref,
                     a_ref, rv_ref, rk_ref, qk_ref, qe_ref, ke_ref, dl_ref, *, seg_starts, seg_ends):
    ti = pl.program_id(1)
    tm = qkv_ref.shape[1]
    hw = GDN_HEADS * GDN_DK
    x = qkv_ref[0]
    row = _iota((tm, 1), 0)
    is_start = functools.reduce(jnp.logical_or, [ti == s for s in seg_starts])
    is_end = functools.reduce(jnp.logical_or, [ti == s for s in seg_ends])
    prev_row = jnp.where(is_start, 0.0, prev_ref[0, SUBLANES - 1:SUBLANES, :])
    next_row = jnp.where(is_end, 0.0, next_ref[0, 0:1, :])
    x_prev = jnp.where(row == 0, prev_row, pltpu.roll(x, 1, 0))
    x_next = jnp.where(row == tm - 1, next_row, pltpu.roll(x, tm - 1, 0))
    y = _silu(x_prev * cw_ref[0:1, :] + x * cw_ref[1:2, :] + x_next * cw_ref[2:3, :])

    ones_blk = _group_ones(LANES, GDN_DK).astype(BF16)

    def l2(slab):
        return slab * lax.rsqrt(_group_sum(slab * slab, ones_blk) + EPS)

    q = jnp.concatenate([l2(y[:, s * LANES:(s + 1) * LANES]) for s in range(hw // LANES)], axis=1)
    q = q * (GDN_DK ** -0.5)
    k = jnp.concatenate([l2(y[:, hw + s * LANES:hw + (s + 1) * LANES]) for s in range(hw // LANES)], axis=1)
    v = y[:, 2 * hw:3 * hw]

    ab = ab_ref[0]
    g_all = -jnp.exp(gp_ref[0:1, :]) * _softplus(ab + gp_ref[1:2, :])
    beta_all = _sigmoid(ab)
    ri = _iota((tm, tm), 0)
    ci = _iota((tm, tm), 1)
    same = jnp.right_shift(ri, 6) == jnp.right_shift(ci, 6)
    g_parts = jnp.concatenate(_split_bf16(g_all, 3), axis=1)

    def cumsum(tri):
        r = jnp.dot(jnp.logical_and(same, tri).astype(BF16), g_parts, preferred_element_type=F32)
        return r[:, 0:LANES] + r[:, LANES:2 * LANES] + r[:, 2 * LANES:3 * LANES]

    gam_all = jnp.where(_iota((tm, LANES), 1) < GDN_HEADS, cumsum(ci <= ri), cumsum(ci >= ri))
    col_head = jnp.right_shift(_iota((LANES, 2 * hw), 1), GDN_DK.bit_length() - 1)
    src = _iota((LANES, 2 * hw), 0)

    def expand(x, parts, sel):
        r = jnp.dot(jnp.concatenate(_split_bf16(x, parts), axis=0), sel.astype(BF16), preferred_element_type=F32)
        return functools.reduce(lambda a, b: a + b, [r[p * tm:(p + 1) * tm] for p in range(parts)])

    gam_exp = expand(gam_all, 3, src == col_head)
    b_exp = expand(beta_all, 2, src == col_head + 2 * GDN_HEADS)
    gam_f, gam_b = gam_exp[:, 0:hw], gam_exp[:, hw:2 * hw]

    c = GDN_CHUNK
    r64 = _iota((c, PACK_W), 0)
    c64 = jnp.bitwise_and(_iota((c, PACK_W), 1), c - 1)
    eye = (r64 == c64).astype(F32)
    bd_mask = jnp.right_shift(_iota((PACK_W, PACK_W), 0), 6) == jnp.right_shift(_iota((PACK_W, PACK_W), 1), 6)

    for ch in range(tm // c):
        rows = slice(ch * c, (ch + 1) * c)
        for gi in range(hw // PACK_W):
            lanes = slice(gi * PACK_W, (gi + 1) * PACK_W)
            kc, qc, vc = k[rows, lanes], q[rows, lanes], v[rows, lanes]
            kbd = _block_diag(kc, bd_mask)
            kq = _mm_nt(jnp.concatenate([kc, qc], axis=0), kbd)
            kk, qkm = kq[0:c], kq[c:2 * c]
            for d in range(2):
                gam = (gam_f if d == 0 else gam_b)[rows, lanes]
                beta = b_exp[rows, d * hw + gi * PACK_W:d * hw + (gi + 1) * PACK_W]
                gam_row = jnp.sum(gam * eye, axis=0, keepdims=True)
                decay = jnp.exp(jnp.minimum(gam - gam_row, 0.0))
                strict = (r64 > c64) if d == 0 else (r64 < c64)
                incl = (r64 >= c64) if d == 0 else (r64 <= c64)
                a_mat = jnp.where(strict, beta * kk * decay, 0.0)
                for pp in range(PACK_W // LANES):
                    a_ref[d, ch * (hw // LANES) + gi * (PACK_W // LANES) + pp] = \
                        a_mat[:, pp * LANES:(pp + 1) * LANES]
                e_gam = jnp.exp(gam)
                g_last = gam[c - 1:c, :] if d == 0 else gam[0:1, :]
                out = slice(d * hw + gi * PACK_W, d * hw + (gi + 1) * PACK_W)
                rv_ref[0, rows, out] = (vc * beta).astype(rv_ref.dtype)
                rk_ref[0, rows, out] = (kc * beta * e_gam).astype(rk_ref.dtype)
                qk_ref[0, rows, out] = jnp.where(incl, qkm * decay, 0.0).astype(qk_ref.dtype)
                qe_ref[0, rows, out] = (qc * e_gam).astype(qe_ref.dtype)
                ke_ref[0, rows, out] = (kc * jnp.exp(g_last - gam)).astype(ke_ref.dtype)
                dl_ref[0, ch, :, out] = jnp.exp(g_last)


def _gdn_prep(qkv, ab, cw, gp, seg_starts, seg_ends):
    b, t, w = qkv.shape
    tm = TOKEN_TILE
    nt = t // tm
    hb = tm // SUBLANES
    nh = t // SUBLANES
    wide = 2 * GDN_HEADS * GDN_DK
    c = GDN_CHUNK
    cpt = tm // c
    spt = cpt * (GDN_HEADS * GDN_DK // LANES)
    tok = lambda width: pl.BlockSpec((1, tm, width), lambda i, j: (i, j, 0))
    out_shape = [jax.ShapeDtypeStruct((2, b * nt * spt, c, LANES), F32)] + \
                [jax.ShapeDtypeStruct((b, t, wide), BF16)] * 5 + \
                [jax.ShapeDtypeStruct((b, t // c, 1, wide), F32)]
    return pl.pallas_call(
        functools.partial(_gdn_prep_kernel, seg_starts=seg_starts, seg_ends=seg_ends),
        grid=(b, nt),
        in_specs=[tok(w),
                  pl.BlockSpec((1, SUBLANES, w), lambda i, j: (i, jnp.maximum(j * hb - 1, 0), 0)),
                  pl.BlockSpec((1, SUBLANES, w), lambda i, j: (i, jnp.minimum((j + 1) * hb, nh - 1), 0)),
                  tok(LANES),
                  pl.BlockSpec(cw.shape, lambda i, j: (0, 0)),
                  pl.BlockSpec(gp.shape, lambda i, j: (0, 0))],
        out_specs=[pl.BlockSpec((2, spt, c, LANES), lambda i, j: (0, i * nt + j, 0, 0))] +
                  [tok(wide)] * 5 + [pl.BlockSpec((1, cpt, 1, wide), lambda i, j: (i, j, 0, 0))],
        out_shape=out_shape,
        compiler_params=_cparams(("parallel", "parallel")),
        name="gdn_prep",
    )(qkv, qkv, qkv, ab, cw, gp)


def _gdn_solve_kernel(a_ref, o_ref, at_ref, x_ref, ot_ref):
    c = GDN_CHUNK
    sub = SUBLANES
    nb = c // sub
    a_rows = pltpu.einshape("sil->isl", a_ref[0])
    for i in range(c):
        at_ref[i * LANES:(i + 1) * LANES, :] = a_rows[i].T
    x_ref[...] = jnp.zeros_like(x_ref)
    sub_iota = _iota((sub, LANES), 0)

    def solve(upper):
        for step in range(nb):
            ib = (nb - 1 - step) if upper else step
            kbs = range(ib, nb) if upper else range(ib + 1)

            def row(r, carry, ib=ib, kbs=kbs):
                i = ib * sub + ((sub - 1 - r) if upper else r)
                base = pl.multiple_of(i * LANES, LANES)
                accs = {(h2, jb): (sub_iota + jb * sub == i).astype(F32)
                        for h2 in range(LANES // c) for jb in range(nb)}
                for kb in kbs:
                    for h2 in range(LANES // c):
                        coefs = [at_ref[pl.ds(base + h2 * c + kb * sub + kk, 1), :] for kk in range(sub)]
                        for jb in (range(kb, nb) if upper else range(kb + 1)):
                            acc = accs[(h2, jb)]
                            for kk in range(sub):
                                lo = (kb * sub + kk) * LANES + h2 * c + jb * sub
                                acc = acc - coefs[kk] * x_ref[lo:lo + sub, :]
                            accs[(h2, jb)] = acc
                for h2 in range(LANES // c):
                    for jb in range(nb):
                        x_ref[pl.ds(base + h2 * c + jb * sub, sub), :] = accs[(h2, jb)]
                return carry

            lax.fori_loop(0, sub, row, 0)

    @pl.when(pl.program_id(0) == 0)
    def _():
        solve(False)

    @pl.when(pl.program_id(0) == 1)
    def _():
        solve(True)

    for i in range(c):
        ot_ref[i] = x_ref[i * LANES:(i + 1) * LANES, :].T
    o_ref[0] = pltpu.einshape("isl->sil", ot_ref[...])


def _gdn_solve(a):
    _, n_slab, c, _ = a.shape
    assert n_slab % LANES == 0
    blk = pl.BlockSpec((1, LANES, c, LANES), lambda d, g: (d, g, 0, 0))
    return pl.pallas_call(
        _gdn_solve_kernel,
        grid=(2, n_slab // LANES),
        in_specs=[blk],
        out_specs=blk,
        out_shape=jax.ShapeDtypeStruct(a.shape, F32),
        scratch_shapes=[pltpu.VMEM((c * LANES, LANES), F32), pltpu.VMEM((c * LANES, LANES), F32),
                        pltpu.VMEM((c, LANES, LANES), F32)],
        compiler_params=_cparams(("parallel", "parallel")),
        name="gdn_solve",
    )(a)


def _gdn_scan_kernel(tf, rvf, rkf, qkf, qef, kef, dlf, tb, rvb, rkb, qkb, qeb, keb, dlb, of_ref, ob_ref, s_ref):
    c = GDN_CHUNK
    cpt = rvf.shape[1] // c
    ng = rvf.shape[2] // PACK_W
    ppg = PACK_W // LANES
    bd_mask = jnp.right_shift(_iota((PACK_W, PACK_W), 0), 6) == jnp.right_shift(_iota((PACK_W, PACK_W), 1), 6)

    @pl.when(pl.program_id(1) == 0)
    def _():
        s_ref[...] = jnp.zeros_like(s_ref)

    chains = ((tf, rvf, rkf, qkf, qef, kef, dlf, of_ref), (tb, rvb, rkb, qkb, qeb, keb, dlb, ob_ref))

    def jobs(j):
        out = []
        for d in range(2):
            ch = j if d == 0 else cpt - 1 - j
            for gi in range(ng):
                out.append((d, gi, ch, slice(ch * c, (ch + 1) * c), slice(gi * PACK_W, (gi + 1) * PACK_W)))
        return out

    factors = []
    for j in range(cpt):
        where = jobs(j)
        tinvs = [jnp.concatenate([chains[d][0][0, ch * ng * ppg + gi * ppg + pp] for pp in range(ppg)],
                                 axis=1).astype(BF16) for d, gi, ch, rows, lanes in where]
        us = [jnp.dot(t16, _block_diag(chains[d][1][0, rows, lanes], bd_mask), preferred_element_type=F32)
              for (d, gi, ch, rows, lanes), t16 in zip(where, tinvs)]
        ws = [jnp.dot(t16, _block_diag(chains[d][2][0, rows, lanes], bd_mask),
                      preferred_element_type=F32).astype(BF16)
              for (d, gi, ch, rows, lanes), t16 in zip(where, tinvs)]
        factors.append((us, ws))

    for j in range(cpt):
        where = jobs(j)
        us, ws = factors[j]
        states = [s_ref[d * ng + gi] for d, gi, _, _, _ in where]
        wqs = [_mm(jnp.concatenate([w, chains[d][4][0, rows, lanes]], axis=0), st)
               for (d, gi, ch, rows, lanes), w, st in zip(where, ws, states)]
        v_news = [u - wq[0:c] for u, wq in zip(us, wqs)]
        kvs = [_mm_tn(chains[d][5][0, rows, lanes], v_new) for (d, gi, ch, rows, lanes), v_new in zip(where, v_news)]
        for (d, gi, ch, rows, lanes), st, kv in zip(where, states, kvs):
            s_ref[d * ng + gi] = st * chains[d][6][0, ch, :, lanes] + jnp.where(bd_mask, kv, 0.0)
        for (d, gi, ch, rows, lanes), wq, v_new in zip(where, wqs, v_news):
            chains[d][7][0, rows, lanes] = wq[c:2 * c] + _mm(chains[d][3][0, rows, lanes],
                                                             _block_diag(v_new, bd_mask))


def _gdn_scan(tinv, rv, rk, qk, qe, ke, dl, n_ctx):
    b, t, wide = rv.shape
    tm = TOKEN_TILE
    nt = t // tm
    nctt = n_ctx // tm
    hw = wide // 2
    c = GDN_CHUNK
    cpt = tm // c
    spt = cpt * (hw // LANES)

    def mirror(g):
        return jnp.where(g < nctt, nctt - 1 - g, nt - 1 - (g - nctt))

    fwd = pl.BlockSpec((1, tm, hw), lambda i, g: (i, g, 0))
    bwd = pl.BlockSpec((1, tm, hw), lambda i, g: (i, mirror(g), 1))
    tfwd = pl.BlockSpec((1, spt, c, LANES), lambda i, g: (0, i * nt + g, 0, 0))
    tbwd = pl.BlockSpec((1, spt, c, LANES), lambda i, g: (1, i * nt + mirror(g), 0, 0))
    dfwd = pl.BlockSpec((1, cpt, 1, hw), lambda i, g: (i, g, 0, 0))
    dbwd = pl.BlockSpec((1, cpt, 1, hw), lambda i, g: (i, mirror(g), 0, 1))
    out_sds = jax.ShapeDtypeStruct((b, t, hw), F32)
    return pl.pallas_call(
        _gdn_scan_kernel,
        grid=(b, nt),
        in_specs=[tfwd] + [fwd] * 5 + [dfwd] + [tbwd] + [bwd] * 5 + [dbwd],
        out_specs=[pl.BlockSpec((1, tm, hw), lambda i, g: (i, g, 0)),
                   pl.BlockSpec((1, tm, hw), lambda i, g: (i, mirror(g), 0))],
        out_shape=[out_sds, out_sds],
        scratch_shapes=[pltpu.VMEM((2 * hw // PACK_W, PACK_W, PACK_W), F32)],
        compiler_params=_cparams(("parallel", "arbitrary")),
        name="gdn_scan",
    )(tinv, rv, rk, qk, qe, ke, dl, tinv, rv, rk, qk, qe, ke, dl)


def _merge_kernel(x_ref, mod_ref, ya_ref, yb_ref, of_ref, ob_ref, z_ref, gate_ref, on_ref,
                  wpa_ref, wpb_ref, wpc_ref, wout_ref, o_ref):
    x = x_ref[0]
    d = x.shape[-1]
    m = mod_ref[0]
    oc = of_ref[0] + ob_ref[0]
    ones_blk = _group_ones(LANES, GDN_DV).astype(BF16)
    ms = jnp.concatenate(
        [_group_sum(oc[:, s * LANES:(s + 1) * LANES] ** 2, ones_blk) for s in range(oc.shape[-1] // LANES)],
        axis=1) * (1.0 / GDN_DV)
    yc = oc * lax.rsqrt(ms + EPS) * on_ref[...] * _silu(z_ref[0].astype(F32))
    gate = gate_ref[0].astype(F32)
    mix = (_sigmoid(gate[:, 0:d]) * _mm(ya_ref[0], wpa_ref[...])
           + _sigmoid(gate[:, d:2 * d]) * _mm(yb_ref[0], wpb_ref[...])
           + _sigmoid(gate[:, 2 * d:3 * d]) * _mm(yc, wpc_ref[...]))
    o_ref[0] = x + m[:, 2 * d:3 * d] * _mm(mix, wout_ref[...])


def _merge(xs, mod_l, ya, yb, oc_f, oc_b, z, gate, onorm, wpa, wpb, wpc, wout, nctt, ctx_row, t_off):
    b, t, d = xs.shape
    tm = TOKEN_TILE
    nt = t // tm - t_off
    tok = lambda width: pl.BlockSpec((1, tm, width), lambda i, j: (i, j + t_off, 0))
    full = lambda a: pl.BlockSpec(a.shape, lambda i, j: (0,) * a.ndim)
    mod_spec = pl.BlockSpec((1, 1, mod_l.shape[-1]),
                            lambda i, j: (jnp.where(j + t_off < nctt, ctx_row, i), 0, 0))
    return pl.pallas_call(
        _merge_kernel,
        grid=(b, nt),
        in_specs=[tok(d), mod_spec, tok(512), tok(512), tok(512), tok(512), tok(512), tok(3 * d),
                  full(onorm), full(wpa), full(wpb), full(wpc), full(wout)],
        out_specs=pl.BlockSpec((1, tm, d), lambda i, j: (i, j, 0)),
        out_shape=jax.ShapeDtypeStruct((b, nt * tm, d), F32),
        compiler_params=_cparams(("parallel", "parallel")),
        name="merge",
    )(xs, mod_l, ya, yb, oc_f, oc_b, z, gate, onorm, wpa, wpb, wpc, wout)


def _ffn_kernel(x_ref, prev_ref, next_ref, mod_ref, g2_ref, wup_ref, cw_ref, wdn_ref, nf_ref, o_ref,
                *, seg_starts, seg_ends, final):
    ti = pl.program_id(1)
    x = x_ref[0]
    tm, d = x.shape
    m = mod_ref[0]
    xall = jnp.concatenate([prev_ref[0], x, next_ref[0]], axis=0)
    xa = _rms(xall, g2_ref[...]) * (1.0 + m[:, 4 * d:5 * d]) + m[:, 3 * d:4 * d]
    n_all = tm + 2 * SUBLANES
    row = _iota((n_all, 1), 0)
    is_start = functools.reduce(jnp.logical_or, [ti == s for s in seg_starts])
    is_end = functools.reduce(jnp.logical_or, [ti == s for s in seg_ends])
    drop = jnp.logical_or(jnp.logical_and(row == SUBLANES - 1, is_start),
                          jnp.logical_and(row == SUBLANES + tm, is_end))
    xa = jnp.where(drop, 0.0, xa).astype(BF16)

    def conv(h, cols):
        hp = pltpu.roll(h, 1, 0)[SUBLANES:SUBLANES + tm]
        hn = pltpu.roll(h, n_all - 1, 0)[SUBLANES:SUBLANES + tm]
        return (hp * cw_ref[0:1, cols] + h[SUBLANES:SUBLANES + tm] * cw_ref[1:2, cols]
                + hn * cw_ref[2:3, cols])

    cwid = FFN_DIM // FFN_SPLIT
    acc = jnp.zeros((tm, d), F32)
    for j in range(FFN_SPLIT):
        ca = slice(j * cwid, (j + 1) * cwid)
        cb = slice(FFN_DIM + j * cwid, FFN_DIM + (j + 1) * cwid)
        ha = conv(jnp.dot(xa, wup_ref[:, ca], preferred_element_type=F32), ca)
        hb = conv(jnp.dot(xa, wup_ref[:, cb], preferred_element_type=F32), cb)
        acc = acc + _mm(_silu(ha) * hb, wdn_ref[ca, :])
    y = x + m[:, 5 * d:6 * d] * acc
    if final:
        y = _rms(y, nf_ref[...])
    o_ref[0] = y


def _ffn(x1, mod_l, g2, wup, cw, wdn, nf, nctt_mod, ctx_row, seg_starts, seg_ends, final):
    b, t, d = x1.shape
    tm = TOKEN_TILE
    nt = t // tm
    hb = tm // SUBLANES
    nh = t // SUBLANES
    full = lambda a: pl.BlockSpec(a.shape, lambda i, j: (0,) * a.ndim)
    mod_spec = pl.BlockSpec((1, 1, mod_l.shape[-1]),
                            lambda i, j: (jnp.where(j < nctt_mod, ctx_row, i), 0, 0))
    return pl.pallas_call(
        functools.partial(_ffn_kernel, seg_starts=seg_starts, seg_ends=seg_ends, final=final),
        grid=(b, nt),
        in_specs=[pl.BlockSpec((1, tm, d), lambda i, j: (i, j, 0)),
                  pl.BlockSpec((1, SUBLANES, d), lambda i, j: (i, jnp.maximum(j * hb - 1, 0), 0)),
                  pl.BlockSpec((1, SUBLANES, d), lambda i, j: (i, jnp.minimum((j + 1) * hb, nh - 1), 0)),
                  mod_spec, full(g2), full(wup), full(cw), full(wdn), full(nf)],
        out_specs=pl.BlockSpec((1, tm, d), lambda i, j: (i, j, 0)),
        out_shape=jax.ShapeDtypeStruct((b, t, d), F32),
        compiler_params=_cparams(("parallel", "parallel")),
        name="ffn",
    )(x1, x1, x1, mod_l, g2, wup, cw, wdn, nf)


def _cols(w, segments):
    parts = [jnp.zeros(w.shape[:-1] + (seg,), w.dtype) if isinstance(seg, int) else w[..., seg[0]:seg[1]]
             for seg in segments]
    return jnp.concatenate(parts, axis=-1)


def _win_segments(d):
    widths = [MLA_Q_RANK, MLA_KV_RANK, MLA_ROPE, SWA_HEADS * SWA_HEAD_DIM, SWA_KV_HEADS * SWA_HEAD_DIM,
              SWA_KV_HEADS * SWA_HEAD_DIM, GDN_HEADS * (2 * GDN_DK + GDN_DV), GDN_HEADS * GDN_DV,
              2 * GDN_HEADS, 2 * GDN_HEADS, 3 * d]
    off = np.cumsum([0] + widths)
    cq, ckv, kr, sq, sk, sv, qkv, z, a, b, gate = [(int(off[i]), int(off[i + 1])) for i in range(11)]
    sq_perm = [(sq[0] + h * SWA_HEAD_DIM, sq[0] + (h + 1) * SWA_HEAD_DIM) for h in _swa_head_order()]
    segs = [cq, ckv, MLA_NOPE, kr, LANES - MLA_NOPE - MLA_ROPE] + sq_perm + \
           [sk, sv, qkv, z, a, b, LANES - 4 * GDN_HEADS, gate]
    return segs


def _swa_head_order():
    group = SWA_HEADS // SWA_KV_HEADS
    return [kv * group + g for g in range(group) for kv in range(SWA_KV_HEADS)]


def _rope_tables(n_ctx, n_lat):
    t = np.arange(n_lat)
    pos = np.stack([t // GRID_W, t % GRID_W]).astype(np.float32)

    def build(head_w, rope_lo, rope_w):
        da = rope_w // 2
        half = da // 2
        inv = ROPE_THETA ** (-np.arange(half, dtype=np.float32) / half)
        c = np.ones((n_lat, LANES), np.float32)
        sa = np.zeros((n_lat, LANES), np.float32)
        sb = np.zeros((n_lat, LANES), np.float32)
        for lane in range(LANES):
            dd = lane % head_w - rope_lo
            if dd < 0 or dd >= rope_w:
                continue
            axis, r = dd // da, dd % da
            ang = pos[axis] * inv[r % half]
            c[:, lane] = np.cos(ang)
            if r < half:
                sa[:, lane] = -np.sin(ang)
            else:
                sb[:, lane] = np.sin(ang)
        ctx = [np.ones((n_ctx, LANES), np.float32), np.zeros((n_ctx, LANES), np.float32),
               np.zeros((n_ctx, LANES), np.float32)]
        return [jnp.asarray(np.concatenate([cx, lt])) for cx, lt in zip(ctx, (c, sa, sb))]

    return build(LANES, MLA_NOPE, MLA_ROPE) + build(SWA_HEAD_DIM, 0, SWA_HEAD_DIM)


def kernel(x, c, ctx, c_ctx, w_mod, b_mod, norm1, norm2, w_in, mla_q_norm, mla_kv_norm, w_uq, w_ukv,
           swa_sink, gdn_conv, gdn_a_log, gdn_dt_bias, gdn_norm, w_branch_a, w_branch_b, w_branch_c,
           w_out, ffn_up, ffn_conv, ffn_down, norm_f):
    b, s, d = x.shape
    n_ctx = ctx.shape[1]
    depth = w_mod.shape[0]
    tm = TOKEN_TILE
    assert n_ctx % tm == 0 and s % tm == 0 and s >= SWA_TILE + 2 * WINDOW and WINDOW % LANES == 0
    nctt = n_ctx // tm
    ntt = (n_ctx + s) // tm

    rows = -(-(b + 1) // SUBLANES) * SUBLANES
    cc = jnp.zeros((rows, d), F32).at[0:b].set(c).at[b].set(c_ctx)
    mod = _modulation(cc, w_mod, b_mod).reshape(depth, rows, 1, 6 * d)

    hd_q = MLA_NOPE + MLA_ROPE
    hd_kv = MLA_NOPE + MLA_V
    uq_segs = [s for h in range(MLA_HEADS) for s in ((h * hd_q, (h + 1) * hd_q), LANES - hd_q)]
    uk_segs = [s for h in range(MLA_HEADS) for s in ((h * hd_kv, h * hd_kv + MLA_NOPE), LANES - MLA_NOPE)]
    uv_segs = [(h * hd_kv + MLA_NOPE, (h + 1) * hd_kv) for h in range(MLA_HEADS)]
    win_segs = _win_segments(d)

    def layer_weights(l):
        wpb = jnp.concatenate([w_branch_b[l, h * SWA_HEAD_DIM:(h + 1) * SWA_HEAD_DIM] for h in _swa_head_order()],
                              axis=0)
        gpar = jnp.zeros((SUBLANES, LANES), F32)
        gpar = gpar.at[0, 0:2 * GDN_HEADS].set(gdn_a_log[l].reshape(-1))
        gpar = gpar.at[1, 0:2 * GDN_HEADS].set(gdn_dt_bias[l].reshape(-1))
        pad_rows = lambda a: jnp.pad(a, ((0, SUBLANES - a.shape[0]), (0, 0)))
        return dict(
            win=_cols(w_in[l], win_segs).astype(BF16),
            wuq=_cols(w_uq[l], uq_segs).astype(BF16),
            wukvk=_cols(w_ukv[l], uk_segs).astype(BF16),
            wukvv=_cols(w_ukv[l], uv_segs).astype(BF16),
            wpa=w_branch_a[l].astype(BF16), wpb=wpb.astype(BF16), wpc=w_branch_c[l].astype(BF16),
            wout=w_out[l].astype(BF16), wup=ffn_up[l].astype(BF16), wdn=ffn_down[l].astype(BF16),
            sink=jnp.broadcast_to(swa_sink[l][:, None], (SWA_HEADS, LANES)),
            onorm=jnp.tile(gdn_norm[l], GDN_HEADS).reshape(1, -1),
            gconv=pad_rows(gdn_conv[l]), fconv=pad_rows(ffn_conv[l]), gpar=gpar)

    rope_tabs = _rope_tables(n_ctx, s)

    xs = jnp.concatenate([ctx, x], axis=1)
    row2 = lambda a: a.reshape(1, -1)
    for l in range(depth):
        last = l == depth - 1
        t_off = nctt if last else 0
        p = layer_weights(l)
        qm, km, vm, sq, sk, sv, qkv, z, ab, gate = _proj(
            xs, mod[l], row2(norm1[l]), p["win"], row2(mla_q_norm[l]), row2(mla_kv_norm[l]),
            p["wuq"], p["wukvk"], p["wukvv"], rope_tabs, nctt, b)
        ya = _mla(qm, km, vm, n_ctx, last)
        yb = _swa(p["sink"], sq, sk, sv, n_ctx, last)
        a_mat, rv, rk, qk, qe, ke, dl = _gdn_prep(qkv, ab, p["gconv"], p["gpar"], (0, nctt), (nctt - 1, ntt - 1))
        oc_f, oc_b = _gdn_scan(_gdn_solve(a_mat), rv, rk, qk, qe, ke, dl, n_ctx)
        x1 = _merge(xs, mod[l], ya, yb, oc_f, oc_b, z, gate, p["onorm"], p["wpa"], p["wpb"], p["wpc"], p["wout"],
                    nctt, b, t_off)
        if last:
            xs = _ffn(x1, mod[l], row2(norm2[l]), p["wup"], p["fconv"], p["wdn"], row2(norm_f),
                      0, b, (0,), (s // tm - 1,), True)
        else:
            xs = _ffn(x1, mod[l], row2(norm2[l]), p["wup"], p["fconv"], p["wdn"], row2(norm_f),
                      nctt, b, (0, nctt), (nctt - 1, ntt - 1), False)
    return xs
```

```python
import functools

import numpy as np
import jax
import jax.numpy as jnp
from jax import lax
from jax.experimental import pallas as pl
from jax.experimental.pallas import tpu as pltpu

F32 = jnp.float32
BF16 = jnp.bfloat16

GRID_W = 64
ROPE_THETA = 10000.0
NEG_INF = -1e30
EPS = 1e-6
LOG2E = 1.4426950408889634

MLA_HEADS = 8
MLA_Q_RANK = 384
MLA_KV_RANK = 256
MLA_NOPE = 64
MLA_ROPE = 32
MLA_V = 64

SWA_HEADS = 8
SWA_KV_HEADS = 2
SWA_HEAD_DIM = 64
WINDOW = 128

GDN_HEADS = 8
GDN_DK = 64
GDN_DV = 64
GDN_CHUNK = 64

FFN_DIM = 2816

LANES = 128
SUBLANES = 8
VMEM_LIMIT_BYTES = 56 * 1024 * 1024

TOKEN_TILE = 256
SWA_TILE = 128
MLA_Q_TILE = 1024
PACK = 4
PACK_W = PACK * GDN_DK
FFN_SPLIT = 1


def _sigmoid(x):
    return 0.5 * jnp.tanh(0.5 * x) + 0.5


def _silu(x):
    return x * _sigmoid(x)


def _softplus(x):
    return jnp.maximum(x, 0.0) + jnp.log(1.0 + jnp.exp(-jnp.abs(x)))


def _rms(x, g):
    return x * lax.rsqrt(jnp.mean(x * x, axis=-1, keepdims=True) + EPS) * g


def _mm(a, b):
    return jnp.dot(a.astype(BF16), b.astype(BF16), preferred_element_type=F32)


def _mm_nt(a, b):
    return lax.dot_general(a.astype(BF16), b.astype(BF16), (((1,), (1,)), ((), ())),
                           preferred_element_type=F32)


def _mm_tn(a, b):
    return lax.dot_general(a.astype(BF16), b.astype(BF16), (((0,), (0,)), ((), ())),
                           preferred_element_type=F32)


def _split_bf16(x, parts):
    out = []
    for _ in range(parts - 1):
        hi = x.astype(BF16)
        out.append(hi)
        x = x - hi.astype(F32)
    out.append(x.astype(BF16))
    return out


def _group_sum(x, ones_blk):
    n = x.shape[0]
    r = jnp.dot(jnp.concatenate(_split_bf16(x, 2), axis=0), ones_blk, preferred_element_type=F32)
    return r[0:n] + r[n:2 * n]


def _iota(shape, dim):
    return lax.broadcasted_iota(jnp.int32, shape, dim)


def _group_ones(n, group):
    sh = group.bit_length() - 1
    return (jnp.right_shift(_iota((n, n), 0), sh) == jnp.right_shift(_iota((n, n), 1), sh)).astype(F32)


def _rope(x, c, sa, sb, half):
    n = x.shape[-1]
    return x * c + pltpu.roll(x, n - half, 1) * sa + pltpu.roll(x, half, 1) * sb


def _cparams(sem):
    return pltpu.CompilerParams(dimension_semantics=sem, vmem_limit_bytes=VMEM_LIMIT_BYTES)


def _mod_kernel(c_ref, w_ref, b_ref, o_ref):
    s = _silu(c_ref[...])
    o_ref[0] = _mm(s, w_ref[0]) + b_ref[0]


def _modulation(cc, w_mod, b_mod):
    depth, d, n = w_mod.shape
    r = cc.shape[0]
    tn = 2048
    return pl.pallas_call(
        _mod_kernel,
        grid=(depth, n // tn),
        in_specs=[pl.BlockSpec((r, d), lambda l, j: (0, 0)),
                  pl.BlockSpec((1, d, tn), lambda l, j: (l, 0, j)),
                  pl.BlockSpec((1, 1, tn), lambda l, j: (l, 0, j))],
        out_specs=pl.BlockSpec((1, r, tn), lambda l, j: (l, 0, j)),
        out_shape=jax.ShapeDtypeStruct((depth, r, n), F32),
        compiler_params=_cparams(("parallel", "parallel")),
        name="modulation",
    )(cc, w_mod, b_mod.reshape(depth, 1, n))


_P_CQ = (0, 384)
_P_CKV = (384, 640)
_P_KR = (640, 768)
_P_SQ = (768, 1280)
_P_SK = (1280, 1408)
_P_SV = (1408, 1536)
_P_QKV = (1536, 3072)
_P_Z = (3072, 3584)
_P_AB = (3584, 3712)
_P_GATE = (3712, 6784)
_P_WIDTH = 6784


def _proj_kernel(x_ref, mod_ref, g1_ref, win_ref, qn_ref, kvn_ref, wuq_ref, wukvk_ref, wukvv_ref,
                 mc_ref, ma_ref, mb_ref, sc_ref, sa_ref, sb_ref,
                 qm_ref, km_ref, vm_ref, sq_ref, sk_ref, sv_ref, qkv_ref, z_ref, ab_ref, gate_ref):
    x = x_ref[0]
    d = x.shape[-1]
    m = mod_ref[0]
    xb = (_rms(x, g1_ref[...]) * (1.0 + m[:, d:2 * d]) + m[:, 0:d]).astype(BF16)

    def proj(piece):
        return jnp.dot(xb, win_ref[:, piece[0]:piece[1]], preferred_element_type=F32)

    mc, ma, mb = mc_ref[...], ma_ref[...], mb_ref[...]
    sc, sa, sb = sc_ref[...], sa_ref[...], sb_ref[...]

    cq = _rms(proj(_P_CQ), qn_ref[...])
    ckv = _rms(proj(_P_CKV), kvn_ref[...])

    gate_ref[0] = proj(_P_GATE).astype(gate_ref.dtype)
    qkv_ref[0] = proj(_P_QKV)
    z_ref[0] = proj(_P_Z).astype(z_ref.dtype)
    ab_ref[0] = proj(_P_AB)
    sv_ref[0] = proj(_P_SV).astype(sv_ref.dtype)

    q = _mm(cq, wuq_ref[...]) * ((MLA_NOPE + MLA_ROPE) ** -0.5 * LOG2E)
    kn = _mm(ckv, wukvk_ref[...])
    vm_ref[0] = _mm(ckv, wukvv_ref[...]).astype(vm_ref.dtype)
    kr = _rope(proj(_P_KR), mc, ma, mb, MLA_ROPE // 4)
    for h in range(MLA_HEADS):
        sl = slice(h * LANES, (h + 1) * LANES)
        qm_ref[0, :, sl] = _rope(q[:, sl], mc, ma, mb, MLA_ROPE // 4).astype(qm_ref.dtype)
        km_ref[0, :, sl] = (kn[:, sl] + kr).astype(km_ref.dtype)

    sq = proj(_P_SQ) * (SWA_HEAD_DIM ** -0.5 * LOG2E)
    for g in range(SWA_HEADS * SWA_HEAD_DIM // LANES):
        sl = slice(g * LANES, (g + 1) * LANES)
        sq_ref[0, :, sl] = _rope(sq[:, sl], sc, sa, sb, SWA_HEAD_DIM // 4).astype(sq_ref.dtype)
    sk_ref[0] = _rope(proj(_P_SK), sc, sa, sb, SWA_HEAD_DIM // 4).astype(sk_ref.dtype)


def _proj(xs, mod_l, g1, win, qn, kvn, wuq, wukvk, wukvv, rope_tabs, nctt, ctx_row):
    b, t, d = xs.shape
    tm = TOKEN_TILE
    nt = t // tm

    def tok(w):
        return pl.BlockSpec((1, tm, w), lambda i, j: (i, j, 0))

    def full(a):
        return pl.BlockSpec(a.shape, lambda i, j: (0,) * a.ndim, pipeline_mode=pl.Buffered(1))

    tab = pl.BlockSpec((tm, LANES), lambda i, j: (j, 0))
    mod_spec = pl.BlockSpec((1, 1, mod_l.shape[-1]),
                            lambda i, j: (jnp.where(j < nctt, ctx_row, i), 0, 0))
    widths = (1024, 1024, 512, 512, 128, 128, 1536, 512, 128, 3072)
    dtypes = (BF16, BF16, BF16, BF16, BF16, BF16, F32, BF16, F32, BF16)
    return pl.pallas_call(
        _proj_kernel,
        grid=(b, nt),
        in_specs=[tok(d), mod_spec, full(g1), full(win), full(qn), full(kvn), full(wuq),
                  full(wukvk), full(wukvv)] + [tab] * 6,
        out_specs=[tok(w) for w in widths],
        out_shape=[jax.ShapeDtypeStruct((b, t, w), dt) for w, dt in zip(widths, dtypes)],
        compiler_params=_cparams(("parallel", "parallel")),
        name="proj",
    )(xs, mod_l, g1, win, qn, kvn, wuq, wukvk, wukvv, *rope_tabs)


def _mla_kernel(q_ref, k_ref, v_ref, o_ref, *, n_ctx, q_tile, skip_ctx):
    n = pl.program_id(2)

    def run(row0, nq, nk):
        low = _iota((nq, LANES), 1) < MLA_V
        v = v_ref[0, 0:nk, :]
        v_low = _iota((nk, LANES), 1) < MLA_V
        one = jnp.ones_like(v)
        ss = [_mm_nt(q_ref[0, pl.ds(row0, nq), hh * LANES:(hh + 1) * LANES],
                     k_ref[0, 0:nk, hh * LANES:(hh + 1) * LANES]) for hh in range(2)]
        ps = [jnp.exp2(s - jnp.max(s, axis=-1, keepdims=True)) for s in ss]
        outs = []
        for hh in range(2):
            acc = _mm(ps[hh], jnp.where(v_low if hh == 0 else jnp.logical_not(v_low), v, one))
            den = acc[:, MLA_V:MLA_V + 1] if hh == 0 else acc[:, 0:1]
            outs.append(acc / den)
        o_ref[0, pl.ds(row0, nq), :] = jnp.where(low, outs[0], outs[1]).astype(o_ref.dtype)

    @pl.when(n == 0)
    def _():
        if skip_ctx:
            o_ref[0, 0:n_ctx, :] = jnp.zeros((n_ctx, LANES), o_ref.dtype)
        else:
            run(0, n_ctx, n_ctx)

    @pl.when(n > 0)
    def _():
        run(pl.multiple_of(n_ctx + (n - 1) * q_tile, LANES), q_tile, k_ref.shape[1])


def _mla(qm, km, vm, n_ctx, skip_ctx):
    b, t, _ = qm.shape
    q_tile = MLA_Q_TILE
    assert (t - n_ctx) % q_tile == 0
    steps = (t - n_ctx) // q_tile + 1
    return pl.pallas_call(
        functools.partial(_mla_kernel, n_ctx=n_ctx, q_tile=q_tile, skip_ctx=skip_ctx),
        grid=(b, MLA_HEADS // 2, steps),
        in_specs=[pl.BlockSpec((1, t, 2 * LANES), lambda i, j, n: (i, 0, j)),
                  pl.BlockSpec((1, t, 2 * LANES), lambda i, j, n: (i, 0, j)),
                  pl.BlockSpec((1, t, LANES), lambda i, j, n: (i, 0, j))],
        out_specs=pl.BlockSpec((1, t, LANES), lambda i, j, n: (i, 0, j)),
        out_shape=jax.ShapeDtypeStruct((b, t, MLA_HEADS * MLA_V), BF16),
        compiler_params=_cparams(("parallel", "parallel", "arbitrary")),
        name="mla",
    )(qm, km, vm)


def _swa_kernel(sink_ref, q_ref, k_ref, v_ref, o_ref, *, n_ctx, skip_ctx):
    i = pl.program_id(1)
    blk = q_ref.shape[1]
    t = k_ref.shape[1]
    n_lat = t - n_ctx
    nctt = n_ctx // blk
    low = _iota((blk, LANES), 1) < SWA_HEAD_DIM
    n_slab = SWA_HEADS * SWA_HEAD_DIM // LANES
    half = n_slab * blk

    parts, sinks = [], []
    for hh in range(2):
        for g in range(n_slab):
            qs = q_ref[0, :, g * LANES:(g + 1) * LANES]
            parts.append(jnp.where(low if hh == 0 else jnp.logical_not(low), qs, jnp.zeros_like(qs)))
            head = hh * n_slab + g
            sinks.append(jnp.broadcast_to(sink_ref[head:head + 1, 0:1] * LOG2E, (blk, 1)))
    q_all = jnp.concatenate(parts, axis=0)
    sink = jnp.concatenate(sinks, axis=0)

    def with_ones(v, hh):
        v_low = _iota(v.shape, 1) < SWA_HEAD_DIM
        return jnp.where(v_low if hh == 0 else jnp.logical_not(v_low), v, jnp.ones_like(v))

    def attend(loc):
        kc = k_ref[0, 0:n_ctx, :]
        vc = v_ref[0, 0:n_ctx, :]
        s_c = _mm_nt(q_all, kc)
        mx = jnp.maximum(jnp.max(s_c, axis=-1, keepdims=True), sink)
        if loc is not None:
            kl, vl, valid = loc
            s_l = jnp.where(valid[None], _mm_nt(q_all, kl).reshape(2 * n_slab, blk, valid.shape[-1]), NEG_INF)
            s_l = s_l.reshape(2 * n_slab * blk, valid.shape[-1])
            mx = jnp.maximum(mx, jnp.max(s_l, axis=-1, keepdims=True))
        p_c = jnp.exp2(s_c - mx).astype(BF16)
        p_sink = jnp.exp2(sink - mx)
        if loc is not None:
            p_l = jnp.exp2(s_l - mx).astype(BF16)
        outs = []
        for hh in range(2):
            rows = slice(hh * half, (hh + 1) * half)
            acc = _mm(p_c[rows], with_ones(vc, hh))
            if loc is not None:
                acc = acc + _mm(p_l[rows], with_ones(vl, hh))
            den = (acc[:, SWA_HEAD_DIM:SWA_HEAD_DIM + 1] if hh == 0 else acc[:, 0:1]) + p_sink[rows]
            outs.append(acc / den)
        for g in range(n_slab):
            rows = slice(g * blk, (g + 1) * blk)
            o_ref[0, :, g * LANES:(g + 1) * LANES] = jnp.where(low, outs[0][rows], outs[1][rows]).astype(o_ref.dtype)

    @pl.when(i < nctt)
    def _():
        if skip_ctx:
            o_ref[...] = jnp.zeros_like(o_ref)
        else:
            attend(None)

    @pl.when(i >= nctt)
    def _():
        n = i - nctt
        span = blk + 2 * WINDOW
        start = jnp.clip(n * blk - WINDOW, 0, n_lat - span)
        row0 = pl.multiple_of(n_ctx + start, LANES)
        kl = k_ref[0, pl.ds(row0, span), :]
        vl = v_ref[0, pl.ds(row0, span), :]
        kpos = start + _iota((blk, span), 1)
        qpos = n * blk + _iota((blk, span), 0)
        valid = jnp.abs(kpos - qpos) <= WINDOW
        attend((kl, vl, valid))


def _swa(sink, sq, sk, sv, n_ctx, skip_ctx):
    b, t, w = sq.shape
    blk = SWA_TILE
    nt = t // blk
    return pl.pallas_call(
        functools.partial(_swa_kernel, n_ctx=n_ctx, skip_ctx=skip_ctx),
        grid=(b, nt),
        in_specs=[pl.BlockSpec(sink.shape, lambda i, n: (0, 0)),
                  pl.BlockSpec((1, blk, w), lambda i, n: (i, n, 0)),
                  pl.BlockSpec((1, t, LANES), lambda i, n: (i, 0, 0)),
                  pl.BlockSpec((1, t, LANES), lambda i, n: (i, 0, 0))],
        out_specs=pl.BlockSpec((1, blk, w), lambda i, n: (i, n, 0)),
        out_shape=jax.ShapeDtypeStruct((b, t, w), BF16),
        compiler_params=_cparams(("parallel", "parallel")),
        name="swa",
    )(sink, sq, sk, sv)


def _block_diag(x, bd_mask):
    return jnp.where(bd_mask, jnp.concatenate([x] * PACK, axis=0), 0.0)


def _gdn_prep_kernel(qkv_ref, prev_ref, next_ref, ab_ref, cw_ref, gp_ref,
                     a_ref, rv_ref, rk_ref, qk_ref, qe_ref, ke_ref, dl_ref, *, seg_starts, seg_ends):
    ti = pl.program_id(1)
    tm = qkv_ref.shape[1]
    hw = GDN_HEADS * GDN_DK
    x = qkv_ref[0]
    row = _iota((tm, 1), 0)
    is_start = functools.reduce(jnp.logical_or, [ti == s for s in seg_starts])
    is_end = functools.reduce(jnp.logical_or, [ti == s for s in seg_ends])
    prev_row = jnp.where(is_start, 0.0, prev_ref[0, SUBLANES - 1:SUBLANES, :])
    next_row = jnp.where(is_end, 0.0, next_ref[0, 0:1, :])
    x_prev = jnp.where(row == 0, prev_row, pltpu.roll(x, 1, 0))
    x_next = jnp.where(row == tm - 1, next_row, pltpu.roll(x, tm - 1, 0))
    y = _silu(x_prev * cw_ref[0:1, :] + x * cw_ref[1:2, :] + x_next * cw_ref[2:3, :])

    ones_blk = _group_ones(LANES, GDN_DK).astype(BF16)

    def l2(slab):
        return slab * lax.rsqrt(_group_sum(slab * slab, ones_blk) + EPS)

    q = jnp.concatenate([l2(y[:, s * LANES:(s + 1) * LANES]) for s in range(hw // LANES)], axis=1)
    q = q * (GDN_DK ** -0.5)
    k = jnp.concatenate([l2(y[:, hw + s * LANES:hw + (s + 1) * LANES]) for s in range(hw // LANES)], axis=1)
    v = y[:, 2 * hw:3 * hw]

    ab = ab_ref[0]
    g_all = -jnp.exp(gp_ref[0:1, :]) * _softplus(ab + gp_ref[1:2, :])
    beta_all = _sigmoid(ab)
    ri = _iota((tm, tm), 0)
    ci = _iota((tm, tm), 1)
    same = jnp.right_shift(ri, 6) == jnp.right_shift(ci, 6)
    g_parts = jnp.concatenate(_split_bf16(g_all, 3), axis=1)

    def cumsum(tri):
        r = jnp.dot(jnp.logical_and(same, tri).astype(BF16), g_parts, preferred_element_type=F32)
        return r[:, 0:LANES] + r[:, LANES:2 * LANES] + r[:, 2 * LANES:3 * LANES]

    gam_all = jnp.where(_iota((tm, LANES), 1) < GDN_HEADS, cumsum(ci <= ri), cumsum(ci >= ri))
    col_head = jnp.right_shift(_iota((LANES, 2 * hw), 1), GDN_DK.bit_length() - 1)
    src = _iota((LANES, 2 * hw), 0)

    def expand(x, parts, sel):
        r = jnp.dot(jnp.concatenate(_split_bf16(x, parts), axis=0), sel.astype(BF16), preferred_element_type=F32)
        return functools.reduce(lambda a, b: a + b, [r[p * tm:(p + 1) * tm] for p in range(parts)])

    gam_exp = expand(gam_all, 3, src == col_head)
    b_exp = expand(beta_all, 2, src == col_head + 2 * GDN_HEADS)
    gam_f, gam_b = gam_exp[:, 0:hw], gam_exp[:, hw:2 * hw]

    c = GDN_CHUNK
    r64 = _iota((c, PACK_W), 0)
    c64 = jnp.bitwise_and(_iota((c, PACK_W), 1), c - 1)
    eye = (r64 == c64).astype(F32)
    bd_mask = jnp.right_shift(_iota((PACK_W, PACK_W), 0), 6) == jnp.right_shift(_iota((PACK_W, PACK_W), 1), 6)

    for ch in range(tm // c):
        rows = slice(ch * c, (ch + 1) * c)
        for gi in range(hw // PACK_W):
            lanes = slice(gi * PACK_W, (gi + 1) * PACK_W)
            kc, qc, vc = k[rows, lanes], q[rows, lanes], v[rows, lanes]
            kbd = _block_diag(kc, bd_mask)
            kq = _mm_nt(jnp.concatenate([kc, qc], axis=0), kbd)
            kk, qkm = kq[0:c], kq[c:2 * c]
            for d in range(2):
                gam = (gam_f if d == 0 else gam_b)[rows, lanes]
                beta = b_exp[rows, d * hw + gi * PACK_W:d * hw + (gi + 1) * PACK_W]
                gam_row = jnp.sum(gam * eye, axis=0, keepdims=True)
                decay = jnp.exp(jnp.minimum(gam - gam_row, 0.0))
                strict = (r64 > c64) if d == 0 else (r64 < c64)
                incl = (r64 >= c64) if d == 0 else (r64 <= c64)
                a_mat = jnp.where(strict, beta * kk * decay, 0.0)
                for pp in range(PACK_W // LANES):
                    a_ref[d, ch * (hw // LANES) + gi * (PACK_W // LANES) + pp] = \
                        a_mat[:, pp * LANES:(pp + 1) * LANES]
                e_gam = jnp.exp(gam)
                g_last = gam[c - 1:c, :] if d == 0 else gam[0:1, :]
                out = slice(d * hw + gi * PACK_W, d * hw + (gi + 1) * PACK_W)
                rv_ref[0, rows, out] = (vc * beta).astype(rv_ref.dtype)
                rk_ref[0, rows, out] = (kc * beta * e_gam).astype(rk_ref.dtype)
                qk_ref[0, rows, out] = jnp.where(incl, qkm * decay, 0.0).astype(qk_ref.dtype)
                qe_ref[0, rows, out] = (qc * e_gam).astype(qe_ref.dtype)
                ke_ref[0, rows, out] = (kc * jnp.exp(g_last - gam)).astype(ke_ref.dtype)
                dl_ref[0, ch, :, out] = jnp.exp(g_last)


def _gdn_prep(qkv, ab, cw, gp, seg_starts, seg_ends):
    b, t, w = qkv.shape
    tm = TOKEN_TILE
    nt = t // tm
    hb = tm // SUBLANES
    nh = t // SUBLANES
    wide = 2 * GDN_HEADS * GDN_DK
    c = GDN_CHUNK
    cpt = tm // c
    spt = cpt * (GDN_HEADS * GDN_DK // LANES)
    tok = lambda width: pl.BlockSpec((1, tm, width), lambda i, j: (i, j, 0))
    out_shape = [jax.ShapeDtypeStruct((2, b * nt * spt, c, LANES), F32)] + \
                [jax.ShapeDtypeStruct((b, t, wide), BF16)] * 5 + \
                [jax.ShapeDtypeStruct((b, t // c, 1, wide), F32)]
    return pl.pallas_call(
        functools.partial(_gdn_prep_kernel, seg_starts=seg_starts, seg_ends=seg_ends),
        grid=(b, nt),
        in_specs=[tok(w),
                  pl.BlockSpec((1, SUBLANES, w), lambda i, j: (i, jnp.maximum(j * hb - 1, 0), 0)),
                  pl.BlockSpec((1, SUBLANES, w), lambda i, j: (i, jnp.minimum((j + 1) * hb, nh - 1), 0)),
                  tok(LANES),
                  pl.BlockSpec(cw.shape, lambda i, j: (0, 0)),
                  pl.BlockSpec(gp.shape, lambda i, j: (0, 0))],
        out_specs=[pl.BlockSpec((2, spt, c, LANES), lambda i, j: (0, i * nt + j, 0, 0))] +
                  [tok(wide)] * 5 + [pl.BlockSpec((1, cpt, 1, wide), lambda i, j: (i, j, 0, 0))],
        out_shape=out_shape,
        compiler_params=_cparams(("parallel", "parallel")),
        name="gdn_prep",
    )(qkv, qkv, qkv, ab, cw, gp)


def _gdn_solve_kernel(a_ref, o_ref, at_ref, x_ref, ot_ref):
    c = GDN_CHUNK
    sub = SUBLANES
    nb = c // sub
    a_rows = pltpu.einshape("sil->isl", a_ref[0])
    for i in range(c):
        at_ref[i * LANES:(i + 1) * LANES, :] = a_rows[i].T
    x_ref[...] = jnp.zeros_like(x_ref)
    sub_iota = _iota((sub, LANES), 0)

    def solve(upper):
        for step in range(nb):
            ib = (nb - 1 - step) if upper else step
            kbs = range(ib, nb) if upper else range(ib + 1)

            def row(r, carry, ib=ib, kbs=kbs):
                i = ib * sub + ((sub - 1 - r) if upper else r)
                base = pl.multiple_of(i * LANES, LANES)
                accs = {(h2, jb): (sub_iota + jb * sub == i).astype(F32)
                        for h2 in range(LANES // c) for jb in range(nb)}
                for kb in kbs:
                    for h2 in range(LANES // c):
                        coefs = [at_ref[pl.ds(base + h2 * c + kb * sub + kk, 1), :] for kk in range(sub)]
                        for jb in (range(kb, nb) if upper else range(kb + 1)):
                            acc = accs[(h2, jb)]
                            for kk in range(sub):
                                lo = (kb * sub + kk) * LANES + h2 * c + jb * sub
                                acc = acc - coefs[kk] * x_ref[lo:lo + sub, :]
                            accs[(h2, jb)] = acc
                for h2 in range(LANES // c):
                    for jb in range(nb):
                        x_ref[pl.ds(base + h2 * c + jb * sub, sub), :] = accs[(h2, jb)]
                return carry

            lax.fori_loop(0, sub, row, 0)

    @pl.when(pl.program_id(0) == 0)
    def _():
        solve(False)

    @pl.when(pl.program_id(0) == 1)
    def _():
        solve(True)

    for i in range(c):
        ot_ref[i] = x_ref[i * LANES:(i + 1) * LANES, :].T
    o_ref[0] = pltpu.einshape("isl->sil", ot_ref[...])


def _gdn_solve(a):
    _, n_slab, c, _ = a.shape
    assert n_slab % LANES == 0
    blk = pl.BlockSpec((1, LANES, c, LANES), lambda d, g: (d, g, 0, 0))
    return pl.pallas_call(
        _gdn_solve_kernel,
        grid=(2, n_slab // LANES),
        in_specs=[blk],
        out_specs=blk,
        out_shape=jax.ShapeDtypeStruct(a.shape, F32),
        scratch_shapes=[pltpu.VMEM((c * LANES, LANES), F32), pltpu.VMEM((c * LANES, LANES), F32),
                        pltpu.VMEM((c, LANES, LANES), F32)],
        compiler_params=_cparams(("parallel", "parallel")),
        name="gdn_solve",
    )(a)


def _gdn_scan_kernel(tf, rvf, rkf, qkf, qef, kef, dlf, tb, rvb, rkb, qkb, qeb, keb, dlb, of_ref, ob_ref, s_ref):
    c = GDN_CHUNK
    cpt = rvf.shape[1] // c
    ng = rvf.shape[2] // PACK_W
    ppg = PACK_W // LANES
    bd_mask = jnp.right_shift(_iota((PACK_W, PACK_W), 0), 6) == jnp.right_shift(_iota((PACK_W, PACK_W), 1), 6)

    @pl.when(pl.program_id(1) == 0)
    def _():
        s_ref[...] = jnp.zeros_like(s_ref)

    chains = ((tf, rvf, rkf, qkf, qef, kef, dlf, of_ref), (tb, rvb, rkb, qkb, qeb, keb, dlb, ob_ref))

    def jobs(j):
        out = []
        for d in range(2):
            ch = j if d == 0 else cpt - 1 - j
            for gi in range(ng):
                out.append((d, gi, ch, slice(ch * c, (ch + 1) * c), slice(gi * PACK_W, (gi + 1) * PACK_W)))
        return out

    factors = []
    for j in range(cpt):
        where = jobs(j)
        tinvs = [jnp.concatenate([chains[d][0][0, ch * ng * ppg + gi * ppg + pp] for pp in range(ppg)],
                                 axis=1).astype(BF16) for d, gi, ch, rows, lanes in where]
        us = [jnp.dot(t16, _block_diag(chains[d][1][0, rows, lanes], bd_mask), preferred_element_type=F32)
              for (d, gi, ch, rows, lanes), t16 in zip(where, tinvs)]
        ws = [jnp.dot(t16, _block_diag(chains[d][2][0, rows, lanes], bd_mask),
                      preferred_element_type=F32).astype(BF16)
              for (d, gi, ch, rows, lanes), t16 in zip(where, tinvs)]
        factors.append((us, ws))

    for j in range(cpt):
        where = jobs(j)
        us, ws = factors[j]
        states = [s_ref[d * ng + gi] for d, gi, _, _, _ in where]
        wqs = [_mm(jnp.concatenate([w, chains[d][4][0, rows, lanes]], axis=0), st)
               for (d, gi, ch, rows, lanes), w, st in zip(where, ws, states)]
        v_news = [u - wq[0:c] for u, wq in zip(us, wqs)]
        kvs = [_mm_tn(chains[d][5][0, rows, lanes], v_new) for (d, gi, ch, rows, lanes), v_new in zip(where, v_news)]
        for (d, gi, ch, rows, lanes), st, kv in zip(where, states, kvs):
            s_ref[d * ng + gi] = st * chains[d][6][0, ch, :, lanes] + jnp.where(bd_mask, kv, 0.0)
        for (d, gi, ch, rows, lanes), wq, v_new in zip(where, wqs, v_news):
            chains[d][7][0, rows, lanes] = wq[c:2 * c] + _mm(chains[d][3][0, rows, lanes],
                                                             _block_diag(v_new, bd_mask))


def _gdn_scan(tinv, rv, rk, qk, qe, ke, dl, n_ctx):
    b, t, wide = rv.shape
    tm = TOKEN_TILE
    nt = t // tm
    nctt = n_ctx // tm
    hw = wide // 2
    c = GDN_CHUNK
    cpt = tm // c
    spt = cpt * (hw // LANES)

    def mirror(g):
        return jnp.where(g < nctt, nctt - 1 - g, nt - 1 - (g - nctt))

    fwd = pl.BlockSpec((1, tm, hw), lambda i, g: (i, g, 0))
    bwd = pl.BlockSpec((1, tm, hw), lambda i, g: (i, mirror(g), 1))
    tfwd = pl.BlockSpec((1, spt, c, LANES), lambda i, g: (0, i * nt + g, 0, 0))
    tbwd = pl.BlockSpec((1, spt, c, LANES), lambda i, g: (1, i * nt + mirror(g), 0, 0))
    dfwd = pl.BlockSpec((1, cpt, 1, hw), lambda i, g: (i, g, 0, 0))
    dbwd = pl.BlockSpec((1, cpt, 1, hw), lambda i, g: (i, mirror(g), 0, 1))
    out_sds = jax.ShapeDtypeStruct((b, t, hw), F32)
    return pl.pallas_call(
        _gdn_scan_kernel,
        grid=(b, nt),
        in_specs=[tfwd] + [fwd] * 5 + [dfwd] + [tbwd] + [bwd] * 5 + [dbwd],
        out_specs=[pl.BlockSpec((1, tm, hw), lambda i, g: (i, g, 0)),
                   pl.BlockSpec((1, tm, hw), lambda i, g: (i, mirror(g), 0))],
        out_shape=[out_sds, out_sds],
        scratch_shapes=[pltpu.VMEM((2 * hw // PACK_W, PACK_W, PACK_W), F32)],
        compiler_params=_cparams(("parallel", "arbitrary")),
        name="gdn_scan",
    )(tinv, rv, rk, qk, qe, ke, dl, tinv, rv, rk, qk, qe, ke, dl)


def _merge_kernel(x_ref, mod_ref, ya_ref, yb_ref, of_ref, ob_ref, z_ref, gate_ref, on_ref,
                  wpa_ref, wpb_ref, wpc_ref, wout_ref, o_ref):
    x = x_ref[0]
    d = x.shape[-1]
    m = mod_ref[0]
    oc = of_ref[0] + ob_ref[0]
    ones_blk = _group_ones(LANES, GDN_DV).astype(BF16)
    ms = jnp.concatenate(
        [_group_sum(oc[:, s * LANES:(s + 1) * LANES] ** 2, ones_blk) for s in range(oc.shape[-1] // LANES)],
        axis=1) * (1.0 / GDN_DV)
    yc = oc * lax.rsqrt(ms + EPS) * on_ref[...] * _silu(z_ref[0].astype(F32))
    gate = gate_ref[0].astype(F32)
    mix = (_sigmoid(gate[:, 0:d]) * _mm(ya_ref[0], wpa_ref[...])
           + _sigmoid(gate[:, d:2 * d]) * _mm(yb_ref[0], wpb_ref[...])
           + _sigmoid(gate[:, 2 * d:3 * d]) * _mm(yc, wpc_ref[...]))
    o_ref[0] = x + m[:, 2 * d:3 * d] * _mm(mix, wout_ref[...])


def _merge(xs, mod_l, ya, yb, oc_f, oc_b, z, gate, onorm, wpa, wpb, wpc, wout, nctt, ctx_row, t_off):
    b, t, d = xs.shape
    tm = TOKEN_TILE
    nt = t // tm - t_off
    tok = lambda width: pl.BlockSpec((1, tm, width), lambda i, j: (i, j + t_off, 0))
    full = lambda a: pl.BlockSpec(a.shape, lambda i, j: (0,) * a.ndim, pipeline_mode=pl.Buffered(1))
    mod_spec = pl.BlockSpec((1, 1, mod_l.shape[-1]),
                            lambda i, j: (jnp.where(j + t_off < nctt, ctx_row, i), 0, 0))
    return pl.pallas_call(
        _merge_kernel,
        grid=(b, nt),
        in_specs=[tok(d), mod_spec, tok(512), tok(512), tok(512), tok(512), tok(512), tok(3 * d),
                  full(onorm), full(wpa), full(wpb), full(wpc), full(wout)],
        out_specs=pl.BlockSpec((1, tm, d), lambda i, j: (i, j, 0)),
        out_shape=jax.ShapeDtypeStruct((b, nt * tm, d), F32),
        compiler_params=_cparams(("parallel", "parallel")),
        name="merge",
    )(xs, mod_l, ya, yb, oc_f, oc_b, z, gate, onorm, wpa, wpb, wpc, wout)


def _ffn_kernel(x_ref, prev_ref, next_ref, mod_ref, g2_ref, wup_ref, cw_ref, wdn_ref, nf_ref, o_ref,
                *, seg_starts, seg_ends, final):
    ti = pl.program_id(1)
    x = x_ref[0]
    tm, d = x.shape
    m = mod_ref[0]
    xall = jnp.concatenate([prev_ref[0], x, next_ref[0]], axis=0)
    xa = _rms(xall, g2_ref[...]) * (1.0 + m[:, 4 * d:5 * d]) + m[:, 3 * d:4 * d]
    n_all = tm + 2 * SUBLANES
    row = _iota((n_all, 1), 0)
    is_start = functools.reduce(jnp.logical_or, [ti == s for s in seg_starts])
    is_end = functools.reduce(jnp.logical_or, [ti == s for s in seg_ends])
    drop = jnp.logical_or(jnp.logical_and(row == SUBLANES - 1, is_start),
                          jnp.logical_and(row == SUBLANES + tm, is_end))
    xa = jnp.where(drop, 0.0, xa).astype(BF16)

    def conv(h, cols):
        hp = pltpu.roll(h, 1, 0)[SUBLANES:SUBLANES + tm]
        hn = pltpu.roll(h, n_all - 1, 0)[SUBLANES:SUBLANES + tm]
        return (hp * cw_ref[0:1, cols] + h[SUBLANES:SUBLANES + tm] * cw_ref[1:2, cols]
                + hn * cw_ref[2:3, cols])

    cwid = FFN_DIM // FFN_SPLIT
    acc = jnp.zeros((tm, d), F32)
    for j in range(FFN_SPLIT):
        ca = slice(j * cwid, (j + 1) * cwid)
        cb = slice(FFN_DIM + j * cwid, FFN_DIM + (j + 1) * cwid)
        ha = conv(jnp.dot(xa, wup_ref[:, ca], preferred_element_type=F32), ca)
        hb = conv(jnp.dot(xa, wup_ref[:, cb], preferred_element_type=F32), cb)
        acc = acc + _mm(_silu(ha) * hb, wdn_ref[ca, :])
    y = x + m[:, 5 * d:6 * d] * acc
    if final:
        y = _rms(y, nf_ref[...])
    o_ref[0] = y


def _ffn(x1, mod_l, g2, wup, cw, wdn, nf, nctt_mod, ctx_row, seg_starts, seg_ends, final):
    b, t, d = x1.shape
    tm = TOKEN_TILE
    nt = t // tm
    hb = tm // SUBLANES
    nh = t // SUBLANES
    full = lambda a: pl.BlockSpec(a.shape, lambda i, j: (0,) * a.ndim, pipeline_mode=pl.Buffered(1))
    mod_spec = pl.BlockSpec((1, 1, mod_l.shape[-1]),
                            lambda i, j: (jnp.where(j < nctt_mod, ctx_row, i), 0, 0))
    return pl.pallas_call(
        functools.partial(_ffn_kernel, seg_starts=seg_starts, seg_ends=seg_ends, final=final),
        grid=(b, nt),
        in_specs=[pl.BlockSpec((1, tm, d), lambda i, j: (i, j, 0)),
                  pl.BlockSpec((1, SUBLANES, d), lambda i, j: (i, jnp.maximum(j * hb - 1, 0), 0)),
                  pl.BlockSpec((1, SUBLANES, d), lambda i, j: (i, jnp.minimum((j + 1) * hb, nh - 1), 0)),
                  mod_spec, full(g2), full(wup), full(cw), full(wdn), full(nf)],
        out_specs=pl.BlockSpec((1, tm, d), lambda i, j: (i, j, 0)),
        out_shape=jax.ShapeDtypeStruct((b, t, d), F32),
        compiler_params=_cparams(("parallel", "parallel")),
        name="ffn",
    )(x1, x1, x1, mod_l, g2, wup, cw, wdn, nf)


def _cols(w, segments):
    parts = [jnp.zeros(w.shape[:-1] + (seg,), w.dtype) if isinstance(seg, int) else w[..., seg[0]:seg[1]]
             for seg in segments]
    return jnp.concatenate(parts, axis=-1)


def _win_segments(d):
    widths = [MLA_Q_RANK, MLA_KV_RANK, MLA_ROPE, SWA_HEADS * SWA_HEAD_DIM, SWA_KV_HEADS * SWA_HEAD_DIM,
              SWA_KV_HEADS * SWA_HEAD_DIM, GDN_HEADS * (2 * GDN_DK + GDN_DV), GDN_HEADS * GDN_DV,
              2 * GDN_HEADS, 2 * GDN_HEADS, 3 * d]
    off = np.cumsum([0] + widths)
    cq, ckv, kr, sq, sk, sv, qkv, z, a, b, gate = [(int(off[i]), int(off[i + 1])) for i in range(11)]
    sq_perm = [(sq[0] + h * SWA_HEAD_DIM, sq[0] + (h + 1) * SWA_HEAD_DIM) for h in _swa_head_order()]
    segs = [cq, ckv, MLA_NOPE, kr, LANES - MLA_NOPE - MLA_ROPE] + sq_perm + \
           [sk, sv, qkv, z, a, b, LANES - 4 * GDN_HEADS, gate]
    return segs


def _swa_head_order():
    group = SWA_HEADS // SWA_KV_HEADS
    return [kv * group + g for g in range(group) for kv in range(SWA_KV_HEADS)]


def _rope_tables(n_ctx, n_lat):
    t = np.arange(n_lat)
    pos = np.stack([t // GRID_W, t % GRID_W]).astype(np.float32)

    def build(head_w, rope_lo, rope_w):
        da = rope_w // 2
        half = da // 2
        inv = ROPE_THETA ** (-np.arange(half, dtype=np.float32) / half)
        c = np.ones((n_lat, LANES), np.float32)
        sa = np.zeros((n_lat, LANES), np.float32)
        sb = np.zeros((n_lat, LANES), np.float32)
        for lane in range(LANES):
            dd = lane % head_w - rope_lo
            if dd < 0 or dd >= rope_w:
                continue
            axis, r = dd // da, dd % da
            ang = pos[axis] * inv[r % half]
            c[:, lane] = np.cos(ang)
            if r < half:
                sa[:, lane] = -np.sin(ang)
            else:
                sb[:, lane] = np.sin(ang)
        ctx = [np.ones((n_ctx, LANES), np.float32), np.zeros((n_ctx, LANES), np.float32),
               np.zeros((n_ctx, LANES), np.float32)]
        return [jnp.asarray(np.concatenate([cx, lt])) for cx, lt in zip(ctx, (c, sa, sb))]

    return build(LANES, MLA_NOPE, MLA_ROPE) + build(SWA_HEAD_DIM, 0, SWA_HEAD_DIM)


def kernel(x, c, ctx, c_ctx, w_mod, b_mod, norm1, norm2, w_in, mla_q_norm, mla_kv_norm, w_uq, w_ukv,
           swa_sink, gdn_conv, gdn_a_log, gdn_dt_bias, gdn_norm, w_branch_a, w_branch_b, w_branch_c,
           w_out, ffn_up, ffn_conv, ffn_down, norm_f):
    b, s, d = x.shape
    n_ctx = ctx.shape[1]
    depth = w_mod.shape[0]
    tm = TOKEN_TILE
    assert n_ctx % tm == 0 and s % tm == 0 and s >= SWA_TILE + 2 * WINDOW and WINDOW % LANES == 0
    nctt = n_ctx // tm
    ntt = (n_ctx + s) // tm

    rows = -(-(b + 1) // SUBLANES) * SUBLANES
    cc = jnp.zeros((rows, d), F32).at[0:b].set(c).at[b].set(c_ctx)
    mod = _modulation(cc, w_mod, b_mod).reshape(depth, rows, 1, 6 * d)

    hd_q = MLA_NOPE + MLA_ROPE
    hd_kv = MLA_NOPE + MLA_V
    uq_segs = [s for h in range(MLA_HEADS) for s in ((h * hd_q, (h + 1) * hd_q), LANES - hd_q)]
    uk_segs = [s for h in range(MLA_HEADS) for s in ((h * hd_kv, h * hd_kv + MLA_NOPE), LANES - MLA_NOPE)]
    uv_segs = [(h * hd_kv + MLA_NOPE, (h + 1) * hd_kv) for h in range(MLA_HEADS)]
    win_segs = _win_segments(d)

    def layer_weights(l):
        wpb = jnp.concatenate([w_branch_b[l, h * SWA_HEAD_DIM:(h + 1) * SWA_HEAD_DIM] for h in _swa_head_order()],
                              axis=0)
        gpar = jnp.zeros((SUBLANES, LANES), F32)
        gpar = gpar.at[0, 0:2 * GDN_HEADS].set(gdn_a_log[l].reshape(-1))
        gpar = gpar.at[1, 0:2 * GDN_HEADS].set(gdn_dt_bias[l].reshape(-1))
        pad_rows = lambda a: jnp.pad(a, ((0, SUBLANES - a.shape[0]), (0, 0)))
        return dict(
            win=_cols(w_in[l], win_segs).astype(BF16),
            wuq=_cols(w_uq[l], uq_segs).astype(BF16),
            wukvk=_cols(w_ukv[l], uk_segs).astype(BF16),
            wukvv=_cols(w_ukv[l], uv_segs).astype(BF16),
            wpa=w_branch_a[l].astype(BF16), wpb=wpb.astype(BF16), wpc=w_branch_c[l].astype(BF16),
            wout=w_out[l].astype(BF16), wup=ffn_up[l].astype(BF16), wdn=ffn_down[l].astype(BF16),
            sink=jnp.broadcast_to(swa_sink[l][:, None], (SWA_HEADS, LANES)),
            onorm=jnp.tile(gdn_norm[l], GDN_HEADS).reshape(1, -1),
            gconv=pad_rows(gdn_conv[l]), fconv=pad_rows(ffn_conv[l]), gpar=gpar)

    rope_tabs = _rope_tables(n_ctx, s)

    xs = jnp.concatenate([ctx, x], axis=1)
    row2 = lambda a: a.reshape(1, -1)
    for l in range(depth):
        last = l == depth - 1
        t_off = nctt if last else 0
        p = layer_weights(l)
        qm, km, vm, sq, sk, sv, qkv, z, ab, gate = _proj(
            xs, mod[l], row2(norm1[l]), p["win"], row2(mla_q_norm[l]), row2(mla_kv_norm[l]),
            p["wuq"], p["wukvk"], p["wukvv"], rope_tabs, nctt, b)
        ya = _mla(qm, km, vm, n_ctx, last)
        yb = _swa(p["sink"], sq, sk, sv, n_ctx, last)
        a_mat, rv, rk, qk, qe, ke, dl = _gdn_prep(qkv, ab, p["gconv"], p["gpar"], (0, nctt), (nctt - 1, ntt - 1))
        oc_f, oc_b = _gdn_scan(_gdn_solve(a_mat), rv, rk, qk, qe, ke, dl, n_ctx)
        x1 = _merge(xs, mod[l], ya, yb, oc_f, oc_b, z, gate, p["onorm"], p["wpa"], p["wpb"], p["wpc"], p["wout"],
                    nctt, b, t_off)
        if last:
            xs = _ffn(x1, mod[l], row2(norm2[l]), p["wup"], p["fconv"], p["wdn"], row2(norm_f),
                      0, b, (0,), (s // tm - 1,), True)
        else:
            xs = _ffn(x1, mod[l], row2(norm2[l]), p["wup"], p["fconv"], p["wdn"], row2(norm_f),
                      nctt, b, (0, nctt), (nctt - 1, ntt - 1), False)
    return xs
```
